```python
import math
import jax, jax.numpy as jnp
from jax import lax
import numpy as np

D_MODEL = 2048
BATCH = 1
SEQ = 16384
DEPTH = 1

GRID_W = 64
CTX_LEN = 256
EPS = 1e-6

SSM_HEADS = 32
SSM_HEAD_DIM = 64
SSM_INNER = SSM_HEADS * SSM_HEAD_DIM
SSM_GROUPS = 4
SSM_STATE = 128
SSM_CONV = 4
SSM_CHUNK = 128
CONV_DIM = SSM_INNER + 2 * SSM_GROUPS * SSM_STATE

HGRN_HEADS = 16
HGRN_KEY_DIM = 128
HGRN_VAL_DIM = D_MODEL // HGRN_HEADS
HGRN_FDIM = HGRN_HEADS * HGRN_KEY_DIM
HGRN_INNER = HGRN_HEADS * HGRN_VAL_DIM
HGRN_CHUNK = 64

N_EXPERTS = 32
TOP_K = 4
D_FF = 2048
SWIGLU_LIMIT = 7.0
SWIGLU_ALPHA = 1.702
MOE_BLOCK = 256

IN_SIZES = (SSM_INNER, CONV_DIM, 2 * SSM_HEADS, HGRN_FDIM, 2 * HGRN_FDIM, HGRN_INNER, HGRN_INNER, 2 * D_MODEL)
D_IN = sum(IN_SIZES)
IN_OFFSETS = tuple(sum(IN_SIZES[:k + 1]) for k in range(len(IN_SIZES) - 1))

kernel_name = 'hybrid_ssd_hgrn2_moe_dit_block'


def rmsnorm(x, w):
    x32 = x.astype(jnp.float32)
    y = x32 * lax.rsqrt(jnp.mean(x32 * x32, axis=-1, keepdims=True) + EPS)
    return (y * w.astype(jnp.float32)).astype(x.dtype)


def modulate(xn, shift, scale):
    return xn * (1 + scale) + shift


def centred_dwconv(u, w, b, n_rows):
    bsz, L, C = u.shape
    row_len = L // n_rows
    r = u.reshape(bsz * n_rows, row_len, C)
    left = SSM_CONV // 2
    rp = jnp.pad(r, ((0, 0), (left, SSM_CONV - 1 - left), (0, 0)))
    out = sum(rp[:, k:k + row_len] * w[k] for k in range(SSM_CONV)) + b
    return out.reshape(bsz, L, C)


def ssd_chunked(xdt, loga, Bm, Cm, s0):
    bsz, L, H, P = xdt.shape
    G, N = Bm.shape[2], Bm.shape[3]
    J = H // G
    Q = SSM_CHUNK
    nc = L // Q
    x = xdt.reshape(bsz, nc, Q, G, J, P)
    Bc = Bm.reshape(bsz, nc, Q, G, N)
    Cc = Cm.reshape(bsz, nc, Q, G, N)
    acs = jnp.cumsum(loga.reshape(bsz, nc, Q, G, J).transpose(0, 3, 4, 1, 2), axis=-1)
    causal = jnp.tril(jnp.ones((Q, Q), bool))
    decay = jnp.exp(jnp.where(causal, acs[..., :, None] - acs[..., None, :], -jnp.inf))
    cb = jnp.einsum('bctgn,bcsgn->bgcts', Cc, Bc)
    y_diag = jnp.einsum('bgcts,bgjcts,bcsgjp->bctgjp', cb, decay, x)
    to_end = jnp.exp(acs[..., -1:] - acs)
    states = jnp.einsum('bcsgn,bgjcs,bcsgjp->cbgjpn', Bc, to_end, x)
    chunk_decay = jnp.exp(acs[..., -1]).transpose(3, 0, 1, 2)

    def step(S, inp):
        st, dec = inp
        return S * dec[..., None, None] + st, S

    S_fin, S_in = lax.scan(step, s0.reshape(bsz, G, J, P, N), (states, chunk_decay))
    y_off = jnp.einsum('bctgn,cbgjpn,bgjct->bctgjp', Cc, S_in, jnp.exp(acs))
    y = (y_diag + y_off).reshape(bsz, L, H, P)
    return y, S_fin.reshape(bsz, H, P, N)


def hgrn2_chunked(q, k, v, logf, s0):
    bsz, L, H, K = q.shape
    V = v.shape[-1]
    Q = HGRN_CHUNK
    nc = L // Q

    def to_chunks(t):
        return t.reshape(bsz, nc, Q, H, t.shape[-1]).transpose(1, 0, 3, 2, 4)

    causal = jnp.tril(jnp.ones((Q, Q), bool))[:, :, None]

    def step(S, inp):
        qc, kc, vc, gc = inp
        bcs = jnp.cumsum(gc, axis=2)
        o_inter = jnp.einsum('bhtk,bhkv->bhtv', qc * jnp.exp(bcs), S)
        rel = jnp.exp(jnp.where(causal, bcs[:, :, :, None, :] - bcs[:, :, None, :, :], -jnp.inf))
        att = jnp.einsum('bhtk,bhtsk,bhsk->bhts', qc, rel, kc)
        o = o_inter + jnp.einsum('bhts,bhsv->bhtv', att, vc)
        last = bcs[:, :, -1:]
        S = jnp.exp(last[:, :, 0])[..., None] * S + jnp.einsum('bhsk,bhsv->bhkv', kc * jnp.exp(last - bcs), vc)
        return S, o

    S_fin, o = lax.scan(step, s0, (to_chunks(q), to_chunks(k), to_chunks(v), to_chunks(logf)))
    return o.transpose(1, 0, 3, 2, 4).reshape(bsz, L, H, V), S_fin


def mamba2_branch(z, xbc, dt_raw, n_rows, conv_w, conv_b, dt_bias, a_log, d_skip, norm_w, s0):
    f32 = jnp.float32
    bsz, L, _ = z.shape
    xbc = jax.nn.silu(centred_dwconv(xbc, conv_w, conv_b, n_rows)).astype(f32)
    xs, Bm, Cm = jnp.split(xbc, [SSM_INNER, SSM_INNER + SSM_GROUPS * SSM_STATE], axis=-1)
    xs = xs.reshape(bsz, L, SSM_HEADS, SSM_HEAD_DIM)
    Bm = Bm.reshape(bsz, L, SSM_GROUPS, SSM_STATE)
    Cm = Cm.reshape(bsz, L, SSM_GROUPS, SSM_STATE)
    dt = jax.nn.softplus(dt_raw.astype(f32).reshape(bsz, L, 2, SSM_HEADS) + dt_bias.astype(f32))
    A = -jnp.exp(a_log.astype(f32))
    y_f, s_f = ssd_chunked(xs * dt[:, :, 0, :, None], A[0] * dt[:, :, 0], Bm, Cm, s0[0])
    flip = lambda t: jnp.flip(t, axis=1)
    y_b, s_b = ssd_chunked(flip(xs * dt[:, :, 1, :, None]), flip(A[1] * dt[:, :, 1]), flip(Bm), flip(Cm), s0[1])
    y = y_f + flip(y_b) + d_skip.astype(f32)[:, None] * xs
    gsz = SSM_INNER // SSM_GROUPS
    y = y.reshape(bsz, L, SSM_GROUPS, gsz) * jax.nn.silu(z.astype(f32)).reshape(bsz, L, SSM_GROUPS, gsz)
    y = y * lax.rsqrt(jnp.mean(y * y, axis=-1, keepdims=True) + EPS) * norm_w.astype(f32).reshape(SSM_GROUPS, gsz)
    return y.reshape(bsz, L, SSM_INNER).astype(z.dtype), (s_f, s_b)


def hgrn2_branch(q, f_raw, i, g, lb, norm_w, s0):
    f32 = jnp.float32
    bsz, L, _ = q.shape
    H, K, V = HGRN_HEADS, HGRN_KEY_DIM, HGRN_VAL_DIM
    q32 = q.astype(f32).reshape(bsz, L, H, K) * (K ** -0.5)
    v32 = jax.nn.silu(i.astype(f32)).reshape(bsz, L, H, V)
    fr = f_raw.astype(f32).reshape(bsz, L, 2, H, K)
    lbh = lb.reshape(2, H, K)
    logf = jnp.log(lbh + (1 - lbh) * jax.nn.sigmoid(fr))
    kin = (1 - lbh) * jax.nn.sigmoid(-fr)
    o_f, s_f = hgrn2_chunked(q32, kin[:, :, 0], v32, logf[:, :, 0], s0[0])
    flip = lambda t: jnp.flip(t, axis=1)
    o_b, s_b = hgrn2_chunked(flip(q32), flip(kin[:, :, 1]), flip(v32), flip(logf[:, :, 1]), s0[1])
    o = o_f + flip(o_b)
    o = o * lax.rsqrt(jnp.mean(o * o, axis=-1, keepdims=True) + EPS) * norm_w.astype(f32)
    o = o.reshape(bsz, L, HGRN_INNER) * jax.nn.silu(g.astype(f32))
    return o.astype(q.dtype), (s_f, s_b)


def mixer_branches(hn, n_rows, init, w_in, conv_w, conv_b, dt_bias, a_log, d_skip, ssm_norm_w, lb, hgrn_norm_w):
    proj = hn @ w_in
    z, xbc, dt_raw, q, f_raw, i, g, merge = jnp.split(proj, IN_OFFSETS, axis=-1)
    y_ssm, ssm_states = mamba2_branch(z, xbc, dt_raw, n_rows, conv_w, conv_b, dt_bias, a_log, d_skip, ssm_norm_w, init[0])
    y_hgrn, hgrn_states = hgrn2_branch(q, f_raw, i, g, lb, hgrn_norm_w, init[1])
    return y_ssm, y_hgrn, merge, (ssm_states, hgrn_states)


def merge_branches(y_ssm, y_hgrn, merge, w_bs, w_bh, w_o):
    g_s, g_h = jnp.split(merge, 2, axis=-1)
    y = jax.nn.sigmoid(g_s) * (y_ssm @ w_bs) + jax.nn.sigmoid(g_h) * (y_hgrn @ w_bh)
    return y @ w_o


def moe_ffn(h, w_r, b_r, w_gu, b_gu, w_dn, b_dn):
    T, D = h.shape
    logits = (h @ w_r + b_r).astype(jnp.float32)
    top_v, top_i = lax.top_k(logits, TOP_K)
    probs = jax.nn.softmax(top_v, axis=-1)
    n_assign = T * TOP_K
    e_flat = top_i.reshape(-1)
    order = jnp.argsort(e_flat)
    e_sorted = e_flat[order]
    tok_sorted = (order // TOP_K).astype(jnp.int32)
    p_sorted = probs.reshape(-1)[order]
    counts = jnp.bincount(e_flat, length=N_EXPERTS)
    padded = (counts + MOE_BLOCK - 1) // MOE_BLOCK * MOE_BLOCK
    pad_end = jnp.cumsum(padded)
    pad_start = pad_end - padded
    start = jnp.cumsum(counts) - counts
    dest = pad_start[e_sorted] + jnp.arange(n_assign) - start[e_sorted]
    n_blocks = -(-(n_assign + N_EXPERTS * (MOE_BLOCK - 1)) // MOE_BLOCK)
    n_slots = n_blocks * MOE_BLOCK
    row_tok = jnp.full((n_slots,), T, jnp.int32).at[dest].set(tok_sorted)
    row_p = jnp.zeros((n_slots,), h.dtype).at[dest].set(p_sorted.astype(h.dtype))
    blk_e = jnp.minimum(jnp.searchsorted(pad_end, jnp.arange(n_blocks) * MOE_BLOCK, side='right'), N_EXPERTS - 1)
    h_pad = jnp.concatenate([h, jnp.zeros((1, D), h.dtype)], axis=0)
    xb = h_pad[row_tok].reshape(n_blocks, MOE_BLOCK, D)

    def expert_block(args):
        xe, e = args
        gu = xe @ w_gu[e] + b_gu[e]
        gate, up = gu[:, :D_FF], gu[:, D_FF:]
        gate = jnp.minimum(gate, SWIGLU_LIMIT)
        up = jnp.clip(up, -SWIGLU_LIMIT, SWIGLU_LIMIT)
        glu = gate * jax.nn.sigmoid(SWIGLU_ALPHA * gate)
        return ((up + 1) * glu) @ w_dn[e] + b_dn[e]

    yb = lax.map(expert_block, (xb, blk_e)).reshape(n_slots, D)
    out = jnp.zeros((T + 1, D), h.dtype).at[row_tok].add(yb * row_p[:, None])
    return out[:T]


def setup_inputs(seed: int = 0) -> dict:
    key = jax.random.key(seed)
    ks = jax.random.split(key, 32)
    f32 = jnp.float32
    D = D_MODEL
    nrm = lambda k, shape, s: jax.random.normal(k, shape, f32) * s
    dt0 = jnp.exp(jax.random.uniform(ks[10], (DEPTH, 2, SSM_HEADS), f32, math.log(1e-3), math.log(1e-1)))
    return {
        'x': nrm(ks[0], (BATCH, SEQ, D), 1.0),
        'c': nrm(ks[1], (BATCH, D), 1.0),
        'ctx': nrm(ks[2], (BATCH, CTX_LEN, D), 1.0),
        'c_ctx': nrm(ks[3], (D,), 1.0),
        'w_ada': nrm(ks[4], (DEPTH, D, 6 * D), 0.5 * D ** -0.5),
        'b_ada': nrm(ks[5], (DEPTH, 6 * D), 0.02),
        'norm1_w': 1.0 + nrm(ks[6], (DEPTH, D), 0.02),
        'norm2_w': 1.0 + nrm(ks[7], (DEPTH, D), 0.02),
        'w_in': nrm(ks[8], (DEPTH, D, D_IN), D ** -0.5),
        'conv_w': nrm(ks[9], (DEPTH, SSM_CONV, CONV_DIM), SSM_CONV ** -0.5),
        'conv_b': nrm(ks[11], (DEPTH, CONV_DIM), 0.02),
        'dt_bias': dt0 + jnp.log(-jnp.expm1(-dt0)),
        'a_log': jnp.log(jax.random.uniform(ks[12], (DEPTH, 2, SSM_HEADS), f32, 1.0, 16.0)),
        'd_skip': 1.0 + nrm(ks[13], (DEPTH, SSM_HEADS), 0.1),
        'ssm_norm_w': 1.0 + nrm(ks[14], (DEPTH, SSM_INNER), 0.02),
        'hgrn_lb': nrm(ks[15], (2, DEPTH + 1, HGRN_FDIM), 1.0),
        'hgrn_norm_w': 1.0 + nrm(ks[16], (DEPTH, HGRN_VAL_DIM), 0.02),
        'w_branch_ssm': nrm(ks[17], (DEPTH, SSM_INNER, D), SSM_INNER ** -0.5),
        'w_branch_hgrn': nrm(ks[18], (DEPTH, HGRN_INNER, D), HGRN_INNER ** -0.5),
        'w_out': nrm(ks[19], (DEPTH, D, D), D ** -0.5),
        'w_router': nrm(ks[20], (DEPTH, D, N_EXPERTS), D ** -0.5),
        'b_router': nrm(ks[21], (DEPTH, N_EXPERTS), 0.01),
        'w_gate_up': nrm(ks[22], (DEPTH, N_EXPERTS, D, 2 * D_FF), D ** -0.5),
        'b_gate_up': nrm(ks[23], (DEPTH, N_EXPERTS, 2 * D_FF), 0.02),
        'w_down': nrm(ks[24], (DEPTH, N_EXPERTS, D_FF, D), D_FF ** -0.5),
        'b_down': nrm(ks[25], (DEPTH, N_EXPERTS, D), 0.02),
        'final_norm_w': 1.0 + nrm(ks[26], (D,), 0.02),
    }


def reference(x, c, ctx, c_ctx, w_ada, b_ada, norm1_w, norm2_w, w_in, conv_w, conv_b, dt_bias, a_log, d_skip, ssm_norm_w, hgrn_lb, hgrn_norm_w, w_branch_ssm, w_branch_hgrn, w_out, w_router, b_router, w_gate_up, b_gate_up, w_down, b_down, final_norm_w):
    f32 = jnp.float32
    bsz, seq, d = x.shape
    rows = seq // GRID_W
    ctx_len = ctx.shape[1]
    lower_bounds = jnp.cumsum(jax.nn.softmax(hgrn_lb.astype(f32), axis=1), axis=1)
    zs = jnp.zeros((bsz, SSM_HEADS, SSM_HEAD_DIM, SSM_STATE), f32)
    zh = jnp.zeros((bsz, HGRN_HEADS, HGRN_KEY_DIM, HGRN_VAL_DIM), f32)
    zero_init = ((zs, zs), (zh, zh))
    h, hc = x, ctx
    for l in range(DEPTH):
        mod = (jax.nn.silu(c) @ w_ada[l] + b_ada[l])[:, None, :]
        mod_c = (jax.nn.silu(c_ctx) @ w_ada[l] + b_ada[l])[None, None, :]
        sh1, sc1, g1, sh2, sc2, g2 = jnp.split(mod, 6, axis=-1)
        csh1, csc1, cg1, csh2, csc2, cg2 = jnp.split(mod_c, 6, axis=-1)
        mix_p = (w_in[l], conv_w[l], conv_b[l], dt_bias[l], a_log[l], d_skip[l], ssm_norm_w[l], lower_bounds[:, l], hgrn_norm_w[l])
        out_p = (w_branch_ssm[l], w_branch_hgrn[l], w_out[l])
        ffn_p = (w_router[l], b_router[l], w_gate_up[l], b_gate_up[l], w_down[l], b_down[l])
        hn_c = modulate(rmsnorm(hc, norm1_w[l]), csh1, csc1)
        ys_c, yh_c, gates_c, ctx_states = mixer_branches(hn_c, 1, zero_init, *mix_p)
        hn = modulate(rmsnorm(h, norm1_w[l]), sh1, sc1)
        ys, yh, gates, _ = mixer_branches(hn, rows, ctx_states, *mix_p)
        h = h + g1 * merge_branches(ys, yh, gates, *out_p)
        hn2 = modulate(rmsnorm(h, norm2_w[l]), sh2, sc2)
        h = h + g2 * moe_ffn(hn2.reshape(-1, d), *ffn_p).reshape(bsz, seq, d)
        if l < DEPTH - 1:
            hc = hc + cg1 * merge_branches(ys_c, yh_c, gates_c, *out_p)
            hn2_c = modulate(rmsnorm(hc, norm2_w[l]), csh2, csc2)
            hc = hc + cg2 * moe_ffn(hn2_c.reshape(-1, d), *ffn_p).reshape(bsz, ctx_len, d)
    return rmsnorm(h, final_norm_w)
```

```python
import functools

import jax
import jax.numpy as jnp
from jax import lax
from jax.experimental import pallas as pl
from jax.experimental.pallas import tpu as pltpu

F32 = jnp.float32
BF16 = jnp.bfloat16
I32 = jnp.int32
HI = lax.Precision.HIGHEST

EPS = 1e-6
GRID_W = 64

LANES = 128
SSM_HEADS = 32
SSM_HEAD_DIM = 64
SSM_GROUPS = 4
SSM_STATE = 128
SSM_INNER = SSM_HEADS * SSM_HEAD_DIM
SSM_GROUP_W = SSM_INNER // SSM_GROUPS
SSM_CHUNK = 128
HGRN_HEADS = 16
HGRN_K = 128
HGRN_V = 128
HGRN_CHUNK = 64
HGRN_SUB = 16
N_EXPERTS = 32
TOP_K = 4
SWIGLU_LIMIT = 7.0
SWIGLU_ALPHA = 1.702
MOE_BM = 512
MOE_TF = 512
VMEM_LIMIT = 56 * 1024 * 1024

COL_Z, COL_Q, COL_F, COL_I, COL_G, COL_GS, COL_GH, COL_XS = 0, 1, 2, 4, 5, 6, 7, 8


def _sigmoid(x):
    return 1.0 / (1.0 + jnp.exp(-x))


def _cparams(sem):
    return pltpu.CompilerParams(dimension_semantics=sem, vmem_limit_bytes=VMEM_LIMIT)


def _mod_kernel(c_ref, w_ref, b_ref, o_ref):
    c = c_ref[...]
    s = c * _sigmoid(c)
    o_ref[...] = jnp.dot(s, w_ref[...], precision=HI, preferred_element_type=F32) + b_ref[...]


def _modulation(cc, w_ada, b_ada):
    d, n = w_ada.shape
    tn = 1024
    return pl.pallas_call(
        _mod_kernel,
        grid=(n // tn,),
        in_specs=[pl.BlockSpec((8, d), lambda j: (0, 0)),
                  pl.BlockSpec((d, tn), lambda j: (0, j)),
                  pl.BlockSpec((1, tn), lambda j: (0, j))],
        out_specs=pl.BlockSpec((8, tn), lambda j: (0, j)),
        out_shape=jax.ShapeDtypeStruct((8, n), F32),
        compiler_params=_cparams(("arbitrary",)),
        name="adaln_mod",
    )(cc, w_ada, b_ada.reshape(1, n))


def _inproj_kernel(x_ref, nw_ref, sh_ref, sc_ref, w_ref, wdt_ref, o_ref, odt_ref, hn_ref):
    @pl.when(pl.program_id(1) == 0)
    def _():
        x = x_ref[...]
        ms = jnp.mean(x * x, axis=-1, keepdims=True)
        y = x * lax.rsqrt(ms + EPS) * nw_ref[...]
        hn = (y * (1.0 + sc_ref[...]) + sh_ref[...]).astype(BF16)
        hn_ref[...] = hn
        odt_ref[...] = jnp.dot(hn, wdt_ref[...], preferred_element_type=F32)

    o_ref[...] = jnp.dot(hn_ref[...], w_ref[...], preferred_element_type=F32)


def _inproj(x2, nw, sh, sc, w_main, w_dt):
    L, d = x2.shape
    n = w_main.shape[1]
    tm = min(512, L)
    tn = 1024
    return pl.pallas_call(
        _inproj_kernel,
        grid=(L // tm, n // tn),
        in_specs=[pl.BlockSpec((tm, d), lambda i, j: (i, 0)),
                  pl.BlockSpec((1, d), lambda i, j: (0, 0)),
                  pl.BlockSpec((1, d), lambda i, j: (0, 0)),
                  pl.BlockSpec((1, d), lambda i, j: (0, 0)),
                  pl.BlockSpec((d, tn), lambda i, j: (0, j)),
                  pl.BlockSpec((d, LANES), lambda i, j: (0, 0))],
        out_specs=[pl.BlockSpec((tm, tn), lambda i, j: (i, j)),
                   pl.BlockSpec((tm, LANES), lambda i, j: (i, 0))],
        out_shape=[jax.ShapeDtypeStruct((L, n), F32), jax.ShapeDtypeStruct((L, LANES), F32)],
        scratch_shapes=[pltpu.VMEM((tm, d), BF16)],
        compiler_params=_cparams(("parallel", "arbitrary")),
        name="inproj",
    )(x2, nw, sh, sc, w_main, w_dt)


def _conv_body(u, w_ref, b_ref, row_len):
    tb = u.shape[0]
    pos = lax.broadcasted_iota(I32, u.shape, 0) & (row_len - 1)
    acc = u * w_ref[2:3, :] + b_ref[...]
    for k, off in ((0, -2), (1, -1), (3, 1)):
        shifted = pltpu.roll(u, shift=(-off) % tb, axis=0)
        valid = (pos + off >= 0) & (pos + off < row_len)
        acc = acc + jnp.where(valid, shifted, 0.0) * w_ref[k:k + 1, :]
    return acc * _sigmoid(acc)


def _conv_kernel(ux_ref, ub_ref, wx_ref, wb_ref, bx_ref, bb_ref, ox_ref, ob_ref, *, row_len):
    ox_ref[...] = _conv_body(ux_ref[...], wx_ref, bx_ref, row_len)
    ob_ref[...] = _conv_body(ub_ref[...], wb_ref, bb_ref, row_len)


def _conv_silu(proj, conv_w, conv_b, row_len):
    L = proj.shape[0]
    d = SSM_INNER
    nbc = 2 * SSM_GROUPS * SSM_STATE
    tb = max(row_len, min(512, L))
    wx, wb = conv_w[:, :d], conv_w[:, d:]
    bx, bb = conv_b[:d].reshape(1, d), conv_b[d:].reshape(1, nbc)
    return pl.pallas_call(
        functools.partial(_conv_kernel, row_len=row_len),
        grid=(L // tb,),
        in_specs=[pl.BlockSpec((tb, d), lambda i: (i, COL_XS)),
                  pl.BlockSpec((tb, nbc), lambda i: (i, COL_XS * 2 + 2)),
                  pl.BlockSpec((4, d), lambda i: (0, 0)),
                  pl.BlockSpec((4, nbc), lambda i: (0, 0)),
                  pl.BlockSpec((1, d), lambda i: (0, 0)),
                  pl.BlockSpec((1, nbc), lambda i: (0, 0))],
        out_specs=[pl.BlockSpec((tb, d), lambda i: (i, 0)),
                   pl.BlockSpec((tb, nbc), lambda i: (i, 0))],
        out_shape=[jax.ShapeDtypeStruct((L, d), F32), jax.ShapeDtypeStruct((L, nbc), F32)],
        compiler_params=_cparams(("parallel",)),
        name="conv_silu",
    )(proj, proj, wx, wb, bx, bb)


def _ssd_kernel(xs_ref, bc_ref, dt_ref, dtb_ref, alog_ref, amask_ref, rep_ref, s0_ref,
                y_ref, sfin_ref, S_ref, *, rev, lane0):
    n = pl.program_id(0)

    @pl.when(n == 0)
    def _():
        S_ref[...] = s0_ref[...]

    Q = xs_ref.shape[0]
    N = SSM_STATE
    GW = SSM_GROUP_W
    ti = lax.broadcasted_iota(I32, (Q, Q), 0)
    si = lax.broadcasted_iota(I32, (Q, Q), 1)
    allowed = (si >= ti) if rev else (si <= ti)
    tri = allowed.astype(F32)

    raw = dt_ref[...] + dtb_ref[...]
    dt = jnp.maximum(raw, 0.0) + jnp.log1p(jnp.exp(-jnp.abs(raw)))
    loga = dt * (-jnp.exp(alog_ref[...]) * amask_ref[...])
    cs = jnp.dot(tri, loga, precision=HI, preferred_element_type=F32)
    csT = cs.T
    rep = rep_ref[...]
    cs_x = jnp.dot(cs, rep, precision=HI, preferred_element_type=F32)
    dt_x = jnp.dot(dt, rep, precision=HI, preferred_element_type=F32)
    tot_x = cs_x[0:1, :] if rev else cs_x[Q - 1:Q, :]
    xdt = xs_ref[...] * dt_x
    e_in = jnp.exp(cs_x)
    xw = (xdt * jnp.exp(tot_x - cs_x)).astype(BF16)
    xdt_b = xdt.astype(BF16)
    dec = jnp.exp(tot_x)
    lane = lax.broadcasted_iota(I32, (Q, LANES), 1)
    lo_mask = lane < SSM_HEAD_DIM

    for g in range(SSM_GROUPS):
        Bg = bc_ref[:, g * N:(g + 1) * N].astype(BF16)
        Cg = bc_ref[:, (SSM_GROUPS + g) * N:(SSM_GROUPS + g + 1) * N].astype(BF16)
        cb = lax.dot_general(Cg, Bg, (((1,), (1,)), ((), ())), preferred_element_type=F32)
        Sg = S_ref[g]
        y_off = jnp.dot(Cg, Sg.astype(BF16), preferred_element_type=F32) * e_in[:, g * GW:(g + 1) * GW]
        upd = lax.dot_general(Bg, xw[:, g * GW:(g + 1) * GW], (((0,), (0,)), ((), ())),
                              preferred_element_type=F32)
        S_ref[g] = Sg * dec[:, g * GW:(g + 1) * GW] + upd
        for jp in range(GW // LANES):
            h0 = g * (GW // SSM_HEAD_DIM) + 2 * jp
            ms = []
            for hh in (h0, h0 + 1):
                col = cs[:, lane0 + hh:lane0 + hh + 1]
                row = csT[lane0 + hh:lane0 + hh + 1, :]
                dec_ts = jnp.exp(jnp.where(allowed, col - row, -jnp.inf))
                ms.append((cb * dec_ts).astype(BF16))
            m2 = jnp.concatenate(ms, axis=1)
            x2 = xdt_b[:, h0 * SSM_HEAD_DIM:(h0 + 2) * SSM_HEAD_DIM]
            r2 = jnp.concatenate([jnp.where(lo_mask, x2, jnp.zeros_like(x2)),
                                  jnp.where(lo_mask, jnp.zeros_like(x2), x2)], axis=0)
            y_d = jnp.dot(m2, r2, preferred_element_type=F32)
            c0 = h0 * SSM_HEAD_DIM
            y_ref[:, c0:c0 + LANES] = y_d + y_off[:, jp * LANES:(jp + 1) * LANES]

    @pl.when(n == pl.num_programs(0) - 1)
    def _():
        sfin_ref[...] = S_ref[...]


def _ssd_scan(xs, bc, dt, dtb, alog, amask, rep, s0, rev):
    L, d = xs.shape
    Q = SSM_CHUNK
    nc = L // Q
    idx = (lambda n: (nc - 1 - n, 0)) if rev else (lambda n: (n, 0))
    const2 = lambda n: (0, 0)
    const3 = lambda n: (0, 0, 0)
    sshape = (SSM_GROUPS, SSM_STATE, SSM_GROUP_W)
    return pl.pallas_call(
        functools.partial(_ssd_kernel, rev=rev, lane0=SSM_HEADS if rev else 0),
        grid=(nc,),
        in_specs=[pl.BlockSpec((Q, d), idx),
                  pl.BlockSpec((Q, bc.shape[1]), idx),
                  pl.BlockSpec((Q, LANES), idx),
                  pl.BlockSpec((1, LANES), const2),
                  pl.BlockSpec((1, LANES), const2),
                  pl.BlockSpec((1, LANES), const2),
                  pl.BlockSpec((LANES, d), const2),
                  pl.BlockSpec(sshape, const3)],
        out_specs=[pl.BlockSpec((Q, d), idx),
                   pl.BlockSpec(sshape, const3)],
        out_shape=[jax.ShapeDtypeStruct((L, d), F32), jax.ShapeDtypeStruct(sshape, F32)],
        scratch_shapes=[pltpu.VMEM(sshape, F32)],
        compiler_params=_cparams(("arbitrary",)),
        name="ssd_bwd" if rev else "ssd_fwd",
    )(xs, bc, dt, dtb, alog, amask, rep, s0)


def _ssd_post_kernel(yf_ref, yb_ref, xs_ref, z_ref, dsk_ref, nw_ref, o_ref):
    z = z_ref[...]
    y = (yf_ref[...] + yb_ref[...] + dsk_ref[...] * xs_ref[...]) * (z * _sigmoid(z))
    for g in range(SSM_GROUPS):
        sl = slice(g * SSM_GROUP_W, (g + 1) * SSM_GROUP_W)
        yg = y[:, sl]
        ms = jnp.mean(yg * yg, axis=-1, keepdims=True)
        o_ref[:, sl] = (yg * lax.rsqrt(ms + EPS) * nw_ref[:, sl]).astype(o_ref.dtype)


def _ssd_post(yf, yb, xs, proj, dsk, nw):
    L, d = yf.shape
    tm = min(512, L)
    row = lambda i: (i, 0)
    const = lambda i: (0, 0)
    return pl.pallas_call(
        _ssd_post_kernel,
        grid=(L // tm,),
        in_specs=[pl.BlockSpec((tm, d), row), pl.BlockSpec((tm, d), row), pl.BlockSpec((tm, d), row),
                  pl.BlockSpec((tm, d), lambda i: (i, COL_Z)),
                  pl.BlockSpec((1, d), const), pl.BlockSpec((1, d), const)],
        out_specs=pl.BlockSpec((tm, d), row),
        out_shape=jax.ShapeDtypeStruct((L, d), BF16),
        compiler_params=_cparams(("parallel",)),
        name="ssd_post",
    )(yf, yb, xs, proj, dsk, nw)


def _hgrn_kernel(q_ref, f_ref, i_ref, lb_ref, s0_ref, o_ref, sfin_ref, ST_ref, *, rev):
    n = pl.program_id(0)

    @pl.when(n == 0)
    def _():
        ST_ref[...] = s0_ref[...]

    C = q_ref.shape[0]
    c = HGRN_SUB
    nsub = C // c
    K = HGRN_K
    ti = lax.broadcasted_iota(I32, (C, C), 0)
    si = lax.broadcasted_iota(I32, (C, C), 1)
    allowed = (si >= ti) if rev else (si <= ti)
    tri = allowed.astype(F32)
    bt, bs = ti // c, si // c
    dist = (bs - bt) if rev else (bt - bs)
    t16 = lax.broadcasted_iota(I32, (c, K), 0)
    ones_b = jnp.ones((K, K), BF16)
    order = list(range(nsub - 1, -1, -1)) if rev else list(range(nsub))

    for h in range(HGRN_HEADS):
        sl = slice(h * K, (h + 1) * K)
        lbr = lb_ref[:, sl]
        lmax = jnp.max(lbr, axis=0, keepdims=True)
        le = jnp.exp(lbr - lmax)
        lb = le[0:1, :] / jnp.sum(le, axis=0, keepdims=True)
        fr = f_ref[:, sl]
        e = jnp.exp(-jnp.abs(fr))
        rinv = 1.0 / (1.0 + e)
        pos = fr >= 0.0
        sg = jnp.where(pos, rinv, e * rinv)
        sgn = jnp.where(pos, e * rinv, rinv)
        logf = jnp.log(lb + (1.0 - lb) * sg)
        kk = (1.0 - lb) * sgn
        qq = q_ref[:, sl] * (K ** -0.5)
        iv = i_ref[:, sl]
        vv = iv * _sigmoid(iv)
        vb = vv.astype(BF16)
        bcs = jnp.dot(tri, logf, precision=HI, preferred_element_type=F32)
        tot = bcs[0:1, :] if rev else bcs[C - 1:C, :]

        ST = ST_ref[h]
        q_in = (qq * jnp.exp(bcs)).astype(BF16)
        o = lax.dot_general(q_in, ST.astype(BF16), (((1,), (1,)), ((), ())), preferred_element_type=F32)
        k_end = (kk * jnp.exp(tot - bcs)).astype(BF16)
        upd = lax.dot_general(vb, k_end, (((0,), (0,)), ((), ())), preferred_element_type=F32)
        ST_ref[h] = ST * jnp.exp(tot) + upd

        r_in, r_out = {}, {}
        for p, sc_i in enumerate(order):
            first = sc_i * c + (c - 1 if rev else 0)
            last = sc_i * c + (0 if rev else c - 1)
            r_in[p] = bcs[first:first + 1, :] - logf[first:first + 1, :]
            r_out[p] = bcs[last:last + 1, :]
        rows_in = [None] * nsub
        rows_out = [None] * nsub
        for p, sc_i in enumerate(order):
            rows_in[sc_i] = jnp.broadcast_to(r_in[p], (c, K))
            rows_out[sc_i] = jnp.broadcast_to(r_out[p], (c, K))
        R_in = jnp.concatenate(rows_in, axis=0)
        R_out = jnp.concatenate(rows_out, axis=0)
        qh = qq * jnp.exp(bcs - R_in)
        kt = (kk * jnp.exp(R_out - bcs)).astype(BF16)
        att = jnp.zeros((C, C), F32)
        for dd in range(1, nsub):
            rows_x = [None] * nsub
            for p, sc_i in enumerate(order):
                if p >= dd:
                    rows_x[sc_i] = jnp.broadcast_to(jnp.exp(r_in[p] - r_out[p - dd]), (c, K))
                else:
                    rows_x[sc_i] = jnp.zeros((c, K), F32)
            qd = (qh * jnp.concatenate(rows_x, axis=0)).astype(BF16)
            a_d = lax.dot_general(qd, kt, (((1,), (1,)), ((), ())), preferred_element_type=F32)
            att = att + jnp.where(dist == dd, a_d, 0.0)
        o = o + jnp.dot(att.astype(BF16), vb, preferred_element_type=F32)

        for sc_i in range(nsub):
            r0 = sc_i * c
            b_i = bcs[r0:r0 + c, :]
            q_i = qq[r0:r0 + c, :]
            ws = []
            for s in range(c):
                ok = (t16 <= s) if rev else (t16 >= s)
                ex = jnp.exp(jnp.where(ok, b_i - bcs[r0 + s:r0 + s + 1, :], -jnp.inf))
                ws.append((q_i * ex * kk[r0 + s:r0 + s + 1, :]).astype(BF16))
            w_all = jnp.concatenate(ws, axis=0)
            a_all = jnp.dot(w_all, ones_b, preferred_element_type=F32)
            o_d = jnp.zeros((c, HGRN_V), F32)
            for s in range(c):
                o_d = o_d + a_all[s * c:(s + 1) * c, :] * vv[r0 + s:r0 + s + 1, :]
            o_ref[r0:r0 + c, sl] = o[r0:r0 + c, :] + o_d

    @pl.when(n == pl.num_programs(0) - 1)
    def _():
        sfin_ref[...] = ST_ref[...]


def _hgrn_scan(proj, lb_d, s0, rev):
    L = proj.shape[0]
    d = HGRN_HEADS * HGRN_K
    C = HGRN_CHUNK
    nc = L // C
    fcol = COL_F + (1 if rev else 0)
    blk = lambda n: (nc - 1 - n) if rev else n
    sshape = (HGRN_HEADS, HGRN_V, HGRN_K)
    const3 = lambda n: (0, 0, 0)
    return pl.pallas_call(
        functools.partial(_hgrn_kernel, rev=rev),
        grid=(nc,),
        in_specs=[pl.BlockSpec((C, d), lambda n: (blk(n), COL_Q)),
                  pl.BlockSpec((C, d), lambda n: (blk(n), fcol)),
                  pl.BlockSpec((C, d), lambda n: (blk(n), COL_I)),
                  pl.BlockSpec(lb_d.shape, lambda n: (0, 0)),
                  pl.BlockSpec(sshape, const3)],
        out_specs=[pl.BlockSpec((C, d), lambda n: (blk(n), 0)),
                   pl.BlockSpec(sshape, const3)],
        out_shape=[jax.ShapeDtypeStruct((L, d), F32), jax.ShapeDtypeStruct(sshape, F32)],
        scratch_shapes=[pltpu.VMEM(sshape, F32)],
        compiler_params=_cparams(("arbitrary",)),
        name="hgrn_bwd" if rev else "hgrn_fwd",
    )(proj, proj, proj, lb_d, s0)


def _hgrn_post_kernel(of_ref, ob_ref, g_ref, nw_ref, o_ref):
    nw = nw_ref[...]
    for h in range(HGRN_HEADS):
        sl = slice(h * HGRN_V, (h + 1) * HGRN_V)
        o = of_ref[:, sl] + ob_ref[:, sl]
        ms = jnp.mean(o * o, axis=-1, keepdims=True)
        g = g_ref[:, sl]
        o_ref[:, sl] = (o * lax.rsqrt(ms + EPS) * nw * (g * _sigmoid(g))).astype(o_ref.dtype)


def _hgrn_post(of, ob, proj, nw):
    L, d = of.shape
    tm = min(512, L)
    row = lambda i: (i, 0)
    return pl.pallas_call(
        _hgrn_post_kernel,
        grid=(L // tm,),
        in_specs=[pl.BlockSpec((tm, d), row), pl.BlockSpec((tm, d), row),
                  pl.BlockSpec((tm, d), lambda i: (i, COL_G)),
                  pl.BlockSpec((1, HGRN_V), lambda i: (0, 0))],
        out_specs=pl.BlockSpec((tm, d), row),
        out_shape=jax.ShapeDtypeStruct((L, d), BF16),
        compiler_params=_cparams(("parallel",)),
        name="hgrn_post",
    )(of, ob, proj, nw)


def _merge_kernel(ys_ref, yh_ref, gs_ref, gh_ref, wbs_ref, wbh_ref, o_ref):
    a = jnp.dot(ys_ref[...], wbs_ref[...], preferred_element_type=F32)
    b = jnp.dot(yh_ref[...], wbh_ref[...], preferred_element_type=F32)
    o_ref[...] = (_sigmoid(gs_ref[...]) * a + _sigmoid(gh_ref[...]) * b).astype(o_ref.dtype)


def _merge(ys, yh, proj, wbs, wbh):
    L, d = ys.shape
    tm = min(512, L)
    tn = 512
    nj = d // tn
    return pl.pallas_call(
        _merge_kernel,
        grid=(L // tm, nj),
        in_specs=[pl.BlockSpec((tm, d), lambda i, j: (i, 0)),
                  pl.BlockSpec((tm, d), lambda i, j: (i, 0)),
                  pl.BlockSpec((tm, tn), lambda i, j: (i, COL_GS * nj + j)),
                  pl.BlockSpec((tm, tn), lambda i, j: (i, COL_GH * nj + j)),
                  pl.BlockSpec((d, tn), lambda i, j: (0, j)),
                  pl.BlockSpec((d, tn), lambda i, j: (0, j))],
        out_specs=pl.BlockSpec((tm, tn), lambda i, j: (i, j)),
        out_shape=jax.ShapeDtypeStruct((L, d), BF16),
        compiler_params=_cparams(("parallel", "arbitrary")),
        name="merge",
    )(ys, yh, proj, proj, wbs, wbh)


def _outproj_kernel(y_ref, x_ref, wo_ref, g1_ref, nw_ref, sh_ref, sc_ref, wr_ref, br_ref,
                    h_ref, hn_ref, lg_ref):
    h = x_ref[...] + g1_ref[...] * jnp.dot(y_ref[...], wo_ref[...], preferred_element_type=F32)
    h_ref[...] = h
    ms = jnp.mean(h * h, axis=-1, keepdims=True)
    hn = h * lax.rsqrt(ms + EPS) * nw_ref[...] * (1.0 + sc_ref[...]) + sh_ref[...]
    hn_ref[...] = hn
    lg_ref[...] = jnp.dot(hn, wr_ref[...], precision=HI, preferred_element_type=F32) + br_ref[...]


def _outproj(y, x2, wo, g1, nw, sh, sc, wr, br):
    L, d = x2.shape
    tm = min(256, L)
    row = lambda i: (i, 0)
    const = lambda i: (0, 0)
    vec = pl.BlockSpec((1, d), const)
    return pl.pallas_call(
        _outproj_kernel,
        grid=(L // tm,),
        in_specs=[pl.BlockSpec((tm, d), row), pl.BlockSpec((tm, d), row),
                  pl.BlockSpec((d, d), const), vec, vec, vec, vec,
                  pl.BlockSpec((d, LANES), const), pl.BlockSpec((1, LANES), const)],
        out_specs=[pl.BlockSpec((tm, d), row), pl.BlockSpec((tm, d), row), pl.BlockSpec((tm, LANES), row)],
        out_shape=[jax.ShapeDtypeStruct((L, d), F32), jax.ShapeDtypeStruct((L, d), F32),
                   jax.ShapeDtypeStruct((L, LANES), F32)],
        compiler_params=_cparams(("parallel",)),
        name="outproj",
    )(y, x2, wo, g1, nw, sh, sc, wr, br)


def _route_kernel(lg_ref, idx_ref, pos_ref, p_ref, cnt_ref, carry_ref):
    n = pl.program_id(0)

    @pl.when(n == 0)
    def _():
        carry_ref[...] = jnp.zeros_like(carry_ref)

    tm = lg_ref.shape[0]
    lane_i = lax.broadcasted_iota(I32, (tm, LANES), 1)
    lane = lane_i.astype(F32)
    neg = jnp.float32(-jnp.inf)
    l = jnp.where(lane_i < N_EXPERTS, lg_ref[...], neg)
    vals, sels, ids = [], [], []
    for _ in range(TOP_K):
        m = jnp.max(l, axis=-1, keepdims=True)
        i_k = jnp.min(jnp.where(l == m, lane, float(LANES)), axis=-1, keepdims=True)
        sel = lane == i_k
        l = jnp.where(sel, neg, l)
        vals.append(m)
        sels.append(sel)
        ids.append(i_k)
    es = [jnp.exp(v - vals[0]) for v in vals]
    den = es[0]
    for e in es[1:]:
        den = den + e
    onehot = jnp.zeros((tm, LANES), F32)
    for sel in sels:
        onehot = onehot + jnp.where(sel, 1.0, 0.0)
    ti = lax.broadcasted_iota(I32, (tm, tm), 0)
    si = lax.broadcasted_iota(I32, (tm, tm), 1)
    strict = jnp.where(si < ti, 1.0, 0.0).astype(BF16)
    before = jnp.dot(strict, onehot.astype(BF16), preferred_element_type=F32) + carry_ref[...]
    idx_out = jnp.zeros((tm, LANES), F32)
    pos_out = jnp.zeros((tm, LANES), F32)
    p_out = jnp.zeros((tm, LANES), F32)
    for k in range(TOP_K):
        pos_k = jnp.sum(jnp.where(sels[k], before, 0.0), axis=-1, keepdims=True)
        idx_out = jnp.where(lane_i == k, ids[k], idx_out)
        pos_out = jnp.where(lane_i == k, pos_k, pos_out)
        p_out = jnp.where(lane_i == k, es[k] / den, p_out)
    idx_ref[...] = idx_out.astype(I32)
    pos_ref[...] = pos_out.astype(I32)
    p_ref[...] = p_out
    carry_ref[...] = carry_ref[...] + jnp.sum(onehot, axis=0, keepdims=True)
    cnt_ref[...] = carry_ref[...]


def _route(logits):
    L = logits.shape[0]
    tm = min(256, L)
    row = lambda i: (i, 0)
    return pl.pallas_call(
        _route_kernel,
        grid=(L // tm,),
        in_specs=[pl.BlockSpec((tm, LANES), row)],
        out_specs=[pl.BlockSpec((tm, LANES), row), pl.BlockSpec((tm, LANES), row),
                   pl.BlockSpec((tm, LANES), row), pl.BlockSpec((1, LANES), lambda i: (0, 0))],
        out_shape=[jax.ShapeDtypeStruct((L, LANES), I32), jax.ShapeDtypeStruct((L, LANES), I32),
                   jax.ShapeDtypeStruct((L, LANES), F32), jax.ShapeDtypeStruct((1, LANES), F32)],
        scratch_shapes=[pltpu.VMEM((1, LANES), F32)],
        compiler_params=_cparams(("arbitrary",)),
        name="route",
    )(logits)


def _dispatch_kernel(slot_ref, x_ref, buf_ref, o_ref, sem):
    del buf_ref
    tm = x_ref.shape[0]
    base = pl.program_id(0) * tm

    def row_copy(t, s):
        return pltpu.make_async_copy(x_ref.at[t], o_ref.at[s], sem)

    def issue(t, carry):
        for k in range(TOP_K):
            row_copy(t, slot_ref[(base + t) * TOP_K + k]).start()
        return carry

    lax.fori_loop(0, tm, issue, 0)

    def drain(t, carry):
        for k in range(TOP_K):
            row_copy(t, slot_ref[(base + t) * TOP_K + k]).wait()
        return carry

    lax.fori_loop(0, tm, drain, 0)


def _dispatch(slots, x3, n_slots):
    L, s, l = x3.shape
    tm = min(256, L)
    buf = jnp.zeros((n_slots, s, l), x3.dtype)
    grid_spec = pltpu.PrefetchScalarGridSpec(
        num_scalar_prefetch=1,
        grid=(L // tm,),
        in_specs=[pl.BlockSpec((tm, s, l), lambda i, sl: (i, 0, 0)),
                  pl.BlockSpec(memory_space=pl.ANY)],
        out_specs=pl.BlockSpec(memory_space=pl.ANY),
        scratch_shapes=[pltpu.SemaphoreType.DMA(())],
    )
    return pl.pallas_call(
        _dispatch_kernel,
        grid_spec=grid_spec,
        out_shape=jax.ShapeDtypeStruct((n_slots, s, l), x3.dtype),
        input_output_aliases={2: 0},
        compiler_params=_cparams(("arbitrary",)),
        name="dispatch",
    )(slots, x3, buf)


def _expert_kernel(be_ref, bv_ref, x_ref, wg_ref, wu_ref, bg_ref, bu_ref, wd_ref, bd_ref, o_ref,
                   xb_ref, acc_ref):
    i = pl.program_id(0)
    j = pl.program_id(1)
    nj = pl.num_programs(1)
    nsl = x_ref.shape[1]
    valid = bv_ref[i] > 0

    @pl.when(valid & (j == 0))
    def _():
        for s in range(nsl):
            xb_ref[:, s * LANES:(s + 1) * LANES] = x_ref[:, s, :].astype(BF16)
        acc_ref[...] = jnp.zeros_like(acc_ref)

    @pl.when(valid)
    def _():
        xb = xb_ref[...]
        gate = jnp.dot(xb, wg_ref[0].astype(BF16), preferred_element_type=F32) + bg_ref[0]
        up = jnp.dot(xb, wu_ref[0].astype(BF16), preferred_element_type=F32) + bu_ref[0]
        gate = jnp.minimum(gate, SWIGLU_LIMIT)
        up = jnp.clip(up, -SWIGLU_LIMIT, SWIGLU_LIMIT)
        glu = gate * _sigmoid(SWIGLU_ALPHA * gate)
        act = ((up + 1.0) * glu).astype(BF16)
        acc_ref[...] += jnp.dot(act, wd_ref[0].astype(BF16), preferred_element_type=F32)

    @pl.when(j == nj - 1)
    def _():
        res = jnp.where(valid, acc_ref[...] + bd_ref[0], 0.0)
        for s in range(nsl):
            o_ref[:, s, :] = res[:, s * LANES:(s + 1) * LANES]


def _experts(blk_e, blk_valid, xs3, w_gu, b_gu, w_dn, b_dn):
    n_slots, nsl, l = xs3.shape
    ne, d, two_ff = w_gu.shape
    dff = two_ff // 2
    bm, tf = MOE_BM, MOE_TF
    nj = dff // tf
    nb = n_slots // bm

    def jeff(j, bv, i):
        return jnp.where(bv[i] > 0, j, nj - 1)

    grid_spec = pltpu.PrefetchScalarGridSpec(
        num_scalar_prefetch=2,
        grid=(nb, nj),
        in_specs=[pl.BlockSpec((bm, nsl, l), lambda i, j, be, bv: (i, 0, 0)),
                  pl.BlockSpec((1, d, tf), lambda i, j, be, bv: (be[i], 0, jeff(j, bv, i))),
                  pl.BlockSpec((1, d, tf), lambda i, j, be, bv: (be[i], 0, nj + jeff(j, bv, i))),
                  pl.BlockSpec((1, 1, tf), lambda i, j, be, bv: (be[i], 0, jeff(j, bv, i))),
                  pl.BlockSpec((1, 1, tf), lambda i, j, be, bv: (be[i], 0, nj + jeff(j, bv, i))),
                  pl.BlockSpec((1, tf, d), lambda i, j, be, bv: (be[i], jeff(j, bv, i), 0)),
                  pl.BlockSpec((1, 1, d), lambda i, j, be, bv: (be[i], 0, 0))],
        out_specs=pl.BlockSpec((bm, nsl, l), lambda i, j, be, bv: (i, 0, 0)),
        scratch_shapes=[pltpu.VMEM((bm, d), BF16), pltpu.VMEM((bm, d), F32)],
    )
    return pl.pallas_call(
        _expert_kernel,
        grid_spec=grid_spec,
        out_shape=jax.ShapeDtypeStruct((n_slots, nsl, l), F32),
        compiler_params=_cparams(("arbitrary", "arbitrary")),
        name="experts",
    )(blk_e, blk_valid, xs3, w_gu, w_gu, b_gu.reshape(ne, 1, two_ff), b_gu.reshape(ne, 1, two_ff),
      w_dn, b_dn.reshape(ne, 1, d))


def _combine_kernel(slot_ref, h_ref, p_ref, g2_ref, fw_ref, yb_ref, o_ref, g0_ref, g1_ref, g2b_ref, g3_ref, sem):
    gbufs = (g0_ref, g1_ref, g2b_ref, g3_ref)
    tm = h_ref.shape[0]
    nsl = g0_ref.shape[1]
    base = pl.program_id(0) * tm

    def row_copy(t, k, s):
        return pltpu.make_async_copy(yb_ref.at[s], gbufs[k].at[t], sem)

    def issue(t, carry):
        for k in range(TOP_K):
            row_copy(t, k, slot_ref[(base + t) * TOP_K + k]).start()
        return carry

    lax.fori_loop(0, tm, issue, 0)

    def drain(t, carry):
        for k in range(TOP_K):
            row_copy(t, k, slot_ref[(base + t) * TOP_K + k]).wait()
        return carry

    lax.fori_loop(0, tm, drain, 0)

    p = p_ref[...]
    ssq = jnp.zeros((tm, 1), F32)
    for s in range(nsl):
        cs = slice(s * LANES, (s + 1) * LANES)
        moe = jnp.zeros((tm, LANES), F32)
        for k in range(TOP_K):
            moe = moe + p[:, k:k + 1] * gbufs[k][:, s, :]
        hs = h_ref[:, cs] + g2_ref[:, cs] * moe
        o_ref[:, cs] = hs
        ssq = ssq + jnp.sum(hs * hs, axis=-1, keepdims=True)
    scale = lax.rsqrt(ssq / (nsl * LANES) + EPS)
    o_ref[...] = o_ref[...] * scale * fw_ref[...]


def _combine(slots, h, p, g2, fw, yb3):
    L, d = h.shape
    nsl = yb3.shape[1]
    tm = min(256, L)
    grid_spec = pltpu.PrefetchScalarGridSpec(
        num_scalar_prefetch=1,
        grid=(L // tm,),
        in_specs=[pl.BlockSpec((tm, d), lambda i, sl: (i, 0)),
                  pl.BlockSpec((tm, LANES), lambda i, sl: (i, 0)),
                  pl.BlockSpec((1, d), lambda i, sl: (0, 0)),
                  pl.BlockSpec((1, d), lambda i, sl: (0, 0)),
                  pl.BlockSpec(memory_space=pl.ANY)],
        out_specs=pl.BlockSpec((tm, d), lambda i, sl: (i, 0)),
        scratch_shapes=[pltpu.VMEM((tm, nsl, LANES), F32) for _ in range(TOP_K)] + [pltpu.SemaphoreType.DMA(())],
    )
    return pl.pallas_call(
        _combine_kernel,
        grid_spec=grid_spec,
        out_shape=jax.ShapeDtypeStruct((L, d), F32),
        compiler_params=_cparams(("arbitrary",)),
        name="combine",
    )(slots, h, p, g2, fw, yb3)


def _mixer_scans(x2, nw1, sh1, sc1, w_main, w_dt, conv_w, conv_b, ssm_vecs, rep, lb, row_len, ssd_init, hgrn_init):
    proj, dtr = _inproj(x2, nw1, sh1, sc1, w_main, w_dt)
    xs, bc = _conv_silu(proj, conv_w, conv_b, row_len)
    dtb, alog, amask = ssm_vecs
    ssd, hgrn = [], []
    for d in (0, 1):
        ssd.append(_ssd_scan(xs, bc, dtr, dtb, alog, amask[d], rep[d], ssd_init[d], rev=bool(d)))
        hgrn.append(_hgrn_scan(proj, lb[d], hgrn_init[d], rev=bool(d)))
    return proj, xs, ssd, hgrn


def kernel(x, c, ctx, c_ctx, w_ada, b_ada, norm1_w, norm2_w, w_in, conv_w, conv_b, dt_bias, a_log, d_skip,
           ssm_norm_w, hgrn_lb, hgrn_norm_w, w_branch_ssm, w_branch_hgrn, w_out, w_router, b_router,
           w_gate_up, b_gate_up, w_down, b_down, final_norm_w):
    bsz, seq, d = x.shape
    assert bsz == 1 and w_ada.shape[0] == 1
    ctx_len = ctx.shape[1]
    H = SSM_HEADS

    w = w_in[0]
    o_z = SSM_INNER
    o_xbc = o_z + SSM_INNER + 2 * SSM_GROUPS * SSM_STATE
    o_dt = o_xbc + 2 * H
    w_main = jnp.concatenate([w[:, :o_z], w[:, o_dt:], w[:, o_z:o_xbc]], axis=1).astype(BF16)
    w_dt = jnp.pad(w[:, o_xbc:o_dt], ((0, 0), (0, LANES - 2 * H))).astype(BF16)
    dtb = jnp.pad(dt_bias[0].reshape(1, 2 * H), ((0, 0), (0, LANES - 2 * H)))
    alog = jnp.pad(a_log[0].reshape(1, 2 * H), ((0, 0), (0, LANES - 2 * H)))
    lane = jnp.arange(LANES)
    amask = [((lane >= dd * H) & (lane < (dd + 1) * H)).astype(F32).reshape(1, LANES) for dd in (0, 1)]
    chan_head = jnp.arange(SSM_INNER) // SSM_HEAD_DIM
    rep = [(lane[:, None] == (chan_head[None, :] + dd * H)).astype(F32) for dd in (0, 1)]
    dsk = jnp.repeat(d_skip[0], SSM_HEAD_DIM).reshape(1, SSM_INNER)
    lb = [hgrn_lb[dd] for dd in (0, 1)]
    wbs = w_branch_ssm[0].astype(BF16)
    wbh = w_branch_hgrn[0].astype(BF16)
    wo = w_out[0].astype(BF16)
    wr = jnp.pad(w_router[0], ((0, 0), (0, LANES - N_EXPERTS)))
    br = jnp.pad(b_router[0].reshape(1, N_EXPERTS), ((0, 0), (0, LANES - N_EXPERTS)))
    row = lambda v: v.reshape(1, -1)

    cc = jnp.zeros((8, d), F32).at[0].set(c[0]).at[1].set(c_ctx)
    mod = _modulation(cc, w_ada[0], b_ada[0])
    sh1, sc1, g1, sh2, sc2, g2 = [mod[0:1, k * d:(k + 1) * d] for k in range(6)]
    csh1, csc1 = mod[1:2, 0:d], mod[1:2, d:2 * d]

    ssm_vecs = (dtb, alog, amask)
    zs = jnp.zeros((SSM_GROUPS, SSM_STATE, SSM_GROUP_W), F32)
    zh = jnp.zeros((HGRN_HEADS, HGRN_V, HGRN_K), F32)
    nw1 = row(norm1_w[0])

    _, _, ssd_c, hgrn_c = _mixer_scans(ctx[0], nw1, csh1, csc1, w_main, w_dt, conv_w[0], conv_b[0], ssm_vecs,
                                       rep, lb, ctx_len, (zs, zs), (zh, zh))
    ssd_init = (ssd_c[0][1], ssd_c[1][1])
    hgrn_init = (hgrn_c[0][1], hgrn_c[1][1])

    x2 = x[0]
    proj, xs, ssd_l, hgrn_l = _mixer_scans(x2, nw1, sh1, sc1, w_main, w_dt, conv_w[0], conv_b[0], ssm_vecs,
                                           rep, lb, GRID_W, ssd_init, hgrn_init)
    y_ssm = _ssd_post(ssd_l[0][0], ssd_l[1][0], xs, proj, dsk, row(ssm_norm_w[0]))
    y_hgrn = _hgrn_post(hgrn_l[0][0], hgrn_l[1][0], proj, row(hgrn_norm_w[0]))
    y = _merge(y_ssm, y_hgrn, proj, wbs, wbh)
    h, hn2, logits = _outproj(y, x2, wo, g1, row(norm2_w[0]), sh2, sc2, wr, br)

    idx, pos, p, counts = _route(logits)
    counts = counts[0, :N_EXPERTS].astype(I32)
    bm = MOE_BM
    n_assign = seq * TOP_K
    n_blocks = -(-(n_assign + N_EXPERTS * (bm - 1)) // bm)
    n_slots = n_blocks * bm
    padded = (counts + bm - 1) // bm * bm
    pad_end = jnp.cumsum(padded)
    pad_start = pad_end - padded
    slots = (pad_start[idx[:, :TOP_K]] + pos[:, :TOP_K]).reshape(-1).astype(I32)
    blk_start = jnp.arange(n_blocks, dtype=I32) * bm
    blk_e = jnp.minimum(jnp.searchsorted(pad_end, blk_start, side="right"), N_EXPERTS - 1).astype(I32)
    blk_valid = (blk_start < pad_end[-1]).astype(I32)

    nsl = d // LANES
    xs3 = _dispatch(slots, hn2.reshape(seq, nsl, LANES), n_slots)
    yb3 = _experts(blk_e, blk_valid, xs3, w_gate_up[0], b_gate_up[0], w_down[0], b_down[0])
    out = _combine(slots, h, p, g2, row(final_norm_w), yb3)
    return out.reshape(bsz, seq, d)
```

```python
import functools

import jax
import jax.numpy as jnp
from jax import lax
from jax.experimental import pallas as pl
from jax.experimental.pallas import tpu as pltpu

F32 = jnp.float32
BF16 = jnp.bfloat16
I32 = jnp.int32
HI = lax.Precision.HIGHEST

EPS = 1e-6
GRID_W = 64

LANES = 128
NSL = 16
SSM_HEADS = 32
SSM_HEAD_DIM = 64
SSM_GROUPS = 4
SSM_STATE = 128
SSM_INNER = SSM_HEADS * SSM_HEAD_DIM
SSM_GROUP_W = SSM_INNER // SSM_GROUPS
SSM_CHUNK = 128
HGRN_HEADS = 16
HGRN_K = 128
HGRN_V = 128
HGRN_CHUNK = 64
HGRN_SUB = 16
HGRN_HEAD_GROUP = 8
HGRN_SAFE_DECAY = 60.0
N_EXPERTS = 32
TOP_K = 4
SWIGLU_LIMIT = 7.0
SWIGLU_ALPHA = 1.702
MOE_BM = 512
MOE_TF = 512
VMEM_LIMIT = 56 * 1024 * 1024

COL_Z, COL_Q, COL_F, COL_I, COL_G, COL_GS, COL_GH, COL_XS = 0, 1, 2, 4, 5, 6, 7, 8


def _sigmoid(x):
    return 1.0 / (1.0 + jnp.exp(-x))


def _cparams(sem):
    return pltpu.CompilerParams(dimension_semantics=sem, vmem_limit_bytes=VMEM_LIMIT)


def _mod_kernel(c_ref, w_ref, b_ref, o_ref):
    c = c_ref[...]
    s = c * _sigmoid(c)
    o_ref[...] = jnp.dot(s, w_ref[...], precision=HI, preferred_element_type=F32) + b_ref[...]


def _modulation(cc, w_ada, b_ada):
    d, n = w_ada.shape
    tn = 1024
    return pl.pallas_call(
        _mod_kernel,
        grid=(n // tn,),
        in_specs=[pl.BlockSpec((8, d), lambda j: (0, 0)),
                  pl.BlockSpec((d, tn), lambda j: (0, j)),
                  pl.BlockSpec((1, tn), lambda j: (0, j))],
        out_specs=pl.BlockSpec((8, tn), lambda j: (0, j)),
        out_shape=jax.ShapeDtypeStruct((8, n), F32),
        compiler_params=_cparams(("arbitrary",)),
        name="adaln_mod",
    )(cc, w_ada, b_ada.reshape(1, n))


def _inproj_kernel(x_ref, nw_ref, sh_ref, sc_ref, w_ref, wdt_ref, o_ref, odt_ref, hn_ref):
    @pl.when(pl.program_id(1) == 0)
    def _():
        x = x_ref[...]
        ms = jnp.mean(x * x, axis=-1, keepdims=True)
        y = x * lax.rsqrt(ms + EPS) * nw_ref[...]
        hn = (y * (1.0 + sc_ref[...]) + sh_ref[...]).astype(BF16)
        hn_ref[...] = hn
        odt_ref[...] = jnp.dot(hn, wdt_ref[...], preferred_element_type=F32)

    o_ref[...] = jnp.dot(hn_ref[...], w_ref[...], preferred_element_type=F32)


def _inproj(x2, nw, sh, sc, w_main, w_dt):
    L, d = x2.shape
    n = w_main.shape[1]
    tm = min(512, L)
    tn = 1024
    return pl.pallas_call(
        _inproj_kernel,
        grid=(L // tm, n // tn),
        in_specs=[pl.BlockSpec((tm, d), lambda i, j: (i, 0)),
                  pl.BlockSpec((1, d), lambda i, j: (0, 0)),
                  pl.BlockSpec((1, d), lambda i, j: (0, 0)),
                  pl.BlockSpec((1, d), lambda i, j: (0, 0)),
                  pl.BlockSpec((d, tn), lambda i, j: (0, j)),
                  pl.BlockSpec((d, LANES), lambda i, j: (0, 0))],
        out_specs=[pl.BlockSpec((tm, tn), lambda i, j: (i, j)),
                   pl.BlockSpec((tm, LANES), lambda i, j: (i, 0))],
        out_shape=[jax.ShapeDtypeStruct((L, n), F32), jax.ShapeDtypeStruct((L, LANES), F32)],
        scratch_shapes=[pltpu.VMEM((tm, d), BF16)],
        compiler_params=_cparams(("parallel", "arbitrary")),
        name="inproj",
    )(x2, nw, sh, sc, w_main, w_dt)


def _conv_body(u, w_ref, b_ref, row_len):
    tb = u.shape[0]
    pos = lax.broadcasted_iota(I32, u.shape, 0) & (row_len - 1)
    acc = u * w_ref[2:3, :] + b_ref[...]
    for k, off in ((0, -2), (1, -1), (3, 1)):
        shifted = pltpu.roll(u, shift=(-off) % tb, axis=0)
        valid = (pos + off >= 0) & (pos + off < row_len)
        acc = acc + jnp.where(valid, shifted, 0.0) * w_ref[k:k + 1, :]
    return acc * _sigmoid(acc)


def _conv_kernel(ux_ref, ub_ref, wx_ref, wb_ref, bx_ref, bb_ref, ox_ref, ob_ref, *, row_len):
    ox_ref[...] = _conv_body(ux_ref[...], wx_ref, bx_ref, row_len)
    ob_ref[...] = _conv_body(ub_ref[...], wb_ref, bb_ref, row_len)


def _conv_silu(proj, conv_w, conv_b, row_len):
    L = proj.shape[0]
    d = SSM_INNER
    nbc = 2 * SSM_GROUPS * SSM_STATE
    tb = max(row_len, min(512, L))
    wx, wb = conv_w[:, :d], conv_w[:, d:]
    bx, bb = conv_b[:d].reshape(1, d), conv_b[d:].reshape(1, nbc)
    return pl.pallas_call(
        functools.partial(_conv_kernel, row_len=row_len),
        grid=(L // tb,),
        in_specs=[pl.BlockSpec((tb, d), lambda i: (i, COL_XS)),
                  pl.BlockSpec((tb, nbc), lambda i: (i, COL_XS * 2 + 2)),
                  pl.BlockSpec((4, d), lambda i: (0, 0)),
                  pl.BlockSpec((4, nbc), lambda i: (0, 0)),
                  pl.BlockSpec((1, d), lambda i: (0, 0)),
                  pl.BlockSpec((1, nbc), lambda i: (0, 0))],
        out_specs=[pl.BlockSpec((tb, d), lambda i: (i, 0)),
                   pl.BlockSpec((tb, nbc), lambda i: (i, 0))],
        out_shape=[jax.ShapeDtypeStruct((L, d), F32), jax.ShapeDtypeStruct((L, nbc), F32)],
        compiler_params=_cparams(("parallel",)),
        name="conv_silu",
    )(proj, proj, wx, wb, bx, bb)


def _ssd_kernel(xs_ref, bc_ref, dt_ref, dtb_ref, alog_ref, amask_ref, rep_ref, s0_ref,
                y_ref, sfin_ref, S_ref, *, rev, lane0):
    n = pl.program_id(0)

    @pl.when(n == 0)
    def _():
        S_ref[...] = s0_ref[...]

    Q = xs_ref.shape[0]
    N = SSM_STATE
    GW = SSM_GROUP_W
    ti = lax.broadcasted_iota(I32, (Q, Q), 0)
    si = lax.broadcasted_iota(I32, (Q, Q), 1)
    allowed = (si >= ti) if rev else (si <= ti)
    tri = allowed.astype(F32)

    raw = dt_ref[...] + dtb_ref[...]
    dt = jnp.maximum(raw, 0.0) + jnp.log1p(jnp.exp(-jnp.abs(raw)))
    loga = dt * (-jnp.exp(alog_ref[...]) * amask_ref[...])
    cs = jnp.dot(tri, loga, precision=HI, preferred_element_type=F32)
    csT = cs.T
    rep = rep_ref[...]
    cs_x = jnp.dot(cs, rep, precision=HI, preferred_element_type=F32)
    dt_x = jnp.dot(dt, rep, precision=HI, preferred_element_type=F32)
    tot_x = cs_x[0:1, :] if rev else cs_x[Q - 1:Q, :]
    xdt = xs_ref[...] * dt_x
    e_in = jnp.exp(cs_x)
    xw = (xdt * jnp.exp(tot_x - cs_x)).astype(BF16)
    xdt_b = xdt.astype(BF16)
    dec = jnp.exp(tot_x)
    lane = lax.broadcasted_iota(I32, (Q, LANES), 1)
    lo_mask = lane < SSM_HEAD_DIM

    for g in range(SSM_GROUPS):
        Bg = bc_ref[:, g * N:(g + 1) * N].astype(BF16)
        Cg = bc_ref[:, (SSM_GROUPS + g) * N:(SSM_GROUPS + g + 1) * N].astype(BF16)
        cb = lax.dot_general(Cg, Bg, (((1,), (1,)), ((), ())), preferred_element_type=F32)
        Sg = S_ref[g]
        y_off = jnp.dot(Cg, Sg.astype(BF16), preferred_element_type=F32) * e_in[:, g * GW:(g + 1) * GW]
        upd = lax.dot_general(Bg, xw[:, g * GW:(g + 1) * GW], (((0,), (0,)), ((), ())),
                              preferred_element_type=F32)
        S_ref[g] = Sg * dec[:, g * GW:(g + 1) * GW] + upd
        for jp in range(GW // LANES):
            h0 = g * (GW // SSM_HEAD_DIM) + 2 * jp
            ms = []
            for hh in (h0, h0 + 1):
                col = cs[:, lane0 + hh:lane0 + hh + 1]
                row = csT[lane0 + hh:lane0 + hh + 1, :]
                dec_ts = jnp.exp(jnp.where(allowed, col - row, -jnp.inf))
                ms.append((cb * dec_ts).astype(BF16))
            m2 = jnp.concatenate(ms, axis=1)
            x2 = xdt_b[:, h0 * SSM_HEAD_DIM:(h0 + 2) * SSM_HEAD_DIM]
            r2 = jnp.concatenate([jnp.where(lo_mask, x2, jnp.zeros_like(x2)),
                                  jnp.where(lo_mask, jnp.zeros_like(x2), x2)], axis=0)
            y_d = jnp.dot(m2, r2, preferred_element_type=F32)
            c0 = h0 * SSM_HEAD_DIM
            y_ref[:, c0:c0 + LANES] = y_d + y_off[:, jp * LANES:(jp + 1) * LANES]

    @pl.when(n == pl.num_programs(0) - 1)
    def _():
        sfin_ref[...] = S_ref[...]


def _ssd_scan(xs, bc, dt, dtb, alog, amask, rep, s0, rev):
    L, d = xs.shape
    Q = SSM_CHUNK
    nc = L // Q
    idx = (lambda n: (nc - 1 - n, 0)) if rev else (lambda n: (n, 0))
    const2 = lambda n: (0, 0)
    const3 = lambda n: (0, 0, 0)
    sshape = (SSM_GROUPS, SSM_STATE, SSM_GROUP_W)
    return pl.pallas_call(
        functools.partial(_ssd_kernel, rev=rev, lane0=SSM_HEADS if rev else 0),
        grid=(nc,),
        in_specs=[pl.BlockSpec((Q, d), idx),
                  pl.BlockSpec((Q, bc.shape[1]), idx),
                  pl.BlockSpec((Q, LANES), idx),
                  pl.BlockSpec((1, LANES), const2),
                  pl.BlockSpec((1, LANES), const2),
                  pl.BlockSpec((1, LANES), const2),
                  pl.BlockSpec((LANES, d), const2),
                  pl.BlockSpec(sshape, const3)],
        out_specs=[pl.BlockSpec((Q, d), idx),
                   pl.BlockSpec(sshape, const3)],
        out_shape=[jax.ShapeDtypeStruct((L, d), F32), jax.ShapeDtypeStruct(sshape, F32)],
        scratch_shapes=[pltpu.VMEM(sshape, F32)],
        compiler_params=_cparams(("arbitrary",)),
        name="ssd_bwd" if rev else "ssd_fwd",
    )(xs, bc, dt, dtb, alog, amask, rep, s0)


def _ssd_post_kernel(yf_ref, yb_ref, xs_ref, z_ref, dsk_ref, nw_ref, o_ref):
    z = z_ref[...]
    y = (yf_ref[...] + yb_ref[...] + dsk_ref[...] * xs_ref[...]) * (z * _sigmoid(z))
    for g in range(SSM_GROUPS):
        sl = slice(g * SSM_GROUP_W, (g + 1) * SSM_GROUP_W)
        yg = y[:, sl]
        ms = jnp.mean(yg * yg, axis=-1, keepdims=True)
        o_ref[:, sl] = (yg * lax.rsqrt(ms + EPS) * nw_ref[:, sl]).astype(o_ref.dtype)


def _ssd_post(yf, yb, xs, proj, dsk, nw):
    L, d = yf.shape
    tm = min(512, L)
    row = lambda i: (i, 0)
    const = lambda i: (0, 0)
    return pl.pallas_call(
        _ssd_post_kernel,
        grid=(L // tm,),
        in_specs=[pl.BlockSpec((tm, d), row), pl.BlockSpec((tm, d), row), pl.BlockSpec((tm, d), row),
                  pl.BlockSpec((tm, d), lambda i: (i, COL_Z)),
                  pl.BlockSpec((1, d), const), pl.BlockSpec((1, d), const)],
        out_specs=pl.BlockSpec((tm, d), row),
        out_shape=jax.ShapeDtypeStruct((L, d), BF16),
        compiler_params=_cparams(("parallel",)),
        name="ssd_post",
    )(yf, yb, xs, proj, dsk, nw)


def _hgrn_kernel(q_ref, f_ref, i_ref, lb_ref, s0_ref, o_ref, sfin_ref, ST_ref, kk_ref, bcs_ref, lf_ref, v_ref,
                 vT_ref, *, rev):
    n = pl.program_id(0)

    @pl.when(n == 0)
    def _():
        ST_ref[...] = s0_ref[...]

    C = q_ref.shape[0]
    c = HGRN_SUB
    nsub = C // c
    K = HGRN_K
    d = HGRN_HEADS * K
    ti = lax.broadcasted_iota(I32, (C, C), 0)
    si = lax.broadcasted_iota(I32, (C, C), 1)
    allowed = (si >= ti) if rev else (si <= ti)
    bt, bs = ti // c, si // c
    dist = (bs - bt) if rev else (bt - bs)
    order = list(range(nsub - 1, -1, -1)) if rev else list(range(nsub))

    for h in range(HGRN_HEADS):
        sl = slice(h * K, (h + 1) * K)
        lbr = lb_ref[:, sl]
        lmax = jnp.max(lbr, axis=0, keepdims=True)
        le = jnp.exp(lbr - lmax)
        lb = le[0:1, :] / jnp.sum(le, axis=0, keepdims=True)
        fr = f_ref[:, sl]
        e = jnp.exp(-jnp.abs(fr))
        rinv = 1.0 / (1.0 + e)
        pos = fr >= 0.0
        sg = jnp.where(pos, rinv, e * rinv)
        sgn = jnp.where(pos, e * rinv, rinv)
        logf = jnp.log(lb + (1.0 - lb) * sg)
        kk_ref[:, sl] = (1.0 - lb) * sgn
        iv = i_ref[:, sl]
        vv = iv * _sigmoid(iv)
        v_ref[:, sl] = vv
        vT_ref[sl, :] = vv.T.astype(BF16)
        hi = logf.astype(BF16)
        r1 = logf - hi.astype(F32)
        mid = r1.astype(BF16)
        lo = (r1 - mid.astype(F32)).astype(BF16)
        lf_ref[:, h * K:(h + 1) * K] = hi
        lf_ref[:, d + h * K:d + (h + 1) * K] = mid
        lf_ref[:, 2 * d + h * K:2 * d + (h + 1) * K] = lo
    cs3 = jnp.dot(allowed.astype(BF16), lf_ref[...], preferred_element_type=F32)
    bcs_ref[...] = (cs3[:, :d] + cs3[:, d:2 * d]) + cs3[:, 2 * d:]

    def levels(col):
        r_in, r_out = {}, {}
        for p, sc_i in enumerate(order):
            last = sc_i * c + (0 if rev else c - 1)
            r_out[p] = bcs_ref[last:last + 1, col]
            r_in[p] = r_out[p - 1] if p > 0 else jnp.zeros_like(r_out[p])
        return r_in, r_out

    r_in_all, r_out_all = levels(slice(0, d))
    worst = r_in_all[0] - r_out_all[0]
    for p in range(1, nsub):
        worst = jnp.maximum(worst, r_in_all[p] - r_out_all[p])
    safe = jnp.max(worst) < HGRN_SAFE_DECAY

    def heads(exact_pairwise):
        t16 = lax.broadcasted_iota(I32, (c, K), 0)
        ones_b = jnp.ones((K, K), BF16)
        t2 = lax.broadcasted_iota(I32, (C, 2 * C), 0)
        col2 = lax.broadcasted_iota(I32, (C, 2 * C), 1)
        s2 = col2 & (C - 1)
        dist2 = (s2 // c - t2 // c) if rev else (t2 // c - s2 // c)
        dist2 = jnp.where(col2 < C, dist2, -1)
        same_blk = (col2 >= C) & (t2 // c == s2 // c) & ((s2 >= t2) if rev else (s2 <= t2))
        band = {dd: dist2 == dd for dd in range(1, nsub)}
        def matmuls(h):
            sl = slice(h * K, (h + 1) * K)
            bcs = bcs_ref[:, sl]
            kk = kk_ref[:, sl]
            qq = q_ref[:, sl] * (K ** -0.5)
            tot = bcs[0:1, :] if rev else bcs[C - 1:C, :]

            ST = ST_ref[h]
            q_in = (qq * jnp.exp(bcs)).astype(BF16)
            o = lax.dot_general(q_in, ST.astype(BF16), (((1,), (1,)), ((), ())), preferred_element_type=F32)
            k_end = (kk * jnp.exp(tot - bcs)).astype(BF16)
            upd = jnp.dot(vT_ref[sl, :], k_end, preferred_element_type=F32)
            new_state = ST * jnp.exp(tot) + upd

            r_in, r_out = levels(sl)
            rows_in = [None] * nsub
            rows_out = [None] * nsub
            for p, sc_i in enumerate(order):
                rows_in[sc_i] = jnp.broadcast_to(r_in[p], (c, K))
                rows_out[sc_i] = jnp.broadcast_to(r_out[p], (c, K))
            R_in = jnp.concatenate(rows_in, axis=0)
            R_out = jnp.concatenate(rows_out, axis=0)
            qh = qq * jnp.exp(bcs - R_in)
            kt = kk * jnp.exp(R_out - bcs)
            lhs = [qh]
            for dd in range(2, nsub):
                rows_x = [None] * nsub
                for p, sc_i in enumerate(order):
                    if p >= dd:
                        rows_x[sc_i] = jnp.broadcast_to(jnp.exp(r_in[p] - r_out[p - dd]), (c, K))
                    else:
                        rows_x[sc_i] = jnp.zeros((c, K), F32)
                lhs.append(qh * jnp.concatenate(rows_x, axis=0))
            lhs = jnp.concatenate(lhs, axis=0).astype(BF16)
            if exact_pairwise:
                rhs = kt.astype(BF16)
            else:
                k_diag = kk * jnp.exp(R_in - bcs)
                rhs = jnp.concatenate([kt, k_diag], axis=0).astype(BF16)
            a = lax.dot_general(lhs, rhs, (((1,), (1,)), ((), ())), preferred_element_type=F32)
            return o, a, new_state

        def finish(h, o, a, new_state):
            sl = slice(h * K, (h + 1) * K)
            ST_ref[h] = new_state
            vv = v_ref[:, sl]
            vb = vv.astype(BF16)
            if exact_pairwise:
                bcs = bcs_ref[:, sl]
                kk = kk_ref[:, sl]
                qq = q_ref[:, sl] * (K ** -0.5)
                att = jnp.where(dist == 1, a[0:C], 0.0)
                for dd in range(2, nsub):
                    att = att + jnp.where(dist == dd, a[(dd - 1) * C:dd * C], 0.0)
                o = o + jnp.dot(att.astype(BF16), vb, preferred_element_type=F32)
                for sc_i in range(nsub):
                    r0 = sc_i * c
                    b_i = bcs[r0:r0 + c, :]
                    q_i = qq[r0:r0 + c, :]
                    ws = []
                    for s in range(c):
                        ok = (t16 <= s) if rev else (t16 >= s)
                        ex = jnp.exp(jnp.where(ok, b_i - bcs[r0 + s:r0 + s + 1, :], -jnp.inf))
                        ws.append((q_i * ex * kk[r0 + s:r0 + s + 1, :]).astype(BF16))
                    w_all = jnp.concatenate(ws, axis=0)
                    a_all = jnp.dot(w_all, ones_b, preferred_element_type=F32)
                    o_d = jnp.zeros((c, HGRN_V), F32)
                    for s in range(c):
                        o_d = o_d + a_all[s * c:(s + 1) * c, :] * vv[r0 + s:r0 + s + 1, :]
                    o_ref[r0:r0 + c, sl] = o[r0:r0 + c, :] + o_d
            else:
                att = jnp.where(band[1] | same_blk, a[0:C], 0.0)
                for dd in range(2, nsub):
                    att = att + jnp.where(band[dd], a[(dd - 1) * C:dd * C], 0.0)
                v2 = jnp.concatenate([vb, vb], axis=0)
                o_ref[:, sl] = o + jnp.dot(att.astype(BF16), v2, preferred_element_type=F32)

        group = 1 if exact_pairwise else HGRN_HEAD_GROUP
        for h0 in range(0, HGRN_HEADS, group):
            staged = [matmuls(h) for h in range(h0, h0 + group)]
            for h, vals in zip(range(h0, h0 + group), staged):
                finish(h, *vals)

    @pl.when(safe)
    def _():
        heads(False)

    @pl.when(jnp.logical_not(safe))
    def _():
        heads(True)

    @pl.when(n == pl.num_programs(0) - 1)
    def _():
        sfin_ref[...] = ST_ref[...]


def _hgrn_scan(proj, lb_d, s0, rev):
    L = proj.shape[0]
    d = HGRN_HEADS * HGRN_K
    C = HGRN_CHUNK
    nc = L // C
    fcol = COL_F + (1 if rev else 0)
    blk = lambda n: (nc - 1 - n) if rev else n
    sshape = (HGRN_HEADS, HGRN_V, HGRN_K)
    const3 = lambda n: (0, 0, 0)
    return pl.pallas_call(
        functools.partial(_hgrn_kernel, rev=rev),
        grid=(nc,),
        in_specs=[pl.BlockSpec((C, d), lambda n: (blk(n), COL_Q)),
                  pl.BlockSpec((C, d), lambda n: (blk(n), fcol)),
                  pl.BlockSpec((C, d), lambda n: (blk(n), COL_I)),
                  pl.BlockSpec(lb_d.shape, lambda n: (0, 0)),
                  pl.BlockSpec(sshape, const3)],
        out_specs=[pl.BlockSpec((C, d), lambda n: (blk(n), 0)),
                   pl.BlockSpec(sshape, const3)],
        out_shape=[jax.ShapeDtypeStruct((L, d), F32), jax.ShapeDtypeStruct(sshape, F32)],
        scratch_shapes=[pltpu.VMEM(sshape, F32), pltpu.VMEM((C, d), F32), pltpu.VMEM((C, d), F32),
                        pltpu.VMEM((C, 3 * d), BF16), pltpu.VMEM((C, d), F32), pltpu.VMEM((d, C), BF16)],
        compiler_params=_cparams(("arbitrary",)),
        name="hgrn_bwd" if rev else "hgrn_fwd",
    )(proj, proj, proj, lb_d, s0)


def _hgrn_post_kernel(of_ref, ob_ref, g_ref, nw_ref, o_ref):
    nw = nw_ref[...]
    for h in range(HGRN_HEADS):
        sl = slice(h * HGRN_V, (h + 1) * HGRN_V)
        o = of_ref[:, sl] + ob_ref[:, sl]
        ms = jnp.mean(o * o, axis=-1, keepdims=True)
        g = g_ref[:, sl]
        o_ref[:, sl] = (o * lax.rsqrt(ms + EPS) * nw * (g * _sigmoid(g))).astype(o_ref.dtype)


def _hgrn_post(of, ob, proj, nw):
    L, d = of.shape
    tm = min(512, L)
    row = lambda i: (i, 0)
    return pl.pallas_call(
        _hgrn_post_kernel,
        grid=(L // tm,),
        in_specs=[pl.BlockSpec((tm, d), row), pl.BlockSpec((tm, d), row),
                  pl.BlockSpec((tm, d), lambda i: (i, COL_G)),
                  pl.BlockSpec((1, HGRN_V), lambda i: (0, 0))],
        out_specs=pl.BlockSpec((tm, d), row),
        out_shape=jax.ShapeDtypeStruct((L, d), BF16),
        compiler_params=_cparams(("parallel",)),
        name="hgrn_post",
    )(of, ob, proj, nw)


def _merge_kernel(ys_ref, yh_ref, gs_ref, gh_ref, wbs_ref, wbh_ref, o_ref):
    a = jnp.dot(ys_ref[...], wbs_ref[...], preferred_element_type=F32)
    b = jnp.dot(yh_ref[...], wbh_ref[...], preferred_element_type=F32)
    o_ref[...] = (_sigmoid(gs_ref[...]) * a + _sigmoid(gh_ref[...]) * b).astype(o_ref.dtype)


def _merge(ys, yh, proj, wbs, wbh):
    L, d = ys.shape
    tm = min(512, L)
    tn = 512
    nj = d // tn
    return pl.pallas_call(
        _merge_kernel,
        grid=(L // tm, nj),
        in_specs=[pl.BlockSpec((tm, d), lambda i, j: (i, 0)),
                  pl.BlockSpec((tm, d), lambda i, j: (i, 0)),
                  pl.BlockSpec((tm, tn), lambda i, j: (i, COL_GS * nj + j)),
                  pl.BlockSpec((tm, tn), lambda i, j: (i, COL_GH * nj + j)),
                  pl.BlockSpec((d, tn), lambda i, j: (0, j)),
                  pl.BlockSpec((d, tn), lambda i, j: (0, j))],
        out_specs=pl.BlockSpec((tm, tn), lambda i, j: (i, j)),
        out_shape=jax.ShapeDtypeStruct((L, d), BF16),
        compiler_params=_cparams(("parallel", "arbitrary")),
        name="merge",
    )(ys, yh, proj, proj, wbs, wbh)


def _outproj_kernel(y_ref, x_ref, wo_ref, g1_ref, nw_ref, sh_ref, sc_ref, wr_ref, br_ref,
                    h_ref, hn_ref, lg_ref):
    h = x_ref[...] + g1_ref[...] * jnp.dot(y_ref[...], wo_ref[...], preferred_element_type=F32)
    h_ref[...] = h
    ms = jnp.mean(h * h, axis=-1, keepdims=True)
    hn = h * lax.rsqrt(ms + EPS) * nw_ref[...] * (1.0 + sc_ref[...]) + sh_ref[...]
    hn_ref[...] = hn
    lg_ref[...] = jnp.dot(hn, wr_ref[...], precision=HI, preferred_element_type=F32) + br_ref[...]


def _outproj(y, x2, wo, g1, nw, sh, sc, wr, br):
    L, d = x2.shape
    tm = min(256, L)
    row = lambda i: (i, 0)
    const = lambda i: (0, 0)
    vec = pl.BlockSpec((1, d), const)
    return pl.pallas_call(
        _outproj_kernel,
        grid=(L // tm,),
        in_specs=[pl.BlockSpec((tm, d), row), pl.BlockSpec((tm, d), row),
                  pl.BlockSpec((d, d), const), vec, vec, vec, vec,
                  pl.BlockSpec((d, LANES), const), pl.BlockSpec((1, LANES), const)],
        out_specs=[pl.BlockSpec((tm, d), row), pl.BlockSpec((tm, d), row), pl.BlockSpec((tm, LANES), row)],
        out_shape=[jax.ShapeDtypeStruct((L, d), F32), jax.ShapeDtypeStruct((L, d), F32),
                   jax.ShapeDtypeStruct((L, LANES), F32)],
        compiler_params=_cparams(("parallel",)),
        name="outproj",
    )(y, x2, wo, g1, nw, sh, sc, wr, br)


def _route_kernel(lg_ref, idx_ref, pos_ref, p_ref, cnt_ref, carry_ref):
    n = pl.program_id(0)

    @pl.when(n == 0)
    def _():
        carry_ref[...] = jnp.zeros_like(carry_ref)

    tm = lg_ref.shape[0]
    lane_i = lax.broadcasted_iota(I32, (tm, LANES), 1)
    lane = lane_i.astype(F32)
    neg = jnp.float32(-jnp.inf)
    l = jnp.where(lane_i < N_EXPERTS, lg_ref[...], neg)
    vals, sels, ids = [], [], []
    for _ in range(TOP_K):
        m = jnp.max(l, axis=-1, keepdims=True)
        i_k = jnp.min(jnp.where(l == m, lane, float(LANES)), axis=-1, keepdims=True)
        sel = lane == i_k
        l = jnp.where(sel, neg, l)
        vals.append(m)
        sels.append(sel)
        ids.append(i_k)
    es = [jnp.exp(v - vals[0]) for v in vals]
    den = es[0]
    for e in es[1:]:
        den = den + e
    onehot = jnp.zeros((tm, LANES), F32)
    for sel in sels:
        onehot = onehot + jnp.where(sel, 1.0, 0.0)
    ti = lax.broadcasted_iota(I32, (tm, tm), 0)
    si = lax.broadcasted_iota(I32, (tm, tm), 1)
    strict = jnp.where(si < ti, 1.0, 0.0).astype(BF16)
    before = jnp.dot(strict, onehot.astype(BF16), preferred_element_type=F32) + carry_ref[...]
    idx_out = jnp.zeros((tm, LANES), F32)
    pos_out = jnp.zeros((tm, LANES), F32)
    p_out = jnp.zeros((tm, LANES), F32)
    for k in range(TOP_K):
        pos_k = jnp.sum(jnp.where(sels[k], before, 0.0), axis=-1, keepdims=True)
        idx_out = jnp.where(lane_i == k, ids[k], idx_out)
        pos_out = jnp.where(lane_i == k, pos_k, pos_out)
        p_out = jnp.where(lane_i == k, es[k] / den, p_out)
    idx_ref[...] = idx_out.astype(I32)
    pos_ref[...] = pos_out.astype(I32)
    p_ref[...] = p_out
    carry_ref[...] = carry_ref[...] + jnp.sum(onehot, axis=0, keepdims=True)
    cnt_ref[...] = carry_ref[...]


def _route(logits):
    L = logits.shape[0]
    tm = min(256, L)
    row = lambda i: (i, 0)
    return pl.pallas_call(
        _route_kernel,
        grid=(L // tm,),
        in_specs=[pl.BlockSpec((tm, LANES), row)],
        out_specs=[pl.BlockSpec((tm, LANES), row), pl.BlockSpec((tm, LANES), row),
                   pl.BlockSpec((tm, LANES), row), pl.BlockSpec((1, LANES), lambda i: (0, 0))],
        out_shape=[jax.ShapeDtypeStruct((L, LANES), I32), jax.ShapeDtypeStruct((L, LANES), I32),
                   jax.ShapeDtypeStruct((L, LANES), F32), jax.ShapeDtypeStruct((1, LANES), F32)],
        scratch_shapes=[pltpu.VMEM((1, LANES), F32)],
        compiler_params=_cparams(("arbitrary",)),
        name="route",
    )(logits)


def _slab_rows(r):
    return pl.ds(pl.multiple_of(r * NSL, NSL), NSL)


def _dispatch_kernel(slot_ref, x_ref, buf_ref, o_ref, sem):
    del buf_ref
    tm = x_ref.shape[0] // NSL
    base = pl.program_id(0) * tm

    def row_copy(t, s):
        return pltpu.make_async_copy(x_ref.at[_slab_rows(t)], o_ref.at[_slab_rows(s)], sem)

    def issue(t, carry):
        for k in range(TOP_K):
            row_copy(t, slot_ref[(base + t) * TOP_K + k]).start()
        return carry

    lax.fori_loop(0, tm, issue, 0)

    def drain(t, carry):
        for k in range(TOP_K):
            row_copy(t, slot_ref[(base + t) * TOP_K + k]).wait()
        return carry

    lax.fori_loop(0, tm, drain, 0)


def _dispatch(slots, x_slab, n_slots):
    L = x_slab.shape[0] // NSL
    tm = min(256, L)
    buf = jnp.zeros((n_slots * NSL, LANES), x_slab.dtype)
    grid_spec = pltpu.PrefetchScalarGridSpec(
        num_scalar_prefetch=1,
        grid=(L // tm,),
        in_specs=[pl.BlockSpec((tm * NSL, LANES), lambda i, sl: (i, 0)),
                  pl.BlockSpec(memory_space=pl.ANY)],
        out_specs=pl.BlockSpec(memory_space=pl.ANY),
        scratch_shapes=[pltpu.SemaphoreType.DMA(())],
    )
    return pl.pallas_call(
        _dispatch_kernel,
        grid_spec=grid_spec,
        out_shape=jax.ShapeDtypeStruct((n_slots * NSL, LANES), x_slab.dtype),
        input_output_aliases={2: 0},
        compiler_params=_cparams(("arbitrary",)),
        name="dispatch",
    )(slots, x_slab, buf)


def _expert_kernel(be_ref, bv_ref, x_ref, wg_ref, wu_ref, bg_ref, bu_ref, wd_ref, bd_ref, o_ref,
                   xb_ref, acc_ref):
    i = pl.program_id(0)
    j = pl.program_id(1)
    nj = pl.num_programs(1)
    bm = xb_ref.shape[0]
    valid = bv_ref[i] > 0

    @pl.when(valid & (j == 0))
    def _():
        for s in range(NSL):
            xb_ref[:, s * LANES:(s + 1) * LANES] = x_ref[pl.ds(s, bm, stride=NSL), :].astype(BF16)
        acc_ref[...] = jnp.zeros_like(acc_ref)

    @pl.when(valid)
    def _():
        xb = xb_ref[...]
        gate = jnp.dot(xb, wg_ref[0].astype(BF16), preferred_element_type=F32) + bg_ref[0]
        up = jnp.dot(xb, wu_ref[0].astype(BF16), preferred_element_type=F32) + bu_ref[0]
        gate = jnp.minimum(gate, SWIGLU_LIMIT)
        up = jnp.clip(up, -SWIGLU_LIMIT, SWIGLU_LIMIT)
        glu = gate * _sigmoid(SWIGLU_ALPHA * gate)
        act = ((up + 1.0) * glu).astype(BF16)
        acc_ref[...] += jnp.dot(act, wd_ref[0].astype(BF16), preferred_element_type=F32)

    @pl.when(j == nj - 1)
    def _():
        res = jnp.where(valid, acc_ref[...] + bd_ref[0], 0.0)
        for s in range(NSL):
            o_ref[pl.ds(s, bm, stride=NSL), :] = res[:, s * LANES:(s + 1) * LANES]


def _experts(blk_e, blk_valid, xs_slab, w_gu, b_gu, w_dn, b_dn):
    n_slots = xs_slab.shape[0] // NSL
    ne, d, two_ff = w_gu.shape
    dff = two_ff // 2
    bm, tf = MOE_BM, MOE_TF
    nj = dff // tf
    nb = n_slots // bm

    def jeff(j, bv, i):
        return jnp.where(bv[i] > 0, j, nj - 1)

    grid_spec = pltpu.PrefetchScalarGridSpec(
        num_scalar_prefetch=2,
        grid=(nb, nj),
        in_specs=[pl.BlockSpec((bm * NSL, LANES), lambda i, j, be, bv: (i, 0)),
                  pl.BlockSpec((1, d, tf), lambda i, j, be, bv: (be[i], 0, jeff(j, bv, i))),
                  pl.BlockSpec((1, d, tf), lambda i, j, be, bv: (be[i], 0, nj + jeff(j, bv, i))),
                  pl.BlockSpec((1, 1, tf), lambda i, j, be, bv: (be[i], 0, jeff(j, bv, i))),
                  pl.BlockSpec((1, 1, tf), lambda i, j, be, bv: (be[i], 0, nj + jeff(j, bv, i))),
                  pl.BlockSpec((1, tf, d), lambda i, j, be, bv: (be[i], jeff(j, bv, i), 0)),
                  pl.BlockSpec((1, 1, d), lambda i, j, be, bv: (be[i], 0, 0))],
        out_specs=pl.BlockSpec((bm * NSL, LANES), lambda i, j, be, bv: (i, 0)),
        scratch_shapes=[pltpu.VMEM((bm, d), BF16), pltpu.VMEM((bm, d), F32)],
    )
    return pl.pallas_call(
        _expert_kernel,
        grid_spec=grid_spec,
        out_shape=jax.ShapeDtypeStruct((n_slots * NSL, LANES), F32),
        compiler_params=_cparams(("arbitrary", "arbitrary")),
        name="experts",
    )(blk_e, blk_valid, xs_slab, w_gu, w_gu, b_gu.reshape(ne, 1, two_ff), b_gu.reshape(ne, 1, two_ff),
      w_dn, b_dn.reshape(ne, 1, d))


def _combine_kernel(slot_ref, h_ref, p_ref, g2_ref, fw_ref, yb_ref, o_ref, g0_ref, g1_ref, g2b_ref, g3_ref, sem):
    gbufs = (g0_ref, g1_ref, g2b_ref, g3_ref)
    tm = h_ref.shape[0]
    base = pl.program_id(0) * tm

    def row_copy(t, k, s):
        return pltpu.make_async_copy(yb_ref.at[_slab_rows(s)], gbufs[k].at[_slab_rows(t)], sem)

    def issue(t, carry):
        for k in range(TOP_K):
            row_copy(t, k, slot_ref[(base + t) * TOP_K + k]).start()
        return carry

    lax.fori_loop(0, tm, issue, 0)

    def drain(t, carry):
        for k in range(TOP_K):
            row_copy(t, k, slot_ref[(base + t) * TOP_K + k]).wait()
        return carry

    lax.fori_loop(0, tm, drain, 0)

    p = p_ref[...]
    ssq = jnp.zeros((tm, 1), F32)
    for s in range(NSL):
        cs = slice(s * LANES, (s + 1) * LANES)
        moe = jnp.zeros((tm, LANES), F32)
        for k in range(TOP_K):
            moe = moe + p[:, k:k + 1] * gbufs[k][pl.ds(s, tm, stride=NSL), :]
        hs = h_ref[:, cs] + g2_ref[:, cs] * moe
        o_ref[:, cs] = hs
        ssq = ssq + jnp.sum(hs * hs, axis=-1, keepdims=True)
    scale = lax.rsqrt(ssq / (NSL * LANES) + EPS)
    o_ref[...] = o_ref[...] * scale * fw_ref[...]


def _combine(slots, h, p, g2, fw, yb_slab):
    L, d = h.shape
    tm = min(256, L)
    grid_spec = pltpu.PrefetchScalarGridSpec(
        num_scalar_prefetch=1,
        grid=(L // tm,),
        in_specs=[pl.BlockSpec((tm, d), lambda i, sl: (i, 0)),
                  pl.BlockSpec((tm, LANES), lambda i, sl: (i, 0)),
                  pl.BlockSpec((1, d), lambda i, sl: (0, 0)),
                  pl.BlockSpec((1, d), lambda i, sl: (0, 0)),
                  pl.BlockSpec(memory_space=pl.ANY)],
        out_specs=pl.BlockSpec((tm, d), lambda i, sl: (i, 0)),
        scratch_shapes=[pltpu.VMEM((tm * NSL, LANES), F32) for _ in range(TOP_K)] + [pltpu.SemaphoreType.DMA(())],
    )
    return pl.pallas_call(
        _combine_kernel,
        grid_spec=grid_spec,
        out_shape=jax.ShapeDtypeStruct((L, d), F32),
        compiler_params=_cparams(("arbitrary",)),
        name="combine",
    )(slots, h, p, g2, fw, yb_slab)


def _mixer_scans(x2, nw1, sh1, sc1, w_main, w_dt, conv_w, conv_b, ssm_vecs, rep, lb, row_len, ssd_init, hgrn_init):
    proj, dtr = _inproj(x2, nw1, sh1, sc1, w_main, w_dt)
    xs, bc = _conv_silu(proj, conv_w, conv_b, row_len)
    dtb, alog, amask = ssm_vecs
    ssd, hgrn = [], []
    for d in (0, 1):
        ssd.append(_ssd_scan(xs, bc, dtr, dtb, alog, amask[d], rep[d], ssd_init[d], rev=bool(d)))
        hgrn.append(_hgrn_scan(proj, lb[d], hgrn_init[d], rev=bool(d)))
    return proj, xs, ssd, hgrn


def kernel(x, c, ctx, c_ctx, w_ada, b_ada, norm1_w, norm2_w, w_in, conv_w, conv_b, dt_bias, a_log, d_skip,
           ssm_norm_w, hgrn_lb, hgrn_norm_w, w_branch_ssm, w_branch_hgrn, w_out, w_router, b_router,
           w_gate_up, b_gate_up, w_down, b_down, final_norm_w):
    bsz, seq, d = x.shape
    assert bsz == 1 and w_ada.shape[0] == 1
    ctx_len = ctx.shape[1]
    H = SSM_HEADS

    w = w_in[0]
    o_z = SSM_INNER
    o_xbc = o_z + SSM_INNER + 2 * SSM_GROUPS * SSM_STATE
    o_dt = o_xbc + 2 * H
    w_main = jnp.concatenate([w[:, :o_z], w[:, o_dt:], w[:, o_z:o_xbc]], axis=1).astype(BF16)
    w_dt = jnp.pad(w[:, o_xbc:o_dt], ((0, 0), (0, LANES - 2 * H))).astype(BF16)
    dtb = jnp.pad(dt_bias[0].reshape(1, 2 * H), ((0, 0), (0, LANES - 2 * H)))
    alog = jnp.pad(a_log[0].reshape(1, 2 * H), ((0, 0), (0, LANES - 2 * H)))
    lane = jnp.arange(LANES)
    amask = [((lane >= dd * H) & (lane < (dd + 1) * H)).astype(F32).reshape(1, LANES) for dd in (0, 1)]
    chan_head = jnp.arange(SSM_INNER) // SSM_HEAD_DIM
    rep = [(lane[:, None] == (chan_head[None, :] + dd * H)).astype(F32) for dd in (0, 1)]
    dsk = jnp.repeat(d_skip[0], SSM_HEAD_DIM).reshape(1, SSM_INNER)
    lb = [hgrn_lb[dd] for dd in (0, 1)]
    wbs = w_branch_ssm[0].astype(BF16)
    wbh = w_branch_hgrn[0].astype(BF16)
    wo = w_out[0].astype(BF16)
    wr = jnp.pad(w_router[0], ((0, 0), (0, LANES - N_EXPERTS)))
    br = jnp.pad(b_router[0].reshape(1, N_EXPERTS), ((0, 0), (0, LANES - N_EXPERTS)))
    row = lambda v: v.reshape(1, -1)

    cc = jnp.zeros((8, d), F32).at[0].set(c[0]).at[1].set(c_ctx)
    mod = _modulation(cc, w_ada[0], b_ada[0])
    sh1, sc1, g1, sh2, sc2, g2 = [mod[0:1, k * d:(k + 1) * d] for k in range(6)]
    csh1, csc1 = mod[1:2, 0:d], mod[1:2, d:2 * d]

    ssm_vecs = (dtb, alog, amask)
    zs = jnp.zeros((SSM_GROUPS, SSM_STATE, SSM_GROUP_W), F32)
    zh = jnp.zeros((HGRN_HEADS, HGRN_V, HGRN_K), F32)
    nw1 = row(norm1_w[0])

    _, _, ssd_c, hgrn_c = _mixer_scans(ctx[0], nw1, csh1, csc1, w_main, w_dt, conv_w[0], conv_b[0], ssm_vecs,
                                       rep, lb, ctx_len, (zs, zs), (zh, zh))
    ssd_init = (ssd_c[0][1], ssd_c[1][1])
    hgrn_init = (hgrn_c[0][1], hgrn_c[1][1])

    x2 = x[0]
    proj, xs, ssd_l, hgrn_l = _mixer_scans(x2, nw1, sh1, sc1, w_main, w_dt, conv_w[0], conv_b[0], ssm_vecs,
                                           rep, lb, GRID_W, ssd_init, hgrn_init)
    y_ssm = _ssd_post(ssd_l[0][0], ssd_l[1][0], xs, proj, dsk, row(ssm_norm_w[0]))
    y_hgrn = _hgrn_post(hgrn_l[0][0], hgrn_l[1][0], proj, row(hgrn_norm_w[0]))
    y = _merge(y_ssm, y_hgrn, proj, wbs, wbh)
    h, hn2, logits = _outproj(y, x2, wo, g1, row(norm2_w[0]), sh2, sc2, wr, br)

    idx, pos, p, counts = _route(logits)
    counts = counts[0, :N_EXPERTS].astype(I32)
    bm = MOE_BM
    n_assign = seq * TOP_K
    n_blocks = -(-(n_assign + N_EXPERTS * (bm - 1)) // bm)
    n_slots = n_blocks * bm
    padded = (counts + bm - 1) // bm * bm
    pad_end = jnp.cumsum(padded)
    pad_start = pad_end - padded
    slots = (pad_start[idx[:, :TOP_K]] + pos[:, :TOP_K]).reshape(-1).astype(I32)
    blk_start = jnp.arange(n_blocks, dtype=I32) * bm
    blk_e = jnp.minimum(jnp.searchsorted(pad_end, blk_start, side="right"), N_EXPERTS - 1).astype(I32)
    blk_valid = (blk_start < pad_end[-1]).astype(I32)

    assert d == NSL * LANES
    xs_slab = _dispatch(slots, hn2.reshape(seq * NSL, LANES), n_slots)
    yb_slab = _experts(blk_e, blk_valid, xs_slab, w_gate_up[0], b_gate_up[0], w_down[0], b_down[0])
    out = _combine(slots, h, p, g2, row(final_norm_w), yb_slab)
    return out.reshape(bsz, seq, d)
```

```python
import functools

import jax
import jax.numpy as jnp
from jax import lax
from jax.experimental import pallas as pl
from jax.experimental.pallas import tpu as pltpu

F32 = jnp.float32
BF16 = jnp.bfloat16
I32 = jnp.int32
U32 = jnp.uint32
HI = lax.Precision.HIGHEST

EPS = 1e-6
GRID_W = 64

LANES = 128
NSL = 8
HALF_D = NSL * LANES
SSM_HEADS = 32
SSM_HEAD_DIM = 64
SSM_GROUPS = 4
SSM_STATE = 128
SSM_INNER = SSM_HEADS * SSM_HEAD_DIM
SSM_GROUP_W = SSM_INNER // SSM_GROUPS
SSM_CHUNK = 128
HGRN_HEADS = 16
HGRN_K = 128
HGRN_V = 128
HGRN_CHUNK = 64
HGRN_SUB = 16
HGRN_HEAD_GROUP = 8
HGRN_SAFE_DECAY = 60.0
N_EXPERTS = 32
TOP_K = 4
SWIGLU_LIMIT = 7.0
SWIGLU_ALPHA = 1.702
MOE_BM = 1024
MOE_HALF = 512
MOE_TF = 512
VMEM_LIMIT = 56 * 1024 * 1024

COL_Z, COL_Q, COL_F, COL_I, COL_G, COL_GS, COL_GH, COL_XS = 0, 1, 2, 4, 5, 6, 7, 8


def _sigmoid(x):
    return 1.0 / (1.0 + jnp.exp(-x))


def _cparams(sem):
    return pltpu.CompilerParams(dimension_semantics=sem, vmem_limit_bytes=VMEM_LIMIT)


def _mod_kernel(c_ref, w_ref, b_ref, o_ref):
    c = c_ref[...]
    s = c * _sigmoid(c)
    o_ref[...] = jnp.dot(s, w_ref[...], precision=HI, preferred_element_type=F32) + b_ref[...]


def _modulation(cc, w_ada, b_ada):
    d, n = w_ada.shape
    tn = 1024
    return pl.pallas_call(
        _mod_kernel,
        grid=(n // tn,),
        in_specs=[pl.BlockSpec((8, d), lambda j: (0, 0)),
                  pl.BlockSpec((d, tn), lambda j: (0, j)),
                  pl.BlockSpec((1, tn), lambda j: (0, j))],
        out_specs=pl.BlockSpec((8, tn), lambda j: (0, j)),
        out_shape=jax.ShapeDtypeStruct((8, n), F32),
        compiler_params=_cparams(("arbitrary",)),
        name="adaln_mod",
    )(cc, w_ada, b_ada.reshape(1, n))


def _inproj_kernel(x_ref, nw_ref, sh_ref, sc_ref, w_ref, wdt_ref, o_ref, odt_ref, hn_ref):
    @pl.when(pl.program_id(1) == 0)
    def _():
        x = x_ref[...]
        ms = jnp.mean(x * x, axis=-1, keepdims=True)
        y = x * lax.rsqrt(ms + EPS) * nw_ref[...]
        hn = (y * (1.0 + sc_ref[...]) + sh_ref[...]).astype(BF16)
        hn_ref[...] = hn
        odt_ref[...] = jnp.dot(hn, wdt_ref[...], preferred_element_type=F32)

    o_ref[...] = jnp.dot(hn_ref[...], w_ref[...], preferred_element_type=F32)


def _inproj(x2, nw, sh, sc, w_main, w_dt):
    L, d = x2.shape
    n = w_main.shape[1]
    tm = min(1024, L)
    tn = 1024
    return pl.pallas_call(
        _inproj_kernel,
        grid=(L // tm, n // tn),
        in_specs=[pl.BlockSpec((tm, d), lambda i, j: (i, 0)),
                  pl.BlockSpec((1, d), lambda i, j: (0, 0)),
                  pl.BlockSpec((1, d), lambda i, j: (0, 0)),
                  pl.BlockSpec((1, d), lambda i, j: (0, 0)),
                  pl.BlockSpec((d, tn), lambda i, j: (0, j)),
                  pl.BlockSpec((d, LANES), lambda i, j: (0, 0))],
        out_specs=[pl.BlockSpec((tm, tn), lambda i, j: (i, j)),
                   pl.BlockSpec((tm, LANES), lambda i, j: (i, 0))],
        out_shape=[jax.ShapeDtypeStruct((L, n), F32), jax.ShapeDtypeStruct((L, LANES), F32)],
        scratch_shapes=[pltpu.VMEM((tm, d), BF16)],
        compiler_params=_cparams(("parallel", "arbitrary")),
        name="inproj",
    )(x2, nw, sh, sc, w_main, w_dt)


def _conv_body(u, w_ref, b_ref, row_len):
    tb = u.shape[0]
    pos = lax.broadcasted_iota(I32, u.shape, 0) & (row_len - 1)
    acc = u * w_ref[2:3, :] + b_ref[...]
    for k, off in ((0, -2), (1, -1), (3, 1)):
        shifted = pltpu.roll(u, shift=(-off) % tb, axis=0)
        valid = (pos + off >= 0) & (pos + off < row_len)
        acc = acc + jnp.where(valid, shifted, 0.0) * w_ref[k:k + 1, :]
    return acc * _sigmoid(acc)


def _conv_kernel(ux_ref, ub_ref, wx_ref, wb_ref, bx_ref, bb_ref, ox_ref, ob_ref, *, row_len):
    ox_ref[...] = _conv_body(ux_ref[...], wx_ref, bx_ref, row_len)
    ob_ref[...] = _conv_body(ub_ref[...], wb_ref, bb_ref, row_len)


def _conv_silu(proj, conv_w, conv_b, row_len):
    L = proj.shape[0]
    d = SSM_INNER
    nbc = 2 * SSM_GROUPS * SSM_STATE
    tb = max(row_len, min(512, L))
    wx, wb = conv_w[:, :d], conv_w[:, d:]
    bx, bb = conv_b[:d].reshape(1, d), conv_b[d:].reshape(1, nbc)
    return pl.pallas_call(
        functools.partial(_conv_kernel, row_len=row_len),
        grid=(L // tb,),
        in_specs=[pl.BlockSpec((tb, d), lambda i: (i, COL_XS)),
                  pl.BlockSpec((tb, nbc), lambda i: (i, COL_XS * 2 + 2)),
                  pl.BlockSpec((4, d), lambda i: (0, 0)),
                  pl.BlockSpec((4, nbc), lambda i: (0, 0)),
                  pl.BlockSpec((1, d), lambda i: (0, 0)),
                  pl.BlockSpec((1, nbc), lambda i: (0, 0))],
        out_specs=[pl.BlockSpec((tb, d), lambda i: (i, 0)),
                   pl.BlockSpec((tb, nbc), lambda i: (i, 0))],
        out_shape=[jax.ShapeDtypeStruct((L, d), F32), jax.ShapeDtypeStruct((L, nbc), F32)],
        compiler_params=_cparams(("parallel",)),
        name="conv_silu",
    )(proj, proj, wx, wb, bx, bb)


def _ssd_kernel(xs_ref, bc_ref, dt_ref, dtb_ref, alog_ref, amask_ref, rep_ref, s0_ref,
                y_ref, sfin_ref, S_ref, *, rev, lane0):
    n = pl.program_id(0)

    @pl.when(n == 0)
    def _():
        S_ref[...] = s0_ref[...]

    Q = xs_ref.shape[0]
    N = SSM_STATE
    GW = SSM_GROUP_W
    ti = lax.broadcasted_iota(I32, (Q, Q), 0)
    si = lax.broadcasted_iota(I32, (Q, Q), 1)
    allowed = (si >= ti) if rev else (si <= ti)
    tri = allowed.astype(F32)

    raw = dt_ref[...] + dtb_ref[...]
    dt = jnp.maximum(raw, 0.0) + jnp.log1p(jnp.exp(-jnp.abs(raw)))
    loga = dt * (-jnp.exp(alog_ref[...]) * amask_ref[...])
    cs = jnp.dot(tri, loga, precision=HI, preferred_element_type=F32)
    csT = cs.T
    tot = cs[0:1, :] if rev else cs[Q - 1:Q, :]
    fac = jnp.concatenate([jnp.exp(cs), jnp.exp(tot - cs), dt], axis=0).astype(BF16)
    tot8 = jnp.broadcast_to(tot, (8, LANES))
    t_hi = tot8.astype(BF16)
    t_r = tot8 - t_hi.astype(F32)
    t_mid = t_r.astype(BF16)
    t_lo = (t_r - t_mid.astype(F32)).astype(BF16)
    tot3 = jnp.concatenate([t_hi, t_mid, t_lo], axis=0)
    lane = lax.broadcasted_iota(I32, (Q, LANES), 1)
    lo_mask = lane < SSM_HEAD_DIM

    for g in range(SSM_GROUPS):
        rep_g = rep_ref[:, g * GW:(g + 1) * GW]
        fx = jnp.dot(fac, rep_g, preferred_element_type=F32)
        e_in, e_out, dt_x = fx[0:Q], fx[Q:2 * Q], fx[2 * Q:3 * Q]
        tx = jnp.dot(tot3, rep_g, preferred_element_type=F32)
        dec = jnp.exp((tx[0:1] + tx[8:9]) + tx[16:17])
        xdt = xs_ref[:, g * GW:(g + 1) * GW] * dt_x
        xw = (xdt * e_out).astype(BF16)
        xdt_b = xdt.astype(BF16)
        Bg = bc_ref[:, g * N:(g + 1) * N].astype(BF16)
        Cg = bc_ref[:, (SSM_GROUPS + g) * N:(SSM_GROUPS + g + 1) * N].astype(BF16)
        cb = lax.dot_general(Cg, Bg, (((1,), (1,)), ((), ())), preferred_element_type=F32)
        Sg = S_ref[g]
        y_off = jnp.dot(Cg, Sg.astype(BF16), preferred_element_type=F32) * e_in
        upd = lax.dot_general(Bg, xw, (((0,), (0,)), ((), ())), preferred_element_type=F32)
        S_ref[g] = Sg * dec + upd
        for jp in range(GW // LANES):
            h0 = g * (GW // SSM_HEAD_DIM) + 2 * jp
            ms = []
            for hh in (h0, h0 + 1):
                col = cs[:, lane0 + hh:lane0 + hh + 1]
                row = csT[lane0 + hh:lane0 + hh + 1, :]
                dec_ts = jnp.exp(jnp.where(allowed, col - row, -jnp.inf))
                ms.append((cb * dec_ts).astype(BF16))
            m2 = jnp.concatenate(ms, axis=1)
            x2 = xdt_b[:, jp * LANES:(jp + 1) * LANES]
            r2 = jnp.concatenate([jnp.where(lo_mask, x2, jnp.zeros_like(x2)),
                                  jnp.where(lo_mask, jnp.zeros_like(x2), x2)], axis=0)
            y_d = jnp.dot(m2, r2, preferred_element_type=F32)
            c0 = h0 * SSM_HEAD_DIM
            y_ref[:, c0:c0 + LANES] = y_d + y_off[:, jp * LANES:(jp + 1) * LANES]

    @pl.when(n == pl.num_programs(0) - 1)
    def _():
        sfin_ref[...] = S_ref[...]


def _ssd_scan(xs, bc, dt, dtb, alog, amask, rep, s0, rev):
    L, d = xs.shape
    Q = SSM_CHUNK
    nc = L // Q
    idx = (lambda n: (nc - 1 - n, 0)) if rev else (lambda n: (n, 0))
    const2 = lambda n: (0, 0)
    const3 = lambda n: (0, 0, 0)
    sshape = (SSM_GROUPS, SSM_STATE, SSM_GROUP_W)
    return pl.pallas_call(
        functools.partial(_ssd_kernel, rev=rev, lane0=SSM_HEADS if rev else 0),
        grid=(nc,),
        in_specs=[pl.BlockSpec((Q, d), idx),
                  pl.BlockSpec((Q, bc.shape[1]), idx),
                  pl.BlockSpec((Q, LANES), idx),
                  pl.BlockSpec((1, LANES), const2),
                  pl.BlockSpec((1, LANES), const2),
                  pl.BlockSpec((1, LANES), const2),
                  pl.BlockSpec((LANES, d), const2),
                  pl.BlockSpec(sshape, const3)],
        out_specs=[pl.BlockSpec((Q, d), idx),
                   pl.BlockSpec(sshape, const3)],
        out_shape=[jax.ShapeDtypeStruct((L, d), F32), jax.ShapeDtypeStruct(sshape, F32)],
        scratch_shapes=[pltpu.VMEM(sshape, F32)],
        compiler_params=_cparams(("arbitrary",)),
        name="ssd_bwd" if rev else "ssd_fwd",
    )(xs, bc, dt, dtb, alog, amask, rep, s0)


def _ssd_post_kernel(yf_ref, yb_ref, xs_ref, z_ref, dsk_ref, nw_ref, o_ref):
    z = z_ref[...]
    y = (yf_ref[...] + yb_ref[...] + dsk_ref[...] * xs_ref[...]) * (z * _sigmoid(z))
    for g in range(SSM_GROUPS):
        sl = slice(g * SSM_GROUP_W, (g + 1) * SSM_GROUP_W)
        yg = y[:, sl]
        ms = jnp.mean(yg * yg, axis=-1, keepdims=True)
        o_ref[:, sl] = (yg * lax.rsqrt(ms + EPS) * nw_ref[:, sl]).astype(o_ref.dtype)


def _ssd_post(yf, yb, xs, proj, dsk, nw):
    L, d = yf.shape
    tm = min(512, L)
    row = lambda i: (i, 0)
    const = lambda i: (0, 0)
    return pl.pallas_call(
        _ssd_post_kernel,
        grid=(L // tm,),
        in_specs=[pl.BlockSpec((tm, d), row), pl.BlockSpec((tm, d), row), pl.BlockSpec((tm, d), row),
                  pl.BlockSpec((tm, d), lambda i: (i, COL_Z)),
                  pl.BlockSpec((1, d), const), pl.BlockSpec((1, d), const)],
        out_specs=pl.BlockSpec((tm, d), row),
        out_shape=jax.ShapeDtypeStruct((L, d), BF16),
        compiler_params=_cparams(("parallel",)),
        name="ssd_post",
    )(yf, yb, xs, proj, dsk, nw)


def _hgrn_kernel(q_ref, f_ref, i_ref, lb_ref, s0_ref, o_ref, sfin_ref, ST_ref, kk_ref, bcs_ref, lf_ref, v_ref,
                 vT_ref, *, rev):
    n = pl.program_id(0)

    @pl.when(n == 0)
    def _():
        ST_ref[...] = s0_ref[...]

    C = q_ref.shape[0]
    c = HGRN_SUB
    nsub = C // c
    K = HGRN_K
    d = HGRN_HEADS * K
    ti = lax.broadcasted_iota(I32, (C, C), 0)
    si = lax.broadcasted_iota(I32, (C, C), 1)
    allowed = (si >= ti) if rev else (si <= ti)
    bt, bs = ti // c, si // c
    dist = (bs - bt) if rev else (bt - bs)
    order = list(range(nsub - 1, -1, -1)) if rev else list(range(nsub))

    for h in range(HGRN_HEADS):
        sl = slice(h * K, (h + 1) * K)
        lbr = lb_ref[:, sl]
        lmax = jnp.max(lbr, axis=0, keepdims=True)
        le = jnp.exp(lbr - lmax)
        lb = le[0:1, :] / jnp.sum(le, axis=0, keepdims=True)
        fr = f_ref[:, sl]
        e = jnp.exp(-jnp.abs(fr))
        rinv = 1.0 / (1.0 + e)
        pos = fr >= 0.0
        sg = jnp.where(pos, rinv, e * rinv)
        sgn = jnp.where(pos, e * rinv, rinv)
        logf = jnp.log(lb + (1.0 - lb) * sg)
        kk_ref[:, sl] = (1.0 - lb) * sgn
        iv = i_ref[:, sl]
        vv = iv * _sigmoid(iv)
        v_ref[:, sl] = vv
        vT_ref[sl, :] = vv.T.astype(BF16)
        hi = logf.astype(BF16)
        r1 = logf - hi.astype(F32)
        mid = r1.astype(BF16)
        lo = (r1 - mid.astype(F32)).astype(BF16)
        lf_ref[:, h * K:(h + 1) * K] = hi
        lf_ref[:, d + h * K:d + (h + 1) * K] = mid
        lf_ref[:, 2 * d + h * K:2 * d + (h + 1) * K] = lo
    cs3 = jnp.dot(allowed.astype(BF16), lf_ref[...], preferred_element_type=F32)
    bcs_ref[...] = (cs3[:, :d] + cs3[:, d:2 * d]) + cs3[:, 2 * d:]

    def levels(col):
        r_in, r_out = {}, {}
        for p, sc_i in enumerate(order):
            last = sc_i * c + (0 if rev else c - 1)
            r_out[p] = bcs_ref[last:last + 1, col]
            r_in[p] = r_out[p - 1] if p > 0 else jnp.zeros_like(r_out[p])
        return r_in, r_out

    r_in_all, r_out_all = levels(slice(0, d))
    worst = r_in_all[0] - r_out_all[0]
    for p in range(1, nsub):
        worst = jnp.maximum(worst, r_in_all[p] - r_out_all[p])
    safe = jnp.max(worst) < HGRN_SAFE_DECAY

    def heads(exact_pairwise):
        t16 = lax.broadcasted_iota(I32, (c, K), 0)
        ones_b = jnp.ones((K, K), BF16)
        t2 = lax.broadcasted_iota(I32, (C, 2 * C), 0)
        col2 = lax.broadcasted_iota(I32, (C, 2 * C), 1)
        s2 = col2 & (C - 1)
        dist2 = (s2 // c - t2 // c) if rev else (t2 // c - s2 // c)
        dist2 = jnp.where(col2 < C, dist2, -1)
        same_blk = (col2 >= C) & (t2 // c == s2 // c) & ((s2 >= t2) if rev else (s2 <= t2))
        band = {dd: dist2 == dd for dd in range(1, nsub)}
        def matmuls(h):
            sl = slice(h * K, (h + 1) * K)
            bcs = bcs_ref[:, sl]
            kk = kk_ref[:, sl]
            qq = q_ref[:, sl] * (K ** -0.5)
            tot = bcs[0:1, :] if rev else bcs[C - 1:C, :]

            ST = ST_ref[h]
            q_in = (qq * jnp.exp(bcs)).astype(BF16)
            o = lax.dot_general(q_in, ST.astype(BF16), (((1,), (1,)), ((), ())), preferred_element_type=F32)
            k_end = (kk * jnp.exp(tot - bcs)).astype(BF16)
            upd = jnp.dot(vT_ref[sl, :], k_end, preferred_element_type=F32)
            new_state = ST * jnp.exp(tot) + upd

            r_in, r_out = levels(sl)
            rows_in = [None] * nsub
            rows_out = [None] * nsub
            for p, sc_i in enumerate(order):
                rows_in[sc_i] = jnp.broadcast_to(r_in[p], (c, K))
                rows_out[sc_i] = jnp.broadcast_to(r_out[p], (c, K))
            R_in = jnp.concatenate(rows_in, axis=0)
            R_out = jnp.concatenate(rows_out, axis=0)
            qh = qq * jnp.exp(bcs - R_in)
            kt = kk * jnp.exp(R_out - bcs)
            lhs = [qh]
            for dd in range(2, nsub):
                rows_x = [None] * nsub
                for p, sc_i in enumerate(order):
                    if p >= dd:
                        rows_x[sc_i] = jnp.broadcast_to(jnp.exp(r_in[p] - r_out[p - dd]), (c, K))
                    else:
                        rows_x[sc_i] = jnp.zeros((c, K), F32)
                lhs.append(qh * jnp.concatenate(rows_x, axis=0))
            lhs = jnp.concatenate(lhs, axis=0).astype(BF16)
            if exact_pairwise:
                rhs = kt.astype(BF16)
            else:
                k_diag = kk * jnp.exp(R_in - bcs)
                rhs = jnp.concatenate([kt, k_diag], axis=0).astype(BF16)
            a = lax.dot_general(lhs, rhs, (((1,), (1,)), ((), ())), preferred_element_type=F32)
            return o, a, new_state

        def finish(h, o, a, new_state):
            sl = slice(h * K, (h + 1) * K)
            ST_ref[h] = new_state
            vv = v_ref[:, sl]
            vb = vv.astype(BF16)
            if exact_pairwise:
                bcs = bcs_ref[:, sl]
                kk = kk_ref[:, sl]
                qq = q_ref[:, sl] * (K ** -0.5)
                att = jnp.where(dist == 1, a[0:C], 0.0)
                for dd in range(2, nsub):
                    att = att + jnp.where(dist == dd, a[(dd - 1) * C:dd * C], 0.0)
                o = o + jnp.dot(att.astype(BF16), vb, preferred_element_type=F32)
                for sc_i in range(nsub):
                    r0 = sc_i * c
                    b_i = bcs[r0:r0 + c, :]
                    q_i = qq[r0:r0 + c, :]
                    ws = []
                    for s in range(c):
                        ok = (t16 <= s) if rev else (t16 >= s)
                        ex = jnp.exp(jnp.where(ok, b_i - bcs[r0 + s:r0 + s + 1, :], -jnp.inf))
                        ws.append((q_i * ex * kk[r0 + s:r0 + s + 1, :]).astype(BF16))
                    w_all = jnp.concatenate(ws, axis=0)
                    a_all = jnp.dot(w_all, ones_b, preferred_element_type=F32)
                    o_d = jnp.zeros((c, HGRN_V), F32)
                    for s in range(c):
                        o_d = o_d + a_all[s * c:(s + 1) * c, :] * vv[r0 + s:r0 + s + 1, :]
                    o_ref[r0:r0 + c, sl] = o[r0:r0 + c, :] + o_d
            else:
                att = jnp.where(band[1] | same_blk, a[0:C], 0.0)
                for dd in range(2, nsub):
                    att = att + jnp.where(band[dd], a[(dd - 1) * C:dd * C], 0.0)
                v2 = jnp.concatenate([vb, vb], axis=0)
                o_ref[:, sl] = o + jnp.dot(att.astype(BF16), v2, preferred_element_type=F32)

        group = 1 if exact_pairwise else HGRN_HEAD_GROUP
        for h0 in range(0, HGRN_HEADS, group):
            staged = [matmuls(h) for h in range(h0, h0 + group)]
            for h, vals in zip(range(h0, h0 + group), staged):
                finish(h, *vals)

    @pl.when(safe)
    def _():
        heads(False)

    @pl.when(jnp.logical_not(safe))
    def _():
        heads(True)

    @pl.when(n == pl.num_programs(0) - 1)
    def _():
        sfin_ref[...] = ST_ref[...]


def _hgrn_scan(proj, lb_d, s0, rev):
    L = proj.shape[0]
    d = HGRN_HEADS * HGRN_K
    C = HGRN_CHUNK
    nc = L // C
    fcol = COL_F + (1 if rev else 0)
    blk = lambda n: (nc - 1 - n) if rev else n
    sshape = (HGRN_HEADS, HGRN_V, HGRN_K)
    const3 = lambda n: (0, 0, 0)
    return pl.pallas_call(
        functools.partial(_hgrn_kernel, rev=rev),
        grid=(nc,),
        in_specs=[pl.BlockSpec((C, d), lambda n: (blk(n), COL_Q)),
                  pl.BlockSpec((C, d), lambda n: (blk(n), fcol)),
                  pl.BlockSpec((C, d), lambda n: (blk(n), COL_I)),
                  pl.BlockSpec(lb_d.shape, lambda n: (0, 0)),
                  pl.BlockSpec(sshape, const3)],
        out_specs=[pl.BlockSpec((C, d), lambda n: (blk(n), 0)),
                   pl.BlockSpec(sshape, const3)],
        out_shape=[jax.ShapeDtypeStruct((L, d), F32), jax.ShapeDtypeStruct(sshape, F32)],
        scratch_shapes=[pltpu.VMEM(sshape, F32), pltpu.VMEM((C, d), F32), pltpu.VMEM((C, d), F32),
                        pltpu.VMEM((C, 3 * d), BF16), pltpu.VMEM((C, d), F32), pltpu.VMEM((d, C), BF16)],
        compiler_params=_cparams(("arbitrary",)),
        name="hgrn_bwd" if rev else "hgrn_fwd",
    )(proj, proj, proj, lb_d, s0)


def _hgrn_post_kernel(of_ref, ob_ref, g_ref, nw_ref, o_ref):
    nw = nw_ref[...]
    for h in range(HGRN_HEADS):
        sl = slice(h * HGRN_V, (h + 1) * HGRN_V)
        o = of_ref[:, sl] + ob_ref[:, sl]
        ms = jnp.mean(o * o, axis=-1, keepdims=True)
        g = g_ref[:, sl]
        o_ref[:, sl] = (o * lax.rsqrt(ms + EPS) * nw * (g * _sigmoid(g))).astype(o_ref.dtype)


def _hgrn_post(of, ob, proj, nw):
    L, d = of.shape
    tm = min(512, L)
    row = lambda i: (i, 0)
    return pl.pallas_call(
        _hgrn_post_kernel,
        grid=(L // tm,),
        in_specs=[pl.BlockSpec((tm, d), row), pl.BlockSpec((tm, d), row),
                  pl.BlockSpec((tm, d), lambda i: (i, COL_G)),
                  pl.BlockSpec((1, HGRN_V), lambda i: (0, 0))],
        out_specs=pl.BlockSpec((tm, d), row),
        out_shape=jax.ShapeDtypeStruct((L, d), BF16),
        compiler_params=_cparams(("parallel",)),
        name="hgrn_post",
    )(of, ob, proj, nw)


def _merge_kernel(ys_ref, yh_ref, gs_ref, gh_ref, wbs_ref, wbh_ref, o_ref):
    a = jnp.dot(ys_ref[...], wbs_ref[...], preferred_element_type=F32)
    b = jnp.dot(yh_ref[...], wbh_ref[...], preferred_element_type=F32)
    o_ref[...] = (_sigmoid(gs_ref[...]) * a + _sigmoid(gh_ref[...]) * b).astype(o_ref.dtype)


def _merge(ys, yh, proj, wbs, wbh):
    L, d = ys.shape
    tm = min(512, L)
    tn = 512
    nj = d // tn
    return pl.pallas_call(
        _merge_kernel,
        grid=(L // tm, nj),
        in_specs=[pl.BlockSpec((tm, d), lambda i, j: (i, 0)),
                  pl.BlockSpec((tm, d), lambda i, j: (i, 0)),
                  pl.BlockSpec((tm, tn), lambda i, j: (i, COL_GS * nj + j)),
                  pl.BlockSpec((tm, tn), lambda i, j: (i, COL_GH * nj + j)),
                  pl.BlockSpec((d, tn), lambda i, j: (0, j)),
                  pl.BlockSpec((d, tn), lambda i, j: (0, j))],
        out_specs=pl.BlockSpec((tm, tn), lambda i, j: (i, j)),
        out_shape=jax.ShapeDtypeStruct((L, d), BF16),
        compiler_params=_cparams(("parallel", "arbitrary")),
        name="merge",
    )(ys, yh, proj, proj, wbs, wbh)


def _outproj_kernel(y_ref, x_ref, wo_ref, g1_ref, nw_ref, sh_ref, sc_ref, wr_ref, br_ref,
                    h_ref, hn_ref, lg_ref):
    h = x_ref[...] + g1_ref[...] * jnp.dot(y_ref[...], wo_ref[...], preferred_element_type=F32)
    h_ref[...] = h
    ms = jnp.mean(h * h, axis=-1, keepdims=True)
    hn = h * lax.rsqrt(ms + EPS) * nw_ref[...] * (1.0 + sc_ref[...]) + sh_ref[...]
    h_hi = hn.astype(BF16)
    h_lo = (hn - h_hi.astype(F32)).astype(BF16)
    wr = wr_ref[...]
    w_hi = wr.astype(BF16)
    w_lo = (wr - w_hi.astype(F32)).astype(BF16)
    lg = jnp.dot(h_hi, w_hi, preferred_element_type=F32)
    lg = lg + (jnp.dot(h_hi, w_lo, preferred_element_type=F32) + jnp.dot(h_lo, w_hi, preferred_element_type=F32))
    lg_ref[...] = lg + br_ref[...]
    _store_packed_slabs(hn_ref, hn)


def _outproj(y, x2, wo, g1, nw, sh, sc, wr, br):
    L, d = x2.shape
    tm = min(256, L)
    row = lambda i: (i, 0)
    const = lambda i: (0, 0)
    vec = pl.BlockSpec((1, d), const)
    return pl.pallas_call(
        _outproj_kernel,
        grid=(L // tm,),
        in_specs=[pl.BlockSpec((tm, d), row), pl.BlockSpec((tm, d), row),
                  pl.BlockSpec((d, d), const), vec, vec, vec, vec,
                  pl.BlockSpec((d, LANES), const), pl.BlockSpec((1, LANES), const)],
        out_specs=[pl.BlockSpec((tm, d), row), pl.BlockSpec((tm * NSL, LANES), row), pl.BlockSpec((tm, LANES), row)],
        out_shape=[jax.ShapeDtypeStruct((L, d), F32), jax.ShapeDtypeStruct((L * NSL, LANES), U32),
                   jax.ShapeDtypeStruct((L, LANES), F32)],
        compiler_params=_cparams(("parallel",)),
        name="outproj",
    )(y, x2, wo, g1, nw, sh, sc, wr, br)


def _route_kernel(lg_ref, idx_ref, pos_ref, p_ref, cnt_ref, carry_ref):
    n = pl.program_id(0)

    @pl.when(n == 0)
    def _():
        carry_ref[...] = jnp.zeros_like(carry_ref)

    tm = lg_ref.shape[0]
    lane_i = lax.broadcasted_iota(I32, (tm, LANES), 1)
    lane = lane_i.astype(F32)
    neg = jnp.float32(-jnp.inf)
    l = jnp.where(lane_i < N_EXPERTS, lg_ref[...], neg)
    vals, sels, ids = [], [], []
    for _ in range(TOP_K):
        m = jnp.max(l, axis=-1, keepdims=True)
        i_k = jnp.min(jnp.where(l == m, lane, float(LANES)), axis=-1, keepdims=True)
        sel = lane == i_k
        l = jnp.where(sel, neg, l)
        vals.append(m)
        sels.append(sel)
        ids.append(i_k)
    es = [jnp.exp(v - vals[0]) for v in vals]
    den = es[0]
    for e in es[1:]:
        den = den + e
    onehot = jnp.zeros((tm, LANES), F32)
    for sel in sels:
        onehot = onehot + jnp.where(sel, 1.0, 0.0)
    ti = lax.broadcasted_iota(I32, (tm, tm), 0)
    si = lax.broadcasted_iota(I32, (tm, tm), 1)
    strict = jnp.where(si < ti, 1.0, 0.0).astype(BF16)
    before = jnp.dot(strict, onehot.astype(BF16), preferred_element_type=F32) + carry_ref[...]
    idx_out = jnp.zeros((tm, LANES), F32)
    pos_out = jnp.zeros((tm, LANES), F32)
    p_out = jnp.zeros((tm, LANES), F32)
    for k in range(TOP_K):
        pos_k = jnp.sum(jnp.where(sels[k], before, 0.0), axis=-1, keepdims=True)
        idx_out = jnp.where(lane_i == k, ids[k], idx_out)
        pos_out = jnp.where(lane_i == k, pos_k, pos_out)
        p_out = jnp.where(lane_i == k, es[k] / den, p_out)
    idx_ref[...] = idx_out.astype(I32)
    pos_ref[...] = pos_out.astype(I32)
    p_ref[...] = p_out
    carry_ref[...] = carry_ref[...] + jnp.sum(onehot, axis=0, keepdims=True)
    cnt_ref[...] = carry_ref[...]


def _route(logits):
    L = logits.shape[0]
    tm = min(256, L)
    row = lambda i: (i, 0)
    return pl.pallas_call(
        _route_kernel,
        grid=(L // tm,),
        in_specs=[pl.BlockSpec((tm, LANES), row)],
        out_specs=[pl.BlockSpec((tm, LANES), row), pl.BlockSpec((tm, LANES), row),
                   pl.BlockSpec((tm, LANES), row), pl.BlockSpec((1, LANES), lambda i: (0, 0))],
        out_shape=[jax.ShapeDtypeStruct((L, LANES), I32), jax.ShapeDtypeStruct((L, LANES), I32),
                   jax.ShapeDtypeStruct((L, LANES), F32), jax.ShapeDtypeStruct((1, LANES), F32)],
        scratch_shapes=[pltpu.VMEM((1, LANES), F32)],
        compiler_params=_cparams(("arbitrary",)),
        name="route",
    )(logits)


def _slab_rows(r):
    return pl.ds(pl.multiple_of(r * NSL, NSL), NSL)


def _pack_pair(lo, hi):
    def bf16_bits(v):
        b = lax.bitcast_convert_type(v, U32)
        return (b + jnp.uint32(0x7FFF) + ((b >> 16) & jnp.uint32(1))) >> 16
    return (bf16_bits(hi) << 16) | bf16_bits(lo)


def _unpack_pair(w):
    lo = lax.bitcast_convert_type(w << 16, F32)
    hi = lax.bitcast_convert_type(w & jnp.uint32(0xFFFF0000), F32)
    return lo, hi


def _store_packed_slabs(ref, val, row0=0):
    m = val.shape[0]
    for s in range(NSL):
        lo = val[:, s * LANES:(s + 1) * LANES]
        hi = val[:, HALF_D + s * LANES:HALF_D + (s + 1) * LANES]
        ref[pl.ds(row0 + s, m, stride=NSL), :] = _pack_pair(lo, hi)


def _dispatch_kernel(slot_ref, tail_ref, x_ref, o_ref, zero_ref, sem):
    tm = x_ref.shape[0] // NSL
    base = pl.program_id(0) * tm

    @pl.when(pl.program_id(0) == 0)
    def _():
        zero_ref[...] = jnp.zeros_like(zero_ref)
        n_half = o_ref.shape[0] // (MOE_HALF * NSL)

        def fill_copy(hb):
            start = pl.multiple_of(hb * (MOE_HALF * NSL), MOE_HALF * NSL)
            return pltpu.make_async_copy(zero_ref, o_ref.at[pl.ds(start, MOE_HALF * NSL)], sem)

        def start_fill(hb, carry):
            @pl.when(tail_ref[hb] > 0)
            def _():
                fill_copy(hb).start()
            return carry

        def wait_fill(hb, carry):
            @pl.when(tail_ref[hb] > 0)
            def _():
                fill_copy(hb).wait()
            return carry

        lax.fori_loop(0, n_half, start_fill, 0)
        lax.fori_loop(0, n_half, wait_fill, 0)

    def row_copy(t, s):
        return pltpu.make_async_copy(x_ref.at[_slab_rows(t)], o_ref.at[_slab_rows(s)], sem)

    def issue(t, carry):
        for k in range(TOP_K):
            row_copy(t, slot_ref[(base + t) * TOP_K + k]).start()
        return carry

    lax.fori_loop(0, tm, issue, 0)

    def drain(t, carry):
        for k in range(TOP_K):
            row_copy(t, slot_ref[(base + t) * TOP_K + k]).wait()
        return carry

    lax.fori_loop(0, tm, drain, 0)


def _dispatch(slots, tails, x_slab, n_slots):
    L = x_slab.shape[0] // NSL
    tm = min(256, L)
    grid_spec = pltpu.PrefetchScalarGridSpec(
        num_scalar_prefetch=2,
        grid=(L // tm,),
        in_specs=[pl.BlockSpec((tm * NSL, LANES), lambda i, sl, tl: (i, 0))],
        out_specs=pl.BlockSpec(memory_space=pl.ANY),
        scratch_shapes=[pltpu.VMEM((MOE_HALF * NSL, LANES), x_slab.dtype), pltpu.SemaphoreType.DMA(())],
    )
    return pl.pallas_call(
        _dispatch_kernel,
        grid_spec=grid_spec,
        out_shape=jax.ShapeDtypeStruct((n_slots * NSL, LANES), x_slab.dtype),
        compiler_params=_cparams(("arbitrary",)),
        name="dispatch",
    )(slots, tails, x_slab)


def _expert_kernel(be_ref, bc_ref, x_ref, wg_ref, wu_ref, bg_ref, bu_ref, wd_ref, bd_ref, o_ref,
                   xb_ref, acc_ref):
    i = pl.program_id(0)
    j = pl.program_id(1)
    nj = pl.num_programs(1)
    hm = MOE_HALF
    count = bc_ref[i]

    for half in range(MOE_BM // hm):
        rows = slice(half * hm, (half + 1) * hm)
        slab_rows = slice(half * hm * NSL, (half + 1) * hm * NSL)
        live = count > half * hm

        @pl.when(live & (j == 0))
        def _():
            for s in range(NSL):
                lo, hi = _unpack_pair(x_ref[pl.ds(half * hm * NSL + s, hm, stride=NSL), :])
                xb_ref[rows, s * LANES:(s + 1) * LANES] = lo.astype(BF16)
                xb_ref[rows, HALF_D + s * LANES:HALF_D + (s + 1) * LANES] = hi.astype(BF16)
            acc_ref[rows, :] = jnp.zeros((hm, acc_ref.shape[1]), F32)

        @pl.when(live)
        def _():
            xb = xb_ref[rows, :]
            gate = jnp.dot(xb, wg_ref[0].astype(BF16), preferred_element_type=F32) + bg_ref[0]
            up = jnp.dot(xb, wu_ref[0].astype(BF16), preferred_element_type=F32) + bu_ref[0]
            gate = jnp.minimum(gate, SWIGLU_LIMIT)
            up = jnp.clip(up, -SWIGLU_LIMIT, SWIGLU_LIMIT)
            glu = gate * _sigmoid(SWIGLU_ALPHA * gate)
            act = ((up + 1.0) * glu).astype(BF16)
            acc_ref[rows, :] += jnp.dot(act, wd_ref[0].astype(BF16), preferred_element_type=F32)

        @pl.when(live & (j == nj - 1))
        def _():
            _store_packed_slabs(o_ref, acc_ref[rows, :] + bd_ref[0], row0=half * hm * NSL)

        @pl.when(jnp.logical_not(live) & (j == nj - 1))
        def _():
            o_ref[slab_rows, :] = jnp.zeros((hm * NSL, LANES), o_ref.dtype)


def _experts(blk_e, blk_count, xs_slab, w_gu, b_gu, w_dn, b_dn):
    n_slots = xs_slab.shape[0] // NSL
    ne, d, two_ff = w_gu.shape
    dff = two_ff // 2
    bm, tf = MOE_BM, MOE_TF
    nj = dff // tf
    nb = n_slots // bm

    def jeff(j, bv, i):
        return jnp.where(bv[i] > 0, j, nj - 1)

    grid_spec = pltpu.PrefetchScalarGridSpec(
        num_scalar_prefetch=2,
        grid=(nb, nj),
        in_specs=[pl.BlockSpec((bm * NSL, LANES), lambda i, j, be, bv: (jnp.where(bv[i] > 0, i, 0), 0)),
                  pl.BlockSpec((1, d, tf), lambda i, j, be, bv: (be[i], 0, jeff(j, bv, i))),
                  pl.BlockSpec((1, d, tf), lambda i, j, be, bv: (be[i], 0, nj + jeff(j, bv, i))),
                  pl.BlockSpec((1, 1, tf), lambda i, j, be, bv: (be[i], 0, jeff(j, bv, i))),
                  pl.BlockSpec((1, 1, tf), lambda i, j, be, bv: (be[i], 0, nj + jeff(j, bv, i))),
                  pl.BlockSpec((1, tf, d), lambda i, j, be, bv: (be[i], jeff(j, bv, i), 0)),
                  pl.BlockSpec((1, 1, d), lambda i, j, be, bv: (be[i], 0, 0))],
        out_specs=pl.BlockSpec((bm * NSL, LANES), lambda i, j, be, bv: (i, 0)),
        scratch_shapes=[pltpu.VMEM((bm, d), BF16), pltpu.VMEM((bm, d), F32)],
    )
    return pl.pallas_call(
        _expert_kernel,
        grid_spec=grid_spec,
        out_shape=jax.ShapeDtypeStruct((n_slots * NSL, LANES), U32),
        compiler_params=_cparams(("arbitrary", "arbitrary")),
        name="experts",
    )(blk_e, blk_count, xs_slab, w_gu, w_gu, b_gu.reshape(ne, 1, two_ff), b_gu.reshape(ne, 1, two_ff),
      w_dn, b_dn.reshape(ne, 1, d))


def _combine_kernel(slot_ref, h_ref, p_ref, g2_ref, fw_ref, yb_ref, o_ref, g0_ref, g1_ref, g2b_ref, g3_ref, sem):
    gbufs = (g0_ref, g1_ref, g2b_ref, g3_ref)
    tm = h_ref.shape[0]
    base = pl.program_id(0) * tm

    def row_copy(t, k, s):
        return pltpu.make_async_copy(yb_ref.at[_slab_rows(s)], gbufs[k].at[_slab_rows(t)], sem)

    def issue(t, carry):
        for k in range(TOP_K):
            row_copy(t, k, slot_ref[(base + t) * TOP_K + k]).start()
        return carry

    lax.fori_loop(0, tm, issue, 0)

    def drain(t, carry):
        for k in range(TOP_K):
            row_copy(t, k, slot_ref[(base + t) * TOP_K + k]).wait()
        return carry

    lax.fori_loop(0, tm, drain, 0)

    p = p_ref[...]
    ssq = jnp.zeros((tm, 1), F32)
    for s in range(NSL):
        moe_lo = jnp.zeros((tm, LANES), F32)
        moe_hi = jnp.zeros((tm, LANES), F32)
        for k in range(TOP_K):
            lo, hi = _unpack_pair(gbufs[k][pl.ds(s, tm, stride=NSL), :])
            moe_lo = moe_lo + p[:, k:k + 1] * lo
            moe_hi = moe_hi + p[:, k:k + 1] * hi
        for moe, c0 in ((moe_lo, s * LANES), (moe_hi, HALF_D + s * LANES)):
            cs = slice(c0, c0 + LANES)
            hs = h_ref[:, cs] + g2_ref[:, cs] * moe
            o_ref[:, cs] = hs
            ssq = ssq + jnp.sum(hs * hs, axis=-1, keepdims=True)
    scale = lax.rsqrt(ssq / h_ref.shape[1] + EPS)
    o_ref[...] = o_ref[...] * scale * fw_ref[...]


def _combine(slots, h, p, g2, fw, yb_slab):
    L, d = h.shape
    tm = min(256, L)
    grid_spec = pltpu.PrefetchScalarGridSpec(
        num_scalar_prefetch=1,
        grid=(L // tm,),
        in_specs=[pl.BlockSpec((tm, d), lambda i, sl: (i, 0)),
                  pl.BlockSpec((tm, LANES), lambda i, sl: (i, 0)),
                  pl.BlockSpec((1, d), lambda i, sl: (0, 0)),
                  pl.BlockSpec((1, d), lambda i, sl: (0, 0)),
                  pl.BlockSpec(memory_space=pl.ANY)],
        out_specs=pl.BlockSpec((tm, d), lambda i, sl: (i, 0)),
        scratch_shapes=[pltpu.VMEM((tm * NSL, LANES), U32) for _ in range(TOP_K)] + [pltpu.SemaphoreType.DMA(())],
    )
    return pl.pallas_call(
        _combine_kernel,
        grid_spec=grid_spec,
        out_shape=jax.ShapeDtypeStruct((L, d), F32),
        compiler_params=_cparams(("arbitrary",)),
        name="combine",
    )(slots, h, p, g2, fw, yb_slab)


def _mixer_scans(x2, nw1, sh1, sc1, w_main, w_dt, conv_w, conv_b, ssm_vecs, rep, lb, row_len, ssd_init, hgrn_init):
    proj, dtr = _inproj(x2, nw1, sh1, sc1, w_main, w_dt)
    xs, bc = _conv_silu(proj, conv_w, conv_b, row_len)
    dtb, alog, amask = ssm_vecs
    ssd, hgrn = [], []
    for d in (0, 1):
        ssd.append(_ssd_scan(xs, bc, dtr, dtb, alog, amask[d], rep[d], ssd_init[d], rev=bool(d)))
        hgrn.append(_hgrn_scan(proj, lb[d], hgrn_init[d], rev=bool(d)))
    return proj, xs, ssd, hgrn


def kernel(x, c, ctx, c_ctx, w_ada, b_ada, norm1_w, norm2_w, w_in, conv_w, conv_b, dt_bias, a_log, d_skip,
           ssm_norm_w, hgrn_lb, hgrn_norm_w, w_branch_ssm, w_branch_hgrn, w_out, w_router, b_router,
           w_gate_up, b_gate_up, w_down, b_down, final_norm_w):
    bsz, seq, d = x.shape
    assert bsz == 1 and w_ada.shape[0] == 1
    ctx_len = ctx.shape[1]
    H = SSM_HEADS

    w = w_in[0]
    o_z = SSM_INNER
    o_xbc = o_z + SSM_INNER + 2 * SSM_GROUPS * SSM_STATE
    o_dt = o_xbc + 2 * H
    w_main = jnp.concatenate([w[:, :o_z], w[:, o_dt:], w[:, o_z:o_xbc]], axis=1).astype(BF16)
    w_dt = jnp.pad(w[:, o_xbc:o_dt], ((0, 0), (0, LANES - 2 * H))).astype(BF16)
    dtb = jnp.pad(dt_bias[0].reshape(1, 2 * H), ((0, 0), (0, LANES - 2 * H)))
    alog = jnp.pad(a_log[0].reshape(1, 2 * H), ((0, 0), (0, LANES - 2 * H)))
    lane = jnp.arange(LANES)
    amask = [((lane >= dd * H) & (lane < (dd + 1) * H)).astype(F32).reshape(1, LANES) for dd in (0, 1)]
    chan_head = jnp.arange(SSM_INNER) // SSM_HEAD_DIM
    rep = [(lane[:, None] == (chan_head[None, :] + dd * H)).astype(BF16) for dd in (0, 1)]
    dsk = jnp.repeat(d_skip[0], SSM_HEAD_DIM).reshape(1, SSM_INNER)
    lb = [hgrn_lb[dd] for dd in (0, 1)]
    wbs = w_branch_ssm[0].astype(BF16)
    wbh = w_branch_hgrn[0].astype(BF16)
    wo = w_out[0].astype(BF16)
    wr = jnp.pad(w_router[0], ((0, 0), (0, LANES - N_EXPERTS)))
    br = jnp.pad(b_router[0].reshape(1, N_EXPERTS), ((0, 0), (0, LANES - N_EXPERTS)))
    row = lambda v: v.reshape(1, -1)

    cc = jnp.zeros((8, d), F32).at[0].set(c[0]).at[1].set(c_ctx)
    mod = _modulation(cc, w_ada[0], b_ada[0])
    sh1, sc1, g1, sh2, sc2, g2 = [mod[0:1, k * d:(k + 1) * d] for k in range(6)]
    csh1, csc1 = mod[1:2, 0:d], mod[1:2, d:2 * d]

    ssm_vecs = (dtb, alog, amask)
    zs = jnp.zeros((SSM_GROUPS, SSM_STATE, SSM_GROUP_W), F32)
    zh = jnp.zeros((HGRN_HEADS, HGRN_V, HGRN_K), F32)
    nw1 = row(norm1_w[0])

    _, _, ssd_c, hgrn_c = _mixer_scans(ctx[0], nw1, csh1, csc1, w_main, w_dt, conv_w[0], conv_b[0], ssm_vecs,
                                       rep, lb, ctx_len, (zs, zs), (zh, zh))
    ssd_init = (ssd_c[0][1], ssd_c[1][1])
    hgrn_init = (hgrn_c[0][1], hgrn_c[1][1])

    x2 = x[0]
    proj, xs, ssd_l, hgrn_l = _mixer_scans(x2, nw1, sh1, sc1, w_main, w_dt, conv_w[0], conv_b[0], ssm_vecs,
                                           rep, lb, GRID_W, ssd_init, hgrn_init)
    y_ssm = _ssd_post(ssd_l[0][0], ssd_l[1][0], xs, proj, dsk, row(ssm_norm_w[0]))
    y_hgrn = _hgrn_post(hgrn_l[0][0], hgrn_l[1][0], proj, row(hgrn_norm_w[0]))
    y = _merge(y_ssm, y_hgrn, proj, wbs, wbh)
    h, hn2, logits = _outproj(y, x2, wo, g1, row(norm2_w[0]), sh2, sc2, wr, br)

    idx, pos, p, counts = _route(logits)
    counts = counts[0, :N_EXPERTS].astype(I32)
    bm = MOE_BM
    n_assign = seq * TOP_K
    n_blocks = -(-(n_assign + N_EXPERTS * (bm - 1)) // bm)
    n_slots = n_blocks * bm
    padded = (counts + bm - 1) // bm * bm
    pad_end = jnp.cumsum(padded)
    pad_start = pad_end - padded
    slots = (pad_start[idx[:, :TOP_K]] + pos[:, :TOP_K]).reshape(-1).astype(I32)
    blk_start = jnp.arange(n_blocks, dtype=I32) * bm
    blk_e = jnp.minimum(jnp.sum(blk_start[:, None] >= pad_end[None, :], axis=1), N_EXPERTS - 1).astype(I32)
    blk_count = jnp.clip(pad_start[blk_e] + counts[blk_e] - blk_start, 0, bm).astype(I32)
    half_start = jnp.arange(n_slots // MOE_HALF, dtype=I32) * MOE_HALF
    half_e = jnp.repeat(blk_e, bm // MOE_HALF)
    tails = (pad_start[half_e] + counts[half_e] - half_start < MOE_HALF).astype(I32)

    assert d == 2 * HALF_D
    xs_slab = _dispatch(slots, tails, hn2, n_slots)
    yb_slab = _experts(blk_e, blk_count, xs_slab, w_gate_up[0], b_gate_up[0], w_down[0], b_down[0])
    out = _combine(slots, h, p, g2, row(final_norm_w), yb_slab)
    return out.reshape(bsz, seq, d)
```

```python
import functools

import jax
import jax.numpy as jnp
from jax import lax
from jax.experimental import pallas as pl
from jax.experimental.pallas import tpu as pltpu

F32 = jnp.float32
BF16 = jnp.bfloat16
I32 = jnp.int32
U32 = jnp.uint32
HI = lax.Precision.HIGHEST

EPS = 1e-6
GRID_W = 64

LANES = 128
NSL = 8
HALF_D = NSL * LANES
SSM_HEADS = 32
SSM_HEAD_DIM = 64
SSM_GROUPS = 4
SSM_STATE = 128
SSM_INNER = SSM_HEADS * SSM_HEAD_DIM
SSM_GROUP_W = SSM_INNER // SSM_GROUPS
SSM_CHUNK = 128
HGRN_HEADS = 16
HGRN_K = 128
HGRN_V = 128
HGRN_CHUNK = 64
HGRN_SUB = 16
HGRN_HEAD_GROUP = 4
HGRN_SAFE_DECAY = 60.0
N_EXPERTS = 32
TOP_K = 4
SWIGLU_LIMIT = 7.0
SWIGLU_ALPHA = 1.702
MOE_BM = 1024
MOE_HALF = 512
MOE_TF = 512
VMEM_LIMIT = 56 * 1024 * 1024

COL_Z, COL_Q, COL_F, COL_I, COL_G, COL_GS, COL_GH, COL_XS = 0, 1, 2, 4, 5, 6, 7, 8


def _sigmoid(x):
    return 1.0 / (1.0 + jnp.exp(-x))


def _cparams(sem):
    return pltpu.CompilerParams(dimension_semantics=sem, vmem_limit_bytes=VMEM_LIMIT)


def _mod_kernel(c_ref, w_ref, b_ref, o_ref):
    c = c_ref[...]
    s = c * _sigmoid(c)
    o_ref[...] = jnp.dot(s, w_ref[...], precision=HI, preferred_element_type=F32) + b_ref[...]


def _modulation(cc, w_ada, b_ada):
    d, n = w_ada.shape
    tn = 1024
    return pl.pallas_call(
        _mod_kernel,
        grid=(n // tn,),
        in_specs=[pl.BlockSpec((8, d), lambda j: (0, 0)),
                  pl.BlockSpec((d, tn), lambda j: (0, j)),
                  pl.BlockSpec((1, tn), lambda j: (0, j))],
        out_specs=pl.BlockSpec((8, tn), lambda j: (0, j)),
        out_shape=jax.ShapeDtypeStruct((8, n), F32),
        compiler_params=_cparams(("arbitrary",)),
        name="adaln_mod",
    )(cc, w_ada, b_ada.reshape(1, n))


def _inproj_kernel(x_ref, nw_ref, sh_ref, sc_ref, w_ref, wdt_ref, o_ref, odt_ref, hn_ref):
    @pl.when(pl.program_id(1) == 0)
    def _():
        x = x_ref[...]
        ms = jnp.mean(x * x, axis=-1, keepdims=True)
        y = x * lax.rsqrt(ms + EPS) * nw_ref[...]
        hn = (y * (1.0 + sc_ref[...]) + sh_ref[...]).astype(BF16)
        hn_ref[...] = hn
        odt_ref[...] = jnp.dot(hn, wdt_ref[...], preferred_element_type=F32)

    o_ref[...] = jnp.dot(hn_ref[...], w_ref[...], preferred_element_type=F32)


def _inproj(x2, nw, sh, sc, w_main, w_dt):
    L, d = x2.shape
    n = w_main.shape[1]
    tm = min(1024, L)
    tn = 1024
    return pl.pallas_call(
        _inproj_kernel,
        grid=(L // tm, n // tn),
        in_specs=[pl.BlockSpec((tm, d), lambda i, j: (i, 0)),
                  pl.BlockSpec((1, d), lambda i, j: (0, 0)),
                  pl.BlockSpec((1, d), lambda i, j: (0, 0)),
                  pl.BlockSpec((1, d), lambda i, j: (0, 0)),
                  pl.BlockSpec((d, tn), lambda i, j: (0, j)),
                  pl.BlockSpec((d, LANES), lambda i, j: (0, 0))],
        out_specs=[pl.BlockSpec((tm, tn), lambda i, j: (i, j)),
                   pl.BlockSpec((tm, LANES), lambda i, j: (i, 0))],
        out_shape=[jax.ShapeDtypeStruct((L, n), F32), jax.ShapeDtypeStruct((L, LANES), F32)],
        scratch_shapes=[pltpu.VMEM((tm, d), BF16)],
        compiler_params=_cparams(("parallel", "arbitrary")),
        name="inproj",
    )(x2, nw, sh, sc, w_main, w_dt)


def _conv_body(u, w_ref, b_ref, row_len):
    tb = u.shape[0]
    pos = lax.broadcasted_iota(I32, u.shape, 0) & (row_len - 1)
    acc = u * w_ref[2:3, :] + b_ref[...]
    for k, off in ((0, -2), (1, -1), (3, 1)):
        shifted = pltpu.roll(u, shift=(-off) % tb, axis=0)
        valid = (pos + off >= 0) & (pos + off < row_len)
        acc = acc + jnp.where(valid, shifted, 0.0) * w_ref[k:k + 1, :]
    return acc * _sigmoid(acc)


def _conv_kernel(ux_ref, ub_ref, wx_ref, wb_ref, bx_ref, bb_ref, ox_ref, ob_ref, *, row_len):
    ox_ref[...] = _conv_body(ux_ref[...], wx_ref, bx_ref, row_len)
    ob_ref[...] = _conv_body(ub_ref[...], wb_ref, bb_ref, row_len)


def _conv_silu(proj, conv_w, conv_b, row_len):
    L = proj.shape[0]
    d = SSM_INNER
    nbc = 2 * SSM_GROUPS * SSM_STATE
    tb = max(row_len, min(512, L))
    wx, wb = conv_w[:, :d], conv_w[:, d:]
    bx, bb = conv_b[:d].reshape(1, d), conv_b[d:].reshape(1, nbc)
    return pl.pallas_call(
        functools.partial(_conv_kernel, row_len=row_len),
        grid=(L // tb,),
        in_specs=[pl.BlockSpec((tb, d), lambda i: (i, COL_XS)),
                  pl.BlockSpec((tb, nbc), lambda i: (i, COL_XS * 2 + 2)),
                  pl.BlockSpec((4, d), lambda i: (0, 0)),
                  pl.BlockSpec((4, nbc), lambda i: (0, 0)),
                  pl.BlockSpec((1, d), lambda i: (0, 0)),
                  pl.BlockSpec((1, nbc), lambda i: (0, 0))],
        out_specs=[pl.BlockSpec((tb, d), lambda i: (i, 0)),
                   pl.BlockSpec((tb, nbc), lambda i: (i, 0))],
        out_shape=[jax.ShapeDtypeStruct((L, d), F32), jax.ShapeDtypeStruct((L, nbc), F32)],
        compiler_params=_cparams(("parallel",)),
        name="conv_silu",
    )(proj, proj, wx, wb, bx, bb)


def _ssd_kernel(xs_ref, bc_ref, dt_ref, dtb_ref, alog_ref, amask_ref, rep_ref, s0_ref,
                y_ref, sfin_ref, S_ref, *, rev, lane0):
    n = pl.program_id(0)

    @pl.when(n == 0)
    def _():
        S_ref[...] = s0_ref[...]

    Q = xs_ref.shape[0]
    N = SSM_STATE
    GW = SSM_GROUP_W
    ti = lax.broadcasted_iota(I32, (Q, Q), 0)
    si = lax.broadcasted_iota(I32, (Q, Q), 1)
    allowed = (si >= ti) if rev else (si <= ti)
    tri = allowed.astype(F32)

    raw = dt_ref[...] + dtb_ref[...]
    dt = jnp.maximum(raw, 0.0) + jnp.log1p(jnp.exp(-jnp.abs(raw)))
    loga = dt * (-jnp.exp(alog_ref[...]) * amask_ref[...])
    cs = jnp.dot(tri, loga, precision=HI, preferred_element_type=F32)
    csT = cs.T
    tot = cs[0:1, :] if rev else cs[Q - 1:Q, :]
    fac = jnp.concatenate([jnp.exp(cs), jnp.exp(tot - cs), dt], axis=0).astype(BF16)
    tot8 = jnp.broadcast_to(tot, (8, LANES))
    t_hi = tot8.astype(BF16)
    t_r = tot8 - t_hi.astype(F32)
    t_mid = t_r.astype(BF16)
    t_lo = (t_r - t_mid.astype(F32)).astype(BF16)
    tot3 = jnp.concatenate([t_hi, t_mid, t_lo], axis=0)
    lane = lax.broadcasted_iota(I32, (Q, LANES), 1)
    lo_mask = lane < SSM_HEAD_DIM

    for g in range(SSM_GROUPS):
        rep_g = rep_ref[:, g * GW:(g + 1) * GW]
        fx = jnp.dot(fac, rep_g, preferred_element_type=F32)
        e_in, e_out, dt_x = fx[0:Q], fx[Q:2 * Q], fx[2 * Q:3 * Q]
        tx = jnp.dot(tot3, rep_g, preferred_element_type=F32)
        dec = jnp.exp((tx[0:1] + tx[8:9]) + tx[16:17])
        xdt = xs_ref[:, g * GW:(g + 1) * GW] * dt_x
        xw = (xdt * e_out).astype(BF16)
        xdt_b = xdt.astype(BF16)
        Bg = bc_ref[:, g * N:(g + 1) * N].astype(BF16)
        Cg = bc_ref[:, (SSM_GROUPS + g) * N:(SSM_GROUPS + g + 1) * N].astype(BF16)
        cb = lax.dot_general(Cg, Bg, (((1,), (1,)), ((), ())), preferred_element_type=F32)
        Sg = S_ref[g]
        y_off = jnp.dot(Cg, Sg.astype(BF16), preferred_element_type=F32) * e_in
        upd = lax.dot_general(Bg, xw, (((0,), (0,)), ((), ())), preferred_element_type=F32)
        S_ref[g] = Sg * dec + upd
        for jp in range(GW // LANES):
            h0 = g * (GW // SSM_HEAD_DIM) + 2 * jp
            ms = []
            for hh in (h0, h0 + 1):
                col = cs[:, lane0 + hh:lane0 + hh + 1]
                row = csT[lane0 + hh:lane0 + hh + 1, :]
                dec_ts = jnp.exp(jnp.where(allowed, col - row, -jnp.inf))
                ms.append((cb * dec_ts).astype(BF16))
            m2 = jnp.concatenate(ms, axis=1)
            x2 = xdt_b[:, jp * LANES:(jp + 1) * LANES]
            r2 = jnp.concatenate([jnp.where(lo_mask, x2, jnp.zeros_like(x2)),
                                  jnp.where(lo_mask, jnp.zeros_like(x2), x2)], axis=0)
            y_d = jnp.dot(m2, r2, preferred_element_type=F32)
            c0 = h0 * SSM_HEAD_DIM
            y_ref[:, c0:c0 + LANES] = y_d + y_off[:, jp * LANES:(jp + 1) * LANES]

    @pl.when(n == pl.num_programs(0) - 1)
    def _():
        sfin_ref[...] = S_ref[...]


def _ssd_scan(xs, bc, dt, dtb, alog, amask, rep, s0, rev):
    L, d = xs.shape
    Q = SSM_CHUNK
    nc = L // Q
    idx = (lambda n: (nc - 1 - n, 0)) if rev else (lambda n: (n, 0))
    const2 = lambda n: (0, 0)
    const3 = lambda n: (0, 0, 0)
    sshape = (SSM_GROUPS, SSM_STATE, SSM_GROUP_W)
    return pl.pallas_call(
        functools.partial(_ssd_kernel, rev=rev, lane0=SSM_HEADS if rev else 0),
        grid=(nc,),
        in_specs=[pl.BlockSpec((Q, d), idx),
                  pl.BlockSpec((Q, bc.shape[1]), idx),
                  pl.BlockSpec((Q, LANES), idx),
                  pl.BlockSpec((1, LANES), const2),
                  pl.BlockSpec((1, LANES), const2),
                  pl.BlockSpec((1, LANES), const2),
                  pl.BlockSpec((LANES, d), const2),
                  pl.BlockSpec(sshape, const3)],
        out_specs=[pl.BlockSpec((Q, d), idx),
                   pl.BlockSpec(sshape, const3)],
        out_shape=[jax.ShapeDtypeStruct((L, d), F32), jax.ShapeDtypeStruct(sshape, F32)],
        scratch_shapes=[pltpu.VMEM(sshape, F32)],
        compiler_params=_cparams(("arbitrary",)),
        name="ssd_bwd" if rev else "ssd_fwd",
    )(xs, bc, dt, dtb, alog, amask, rep, s0)


def _ssd_post_kernel(yf_ref, yb_ref, xs_ref, z_ref, dsk_ref, nw_ref, o_ref):
    z = z_ref[...]
    y = (yf_ref[...] + yb_ref[...] + dsk_ref[...] * xs_ref[...]) * (z * _sigmoid(z))
    for g in range(SSM_GROUPS):
        sl = slice(g * SSM_GROUP_W, (g + 1) * SSM_GROUP_W)
        yg = y[:, sl]
        ms = jnp.mean(yg * yg, axis=-1, keepdims=True)
        o_ref[:, sl] = (yg * lax.rsqrt(ms + EPS) * nw_ref[:, sl]).astype(o_ref.dtype)


def _ssd_post(yf, yb, xs, proj, dsk, nw):
    L, d = yf.shape
    tm = min(512, L)
    row = lambda i: (i, 0)
    const = lambda i: (0, 0)
    return pl.pallas_call(
        _ssd_post_kernel,
        grid=(L // tm,),
        in_specs=[pl.BlockSpec((tm, d), row), pl.BlockSpec((tm, d), row), pl.BlockSpec((tm, d), row),
                  pl.BlockSpec((tm, d), lambda i: (i, COL_Z)),
                  pl.BlockSpec((1, d), const), pl.BlockSpec((1, d), const)],
        out_specs=pl.BlockSpec((tm, d), row),
        out_shape=jax.ShapeDtypeStruct((L, d), BF16),
        compiler_params=_cparams(("parallel",)),
        name="ssd_post",
    )(yf, yb, xs, proj, dsk, nw)


class _HgrnDir:
    def __init__(self, refs, rev):
        self.refs = refs
        self.rev = rev

    def build(self):
        (q_ref, f_ref, i_ref, lb_ref, o_ref, ST_ref, kk_ref, bcs_ref, lf_ref, v_ref, vT_ref), rev = self.refs, self.rev
        _hgrn_dir_body(self, q_ref, f_ref, i_ref, lb_ref, o_ref, ST_ref, kk_ref, bcs_ref, lf_ref, v_ref, vT_ref, rev)
        return self


def _hgrn_dir_body(self, q_ref, f_ref, i_ref, lb_ref, o_ref, ST_ref, kk_ref, bcs_ref, lf_ref, v_ref, vT_ref, rev):
    C = q_ref.shape[0]
    c = HGRN_SUB
    nsub = C // c
    K = HGRN_K
    d = HGRN_HEADS * K
    ti = lax.broadcasted_iota(I32, (C, C), 0)
    si = lax.broadcasted_iota(I32, (C, C), 1)
    allowed = (si >= ti) if rev else (si <= ti)
    bt, bs = ti // c, si // c
    dist = (bs - bt) if rev else (bt - bs)
    order = list(range(nsub - 1, -1, -1)) if rev else list(range(nsub))

    def gate(h):
        sl = slice(h * K, (h + 1) * K)
        lbr = lb_ref[:, sl]
        lmax = jnp.max(lbr, axis=0, keepdims=True)
        le = jnp.exp(lbr - lmax)
        lb = le[0:1, :] / jnp.sum(le, axis=0, keepdims=True)
        fr = f_ref[:, sl]
        e = jnp.exp(-jnp.abs(fr))
        rinv = 1.0 / (1.0 + e)
        pos = fr >= 0.0
        sg = jnp.where(pos, rinv, e * rinv)
        sgn = jnp.where(pos, e * rinv, rinv)
        logf = jnp.log(lb + (1.0 - lb) * sg)
        kk_ref[:, sl] = (1.0 - lb) * sgn
        iv = i_ref[:, sl]
        vv = iv * _sigmoid(iv)
        v_ref[:, sl] = vv
        vT_ref[sl, :] = vv.T.astype(BF16)
        hi = logf.astype(BF16)
        r1 = logf - hi.astype(F32)
        mid = r1.astype(BF16)
        lo = (r1 - mid.astype(F32)).astype(BF16)
        lf_ref[:, h * K:(h + 1) * K] = hi
        lf_ref[:, d + h * K:d + (h + 1) * K] = mid
        lf_ref[:, 2 * d + h * K:2 * d + (h + 1) * K] = lo

    def levels(col):
        r_in, r_out = {}, {}
        for p, sc_i in enumerate(order):
            last = sc_i * c + (0 if rev else c - 1)
            r_out[p] = bcs_ref[last:last + 1, col]
            r_in[p] = r_out[p - 1] if p > 0 else jnp.zeros_like(r_out[p])
        return r_in, r_out

    def cumulate():
        cs3 = jnp.dot(allowed.astype(BF16), lf_ref[...], preferred_element_type=F32)
        bcs_ref[...] = (cs3[:, :d] + cs3[:, d:2 * d]) + cs3[:, 2 * d:]
        r_in_all, r_out_all = levels(slice(0, d))
        worst = r_in_all[0] - r_out_all[0]
        for p in range(1, nsub):
            worst = jnp.maximum(worst, r_in_all[p] - r_out_all[p])
        return worst

    def stages(exact_pairwise):
        t16 = lax.broadcasted_iota(I32, (c, K), 0)
        ones_b = jnp.ones((K, K), BF16)
        t2 = lax.broadcasted_iota(I32, (C, 2 * C), 0)
        col2 = lax.broadcasted_iota(I32, (C, 2 * C), 1)
        s2 = col2 & (C - 1)
        dist2 = (s2 // c - t2 // c) if rev else (t2 // c - s2 // c)
        dist2 = jnp.where(col2 < C, dist2, -1)
        same_blk = (col2 >= C) & (t2 // c == s2 // c) & ((s2 >= t2) if rev else (s2 <= t2))
        band = {dd: dist2 == dd for dd in range(1, nsub)}
        def matmuls(h):
            sl = slice(h * K, (h + 1) * K)
            bcs = bcs_ref[:, sl]
            kk = kk_ref[:, sl]
            qq = q_ref[:, sl] * (K ** -0.5)
            tot = bcs[0:1, :] if rev else bcs[C - 1:C, :]

            r_in, r_out = levels(sl)

            def by_sub_chunk(fn):
                rows = [None] * nsub
                for p, sc_i in enumerate(order):
                    rows[sc_i] = jnp.broadcast_to(fn(p), (c, K))
                return jnp.concatenate(rows, axis=0)

            R_in = by_sub_chunk(lambda p: r_in[p])
            R_out = by_sub_chunk(lambda p: r_out[p])
            qh = qq * jnp.exp(bcs - R_in)
            kt = kk * jnp.exp(R_out - bcs)
            ST = ST_ref[h]
            if exact_pairwise:
                q_in = (qq * jnp.exp(bcs)).astype(BF16)
                k_end = (kk * jnp.exp(tot - bcs)).astype(BF16)
            else:
                q_in = (qh * by_sub_chunk(lambda p: jnp.exp(r_in[p]))).astype(BF16)
                k_end = (kt * by_sub_chunk(lambda p: jnp.exp(tot - r_out[p]))).astype(BF16)
            o = lax.dot_general(q_in, ST.astype(BF16), (((1,), (1,)), ((), ())), preferred_element_type=F32)
            upd = jnp.dot(vT_ref[sl, :], k_end, preferred_element_type=F32)
            new_state = ST * jnp.exp(tot) + upd
            lhs = [qh]
            for dd in range(2, nsub):
                rows_x = [None] * nsub
                for p, sc_i in enumerate(order):
                    if p >= dd:
                        rows_x[sc_i] = jnp.broadcast_to(jnp.exp(r_in[p] - r_out[p - dd]), (c, K))
                    else:
                        rows_x[sc_i] = jnp.zeros((c, K), F32)
                lhs.append(qh * jnp.concatenate(rows_x, axis=0))
            lhs = jnp.concatenate(lhs, axis=0).astype(BF16)
            if exact_pairwise:
                rhs = kt.astype(BF16)
            else:
                k_diag = kt * by_sub_chunk(lambda p: jnp.exp(r_in[p] - r_out[p]))
                rhs = jnp.concatenate([kt, k_diag], axis=0).astype(BF16)
            a = lax.dot_general(lhs, rhs, (((1,), (1,)), ((), ())), preferred_element_type=F32)
            return o, a, new_state

        def finish(h, o, a, new_state):
            sl = slice(h * K, (h + 1) * K)
            ST_ref[h] = new_state
            vv = v_ref[:, sl]
            vb = vv.astype(BF16)
            if exact_pairwise:
                bcs = bcs_ref[:, sl]
                kk = kk_ref[:, sl]
                qq = q_ref[:, sl] * (K ** -0.5)
                att = jnp.where(dist == 1, a[0:C], 0.0)
                for dd in range(2, nsub):
                    att = att + jnp.where(dist == dd, a[(dd - 1) * C:dd * C], 0.0)
                o = o + jnp.dot(att.astype(BF16), vb, preferred_element_type=F32)
                for sc_i in range(nsub):
                    r0 = sc_i * c
                    b_i = bcs[r0:r0 + c, :]
                    q_i = qq[r0:r0 + c, :]
                    ws = []
                    for s in range(c):
                        ok = (t16 <= s) if rev else (t16 >= s)
                        ex = jnp.exp(jnp.where(ok, b_i - bcs[r0 + s:r0 + s + 1, :], -jnp.inf))
                        ws.append((q_i * ex * kk[r0 + s:r0 + s + 1, :]).astype(BF16))
                    w_all = jnp.concatenate(ws, axis=0)
                    a_all = jnp.dot(w_all, ones_b, preferred_element_type=F32)
                    o_d = jnp.zeros((c, HGRN_V), F32)
                    for s in range(c):
                        o_d = o_d + a_all[s * c:(s + 1) * c, :] * vv[r0 + s:r0 + s + 1, :]
                    o_ref[r0:r0 + c, sl] = o[r0:r0 + c, :] + o_d
            else:
                att = jnp.zeros((C, 2 * C), F32)
                for dd in range(nsub - 1, 1, -1):
                    att = jnp.where(band[dd], a[(dd - 1) * C:dd * C], att)
                att = jnp.where(band[1] | same_blk, a[0:C], att)
                v2 = jnp.concatenate([vb, vb], axis=0)
                o_ref[:, sl] = o + jnp.dot(att.astype(BF16), v2, preferred_element_type=F32)

        return matmuls, finish

    self.gate, self.cumulate, self.stages = gate, cumulate, stages


def _hgrn_kernel(qf_ref, ff_ref, if_ref, qb_ref, fb_ref, ib_ref, lbf_ref, lbb_ref, s0f_ref, s0b_ref,
                 of_ref, ob_ref, sff_ref, sfb_ref, *scratch):
    n = pl.program_id(0)
    nscr = len(scratch) // 2
    scr_f, scr_b = scratch[:nscr], scratch[nscr:]

    @pl.when(n == 0)
    def _():
        scr_f[0][...] = s0f_ref[...]
        scr_b[0][...] = s0b_ref[...]

    dirs = [_HgrnDir((qf_ref, ff_ref, if_ref, lbf_ref, of_ref) + tuple(scr_f), False).build(),
            _HgrnDir((qb_ref, fb_ref, ib_ref, lbb_ref, ob_ref) + tuple(scr_b), True).build()]
    for h in range(HGRN_HEADS):
        for dr in dirs:
            dr.gate(h)
    worst = jnp.maximum(dirs[0].cumulate(), dirs[1].cumulate())
    safe = jnp.max(worst) < HGRN_SAFE_DECAY

    def heads(exact_pairwise):
        fns = [dr.stages(exact_pairwise) for dr in dirs]
        group = 1 if exact_pairwise else HGRN_HEAD_GROUP
        for h0 in range(0, HGRN_HEADS, group):
            staged = [(fin, h, mm(h)) for h in range(h0, h0 + group) for mm, fin in fns]
            for fin, h, vals in staged:
                fin(h, *vals)

    @pl.when(safe)
    def _():
        heads(False)

    @pl.when(jnp.logical_not(safe))
    def _():
        heads(True)

    @pl.when(n == pl.num_programs(0) - 1)
    def _():
        sff_ref[...] = scr_f[0][...]
        sfb_ref[...] = scr_b[0][...]


def _hgrn_scan(proj, lb, s0):
    L = proj.shape[0]
    d = HGRN_HEADS * HGRN_K
    C = HGRN_CHUNK
    nc = L // C
    sshape = (HGRN_HEADS, HGRN_V, HGRN_K)
    const3 = lambda n: (0, 0, 0)
    fwd = lambda col: pl.BlockSpec((C, d), lambda n: (n, col))
    bwd = lambda col: pl.BlockSpec((C, d), lambda n: (nc - 1 - n, col))
    lb_spec = pl.BlockSpec(lb[0].shape, lambda n: (0, 0))
    st_spec = pl.BlockSpec(sshape, const3)
    scratch = [pltpu.VMEM(sshape, F32), pltpu.VMEM((C, d), F32), pltpu.VMEM((C, d), F32),
               pltpu.VMEM((C, 3 * d), BF16), pltpu.VMEM((C, d), F32), pltpu.VMEM((d, C), BF16)]
    return pl.pallas_call(
        _hgrn_kernel,
        grid=(nc,),
        in_specs=[fwd(COL_Q), fwd(COL_F), fwd(COL_I), bwd(COL_Q), bwd(COL_F + 1), bwd(COL_I),
                  lb_spec, lb_spec, st_spec, st_spec],
        out_specs=[fwd(0), bwd(0), st_spec, st_spec],
        out_shape=[jax.ShapeDtypeStruct((L, d), F32), jax.ShapeDtypeStruct((L, d), F32),
                   jax.ShapeDtypeStruct(sshape, F32), jax.ShapeDtypeStruct(sshape, F32)],
        scratch_shapes=scratch + scratch,
        compiler_params=_cparams(("arbitrary",)),
        name="hgrn",
    )(proj, proj, proj, proj, proj, proj, lb[0], lb[1], s0[0], s0[1])


def _hgrn_post_kernel(of_ref, ob_ref, g_ref, nw_ref, o_ref):
    nw = nw_ref[...]
    for h in range(HGRN_HEADS):
        sl = slice(h * HGRN_V, (h + 1) * HGRN_V)
        o = of_ref[:, sl] + ob_ref[:, sl]
        ms = jnp.mean(o * o, axis=-1, keepdims=True)
        g = g_ref[:, sl]
        o_ref[:, sl] = (o * lax.rsqrt(ms + EPS) * nw * (g * _sigmoid(g))).astype(o_ref.dtype)


def _hgrn_post(of, ob, proj, nw):
    L, d = of.shape
    tm = min(512, L)
    row = lambda i: (i, 0)
    return pl.pallas_call(
        _hgrn_post_kernel,
        grid=(L // tm,),
        in_specs=[pl.BlockSpec((tm, d), row), pl.BlockSpec((tm, d), row),
                  pl.BlockSpec((tm, d), lambda i: (i, COL_G)),
                  pl.BlockSpec((1, HGRN_V), lambda i: (0, 0))],
        out_specs=pl.BlockSpec((tm, d), row),
        out_shape=jax.ShapeDtypeStruct((L, d), BF16),
        compiler_params=_cparams(("parallel",)),
        name="hgrn_post",
    )(of, ob, proj, nw)


def _merge_kernel(ys_ref, yh_ref, gs_ref, gh_ref, wbs_ref, wbh_ref, o_ref):
    a = jnp.dot(ys_ref[...], wbs_ref[...], preferred_element_type=F32)
    b = jnp.dot(yh_ref[...], wbh_ref[...], preferred_element_type=F32)
    o_ref[...] = (_sigmoid(gs_ref[...]) * a + _sigmoid(gh_ref[...]) * b).astype(o_ref.dtype)


def _merge(ys, yh, proj, wbs, wbh):
    L, d = ys.shape
    tm = min(1024, L)
    tn = 512
    nj = d // tn
    return pl.pallas_call(
        _merge_kernel,
        grid=(L // tm, nj),
        in_specs=[pl.BlockSpec((tm, d), lambda i, j: (i, 0)),
                  pl.BlockSpec((tm, d), lambda i, j: (i, 0)),
                  pl.BlockSpec((tm, tn), lambda i, j: (i, COL_GS * nj + j)),
                  pl.BlockSpec((tm, tn), lambda i, j: (i, COL_GH * nj + j)),
                  pl.BlockSpec((d, tn), lambda i, j: (0, j)),
                  pl.BlockSpec((d, tn), lambda i, j: (0, j))],
        out_specs=pl.BlockSpec((tm, tn), lambda i, j: (i, j)),
        out_shape=jax.ShapeDtypeStruct((L, d), BF16),
        compiler_params=_cparams(("parallel", "arbitrary")),
        name="merge",
    )(ys, yh, proj, proj, wbs, wbh)


def _outproj_kernel(y_ref, x_ref, wo_ref, g1_ref, nw_ref, sh_ref, sc_ref, wr_ref, br_ref,
                    h_ref, hn_ref, lg_ref):
    h = x_ref[...] + g1_ref[...] * jnp.dot(y_ref[...], wo_ref[...], preferred_element_type=F32)
    h_ref[...] = h
    ms = jnp.mean(h * h, axis=-1, keepdims=True)
    hn = h * lax.rsqrt(ms + EPS) * nw_ref[...] * (1.0 + sc_ref[...]) + sh_ref[...]
    h_hi = hn.astype(BF16)
    h_lo = (hn - h_hi.astype(F32)).astype(BF16)
    wr = wr_ref[...]
    w_hi = wr.astype(BF16)
    w_lo = (wr - w_hi.astype(F32)).astype(BF16)
    lg = jnp.dot(h_hi, w_hi, preferred_element_type=F32)
    lg = lg + (jnp.dot(h_hi, w_lo, preferred_element_type=F32) + jnp.dot(h_lo, w_hi, preferred_element_type=F32))
    lg_ref[...] = lg + br_ref[...]
    _store_packed_slabs(hn_ref, hn)


def _outproj(y, x2, wo, g1, nw, sh, sc, wr, br):
    L, d = x2.shape
    tm = min(256, L)
    row = lambda i: (i, 0)
    const = lambda i: (0, 0)
    vec = pl.BlockSpec((1, d), const)
    return pl.pallas_call(
        _outproj_kernel,
        grid=(L // tm,),
        in_specs=[pl.BlockSpec((tm, d), row), pl.BlockSpec((tm, d), row),
                  pl.BlockSpec((d, d), const), vec, vec, vec, vec,
                  pl.BlockSpec((d, LANES), const), pl.BlockSpec((1, LANES), const)],
        out_specs=[pl.BlockSpec((tm, d), row), pl.BlockSpec((tm * NSL, LANES), row), pl.BlockSpec((tm, LANES), row)],
        out_shape=[jax.ShapeDtypeStruct((L, d), F32), jax.ShapeDtypeStruct((L * NSL, LANES), U32),
                   jax.ShapeDtypeStruct((L, LANES), F32)],
        compiler_params=_cparams(("parallel",)),
        name="outproj",
    )(y, x2, wo, g1, nw, sh, sc, wr, br)


def _route_kernel(lg_ref, idx_ref, pos_ref, p_ref, cnt_ref, carry_ref):
    n = pl.program_id(0)

    @pl.when(n == 0)
    def _():
        carry_ref[...] = jnp.zeros_like(carry_ref)

    tm = lg_ref.shape[0]
    lane_i = lax.broadcasted_iota(I32, (tm, LANES), 1)
    lane = lane_i.astype(F32)
    neg = jnp.float32(-jnp.inf)
    l = jnp.where(lane_i < N_EXPERTS, lg_ref[...], neg)
    vals, sels, ids = [], [], []
    for _ in range(TOP_K):
        m = jnp.max(l, axis=-1, keepdims=True)
        i_k = jnp.min(jnp.where(l == m, lane, float(LANES)), axis=-1, keepdims=True)
        sel = lane == i_k
        l = jnp.where(sel, neg, l)
        vals.append(m)
        sels.append(sel)
        ids.append(i_k)
    es = [jnp.exp(v - vals[0]) for v in vals]
    den = es[0]
    for e in es[1:]:
        den = den + e
    onehot = jnp.zeros((tm, LANES), F32)
    for sel in sels:
        onehot = onehot + jnp.where(sel, 1.0, 0.0)
    ti = lax.broadcasted_iota(I32, (tm, tm), 0)
    si = lax.broadcasted_iota(I32, (tm, tm), 1)
    strict = jnp.where(si < ti, 1.0, 0.0).astype(BF16)
    before = jnp.dot(strict, onehot.astype(BF16), preferred_element_type=F32) + carry_ref[...]
    idx_out = jnp.zeros((tm, LANES), F32)
    pos_out = jnp.zeros((tm, LANES), F32)
    p_out = jnp.zeros((tm, LANES), F32)
    for k in range(TOP_K):
        pos_k = jnp.sum(jnp.where(sels[k], before, 0.0), axis=-1, keepdims=True)
        idx_out = jnp.where(lane_i == k, ids[k], idx_out)
        pos_out = jnp.where(lane_i == k, pos_k, pos_out)
        p_out = jnp.where(lane_i == k, es[k] / den, p_out)
    idx_ref[...] = idx_out.astype(I32)
    pos_ref[...] = pos_out.astype(I32)
    p_ref[...] = p_out
    carry_ref[...] = carry_ref[...] + jnp.sum(onehot, axis=0, keepdims=True)
    cnt_ref[...] = carry_ref[...]


def _route(logits):
    L = logits.shape[0]
    tm = min(256, L)
    row = lambda i: (i, 0)
    return pl.pallas_call(
        _route_kernel,
        grid=(L // tm,),
        in_specs=[pl.BlockSpec((tm, LANES), row)],
        out_specs=[pl.BlockSpec((tm, LANES), row), pl.BlockSpec((tm, LANES), row),
                   pl.BlockSpec((tm, LANES), row), pl.BlockSpec((1, LANES), lambda i: (0, 0))],
        out_shape=[jax.ShapeDtypeStruct((L, LANES), I32), jax.ShapeDtypeStruct((L, LANES), I32),
                   jax.ShapeDtypeStruct((L, LANES), F32), jax.ShapeDtypeStruct((1, LANES), F32)],
        scratch_shapes=[pltpu.VMEM((1, LANES), F32)],
        compiler_params=_cparams(("arbitrary",)),
        name="route",
    )(logits)


def _slab_rows(r):
    return pl.ds(pl.multiple_of(r * NSL, NSL), NSL)


def _pack_pair(lo, hi):
    def bf16_bits(v):
        b = lax.bitcast_convert_type(v, U32)
        return (b + jnp.uint32(0x7FFF) + ((b >> 16) & jnp.uint32(1))) >> 16
    return (bf16_bits(hi) << 16) | bf16_bits(lo)


def _unpack_pair(w):
    lo = lax.bitcast_convert_type(w << 16, F32)
    hi = lax.bitcast_convert_type(w & jnp.uint32(0xFFFF0000), F32)
    return lo, hi


def _store_packed_slabs(ref, val, row0=0):
    m = val.shape[0]
    for s in range(NSL):
        lo = val[:, s * LANES:(s + 1) * LANES]
        hi = val[:, HALF_D + s * LANES:HALF_D + (s + 1) * LANES]
        ref[pl.ds(row0 + s, m, stride=NSL), :] = _pack_pair(lo, hi)


def _dispatch_kernel(slot_ref, tail_ref, x_ref, o_ref, zero_ref, sem):
    tm = x_ref.shape[0] // NSL
    base = pl.program_id(0) * tm

    @pl.when(pl.program_id(0) == 0)
    def _():
        zero_ref[...] = jnp.zeros_like(zero_ref)
        n_half = o_ref.shape[0] // (MOE_HALF * NSL)

        def fill_copy(hb):
            start = pl.multiple_of(hb * (MOE_HALF * NSL), MOE_HALF * NSL)
            return pltpu.make_async_copy(zero_ref, o_ref.at[pl.ds(start, MOE_HALF * NSL)], sem)

        def start_fill(hb, carry):
            @pl.when(tail_ref[hb] > 0)
            def _():
                fill_copy(hb).start()
            return carry

        def wait_fill(hb, carry):
            @pl.when(tail_ref[hb] > 0)
            def _():
                fill_copy(hb).wait()
            return carry

        lax.fori_loop(0, n_half, start_fill, 0)
        lax.fori_loop(0, n_half, wait_fill, 0)

    def row_copy(t, s):
        return pltpu.make_async_copy(x_ref.at[_slab_rows(t)], o_ref.at[_slab_rows(s)], sem)

    def issue(t, carry):
        for k in range(TOP_K):
            row_copy(t, slot_ref[(base + t) * TOP_K + k]).start()
        return carry

    lax.fori_loop(0, tm, issue, 0)

    def drain(t, carry):
        for k in range(TOP_K):
            row_copy(t, slot_ref[(base + t) * TOP_K + k]).wait()
        return carry

    lax.fori_loop(0, tm, drain, 0)


def _dispatch(slots, tails, x_slab, n_slots):
    L = x_slab.shape[0] // NSL
    tm = min(256, L)
    grid_spec = pltpu.PrefetchScalarGridSpec(
        num_scalar_prefetch=2,
        grid=(L // tm,),
        in_specs=[pl.BlockSpec((tm * NSL, LANES), lambda i, sl, tl: (i, 0))],
        out_specs=pl.BlockSpec(memory_space=pl.ANY),
        scratch_shapes=[pltpu.VMEM((MOE_HALF * NSL, LANES), x_slab.dtype), pltpu.SemaphoreType.DMA(())],
    )
    return pl.pallas_call(
        _dispatch_kernel,
        grid_spec=grid_spec,
        out_shape=jax.ShapeDtypeStruct((n_slots * NSL, LANES), x_slab.dtype),
        compiler_params=_cparams(("arbitrary",)),
        name="dispatch",
    )(slots, tails, x_slab)


def _expert_kernel(be_ref, bc_ref, x_ref, wg_ref, wu_ref, bg_ref, bu_ref, wd_ref, bd_ref, o_ref,
                   xb_ref, acc_ref):
    i = pl.program_id(0)
    j = pl.program_id(1)
    nj = pl.num_programs(1)
    hm = MOE_HALF
    count = bc_ref[i]

    for half in range(MOE_BM // hm):
        rows = slice(half * hm, (half + 1) * hm)
        slab_rows = slice(half * hm * NSL, (half + 1) * hm * NSL)
        live = count > half * hm

        @pl.when(live & (j == 0))
        def _():
            for s in range(NSL):
                lo, hi = _unpack_pair(x_ref[pl.ds(half * hm * NSL + s, hm, stride=NSL), :])
                xb_ref[rows, s * LANES:(s + 1) * LANES] = lo.astype(BF16)
                xb_ref[rows, HALF_D + s * LANES:HALF_D + (s + 1) * LANES] = hi.astype(BF16)
            acc_ref[rows, :] = jnp.zeros((hm, acc_ref.shape[1]), F32)

        @pl.when(live)
        def _():
            xb = xb_ref[rows, :]
            gate = jnp.dot(xb, wg_ref[0].astype(BF16), preferred_element_type=F32) + bg_ref[0]
            up = jnp.dot(xb, wu_ref[0].astype(BF16), preferred_element_type=F32) + bu_ref[0]
            gate = jnp.minimum(gate, SWIGLU_LIMIT)
            up = jnp.clip(up, -SWIGLU_LIMIT, SWIGLU_LIMIT)
            glu = gate * _sigmoid(SWIGLU_ALPHA * gate)
            act = ((up + 1.0) * glu).astype(BF16)
            acc_ref[rows, :] += jnp.dot(act, wd_ref[0].astype(BF16), preferred_element_type=F32)

        @pl.when(live & (j == nj - 1))
        def _():
            _store_packed_slabs(o_ref, acc_ref[rows, :] + bd_ref[0], row0=half * hm * NSL)

        @pl.when(jnp.logical_not(live) & (j == nj - 1))
        def _():
            o_ref[slab_rows, :] = jnp.zeros((hm * NSL, LANES), o_ref.dtype)


def _experts(blk_e, blk_count, xs_slab, w_gu, b_gu, w_dn, b_dn):
    n_slots = xs_slab.shape[0] // NSL
    ne, d, two_ff = w_gu.shape
    dff = two_ff // 2
    bm, tf = MOE_BM, MOE_TF
    nj = dff // tf
    nb = n_slots // bm

    def jeff(j, bv, i):
        return jnp.where(bv[i] > 0, j, nj - 1)

    grid_spec = pltpu.PrefetchScalarGridSpec(
        num_scalar_prefetch=2,
        grid=(nb, nj),
        in_specs=[pl.BlockSpec((bm * NSL, LANES), lambda i, j, be, bv: (jnp.where(bv[i] > 0, i, 0), 0)),
                  pl.BlockSpec((1, d, tf), lambda i, j, be, bv: (be[i], 0, jeff(j, bv, i))),
                  pl.BlockSpec((1, d, tf), lambda i, j, be, bv: (be[i], 0, nj + jeff(j, bv, i))),
                  pl.BlockSpec((1, 1, tf), lambda i, j, be, bv: (be[i], 0, jeff(j, bv, i))),
                  pl.BlockSpec((1, 1, tf), lambda i, j, be, bv: (be[i], 0, nj + jeff(j, bv, i))),
                  pl.BlockSpec((1, tf, d), lambda i, j, be, bv: (be[i], jeff(j, bv, i), 0)),
                  pl.BlockSpec((1, 1, d), lambda i, j, be, bv: (be[i], 0, 0))],
        out_specs=pl.BlockSpec((bm * NSL, LANES), lambda i, j, be, bv: (i, 0)),
        scratch_shapes=[pltpu.VMEM((bm, d), BF16), pltpu.VMEM((bm, d), F32)],
    )
    return pl.pallas_call(
        _expert_kernel,
        grid_spec=grid_spec,
        out_shape=jax.ShapeDtypeStruct((n_slots * NSL, LANES), U32),
        compiler_params=_cparams(("arbitrary", "arbitrary")),
        name="experts",
    )(blk_e, blk_count, xs_slab, w_gu, w_gu, b_gu.reshape(ne, 1, two_ff), b_gu.reshape(ne, 1, two_ff),
      w_dn, b_dn.reshape(ne, 1, d))


def _combine_kernel(slot_ref, h_ref, p_ref, g2_ref, fw_ref, yb_ref, o_ref, *scratch):
    buf_sets = (scratch[0:TOP_K], scratch[TOP_K:2 * TOP_K])
    sems = scratch[2 * TOP_K]
    tm = h_ref.shape[0]
    i = pl.program_id(0)

    def gather(tile, which, wait):
        bufs, sem = buf_sets[which], sems.at[which]

        def body(t, carry):
            for k in range(TOP_K):
                s = slot_ref[(tile * tm + t) * TOP_K + k]
                cp = pltpu.make_async_copy(yb_ref.at[_slab_rows(s)], bufs[k].at[_slab_rows(t)], sem)
                if wait:
                    cp.wait()
                else:
                    cp.start()
            return carry

        lax.fori_loop(0, tm, body, 0)

    for which in (0, 1):
        @pl.when(lax.rem(i, 2) == which)
        def _():
            if which == 0:
                @pl.when(i == 0)
                def _():
                    gather(i, 0, wait=False)

            @pl.when(i + 1 < pl.num_programs(0))
            def _():
                gather(i + 1, 1 - which, wait=False)

            gather(i, which, wait=True)
            _combine_reduce(buf_sets[which], h_ref, p_ref, g2_ref, fw_ref, o_ref)


def _combine_reduce(gbufs, h_ref, p_ref, g2_ref, fw_ref, o_ref):
    tm = h_ref.shape[0]
    p = p_ref[...]
    ssq = jnp.zeros((tm, 1), F32)
    for s in range(NSL):
        moe_lo = jnp.zeros((tm, LANES), F32)
        moe_hi = jnp.zeros((tm, LANES), F32)
        for k in range(TOP_K):
            lo, hi = _unpack_pair(gbufs[k][pl.ds(s, tm, stride=NSL), :])
            moe_lo = moe_lo + p[:, k:k + 1] * lo
            moe_hi = moe_hi + p[:, k:k + 1] * hi
        for moe, c0 in ((moe_lo, s * LANES), (moe_hi, HALF_D + s * LANES)):
            cs = slice(c0, c0 + LANES)
            hs = h_ref[:, cs] + g2_ref[:, cs] * moe
            o_ref[:, cs] = hs
            ssq = ssq + jnp.sum(hs * hs, axis=-1, keepdims=True)
    scale = lax.rsqrt(ssq / h_ref.shape[1] + EPS)
    o_ref[...] = o_ref[...] * scale * fw_ref[...]


def _combine(slots, h, p, g2, fw, yb_slab):
    L, d = h.shape
    tm = min(256, L)
    grid_spec = pltpu.PrefetchScalarGridSpec(
        num_scalar_prefetch=1,
        grid=(L // tm,),
        in_specs=[pl.BlockSpec((tm, d), lambda i, sl: (i, 0)),
                  pl.BlockSpec((tm, LANES), lambda i, sl: (i, 0)),
                  pl.BlockSpec((1, d), lambda i, sl: (0, 0)),
                  pl.BlockSpec((1, d), lambda i, sl: (0, 0)),
                  pl.BlockSpec(memory_space=pl.ANY)],
        out_specs=pl.BlockSpec((tm, d), lambda i, sl: (i, 0)),
        scratch_shapes=[pltpu.VMEM((tm * NSL, LANES), U32) for _ in range(2 * TOP_K)]
        + [pltpu.SemaphoreType.DMA((2,))],
    )
    return pl.pallas_call(
        _combine_kernel,
        grid_spec=grid_spec,
        out_shape=jax.ShapeDtypeStruct((L, d), F32),
        compiler_params=_cparams(("arbitrary",)),
        name="combine",
    )(slots, h, p, g2, fw, yb_slab)


def _mixer_scans(x2, nw1, sh1, sc1, w_main, w_dt, conv_w, conv_b, ssm_vecs, rep, lb, row_len, ssd_init, hgrn_init):
    proj, dtr = _inproj(x2, nw1, sh1, sc1, w_main, w_dt)
    xs, bc = _conv_silu(proj, conv_w, conv_b, row_len)
    dtb, alog, amask = ssm_vecs
    ssd = [_ssd_scan(xs, bc, dtr, dtb, alog, amask[d], rep[d], ssd_init[d], rev=bool(d)) for d in (0, 1)]
    o_f, o_b, st_f, st_b = _hgrn_scan(proj, lb, hgrn_init)
    return proj, xs, ssd, [(o_f, st_f), (o_b, st_b)]


def kernel(x, c, ctx, c_ctx, w_ada, b_ada, norm1_w, norm2_w, w_in, conv_w, conv_b, dt_bias, a_log, d_skip,
           ssm_norm_w, hgrn_lb, hgrn_norm_w, w_branch_ssm, w_branch_hgrn, w_out, w_router, b_router,
           w_gate_up, b_gate_up, w_down, b_down, final_norm_w):
    bsz, seq, d = x.shape
    assert bsz == 1 and w_ada.shape[0] == 1
    ctx_len = ctx.shape[1]
    H = SSM_HEADS

    w = w_in[0]
    o_z = SSM_INNER
    o_xbc = o_z + SSM_INNER + 2 * SSM_GROUPS * SSM_STATE
    o_dt = o_xbc + 2 * H
    w_main = jnp.concatenate([w[:, :o_z], w[:, o_dt:], w[:, o_z:o_xbc]], axis=1).astype(BF16)
    w_dt = jnp.pad(w[:, o_xbc:o_dt], ((0, 0), (0, LANES - 2 * H))).astype(BF16)
    dtb = jnp.pad(dt_bias[0].reshape(1, 2 * H), ((0, 0), (0, LANES - 2 * H)))
    alog = jnp.pad(a_log[0].reshape(1, 2 * H), ((0, 0), (0, LANES - 2 * H)))
    lane = jnp.arange(LANES)
    amask = [((lane >= dd * H) & (lane < (dd + 1) * H)).astype(F32).reshape(1, LANES) for dd in (0, 1)]
    chan_head = jnp.arange(SSM_INNER) // SSM_HEAD_DIM
    rep = [(lane[:, None] == (chan_head[None, :] + dd * H)).astype(BF16) for dd in (0, 1)]
    dsk = jnp.repeat(d_skip[0], SSM_HEAD_DIM).reshape(1, SSM_INNER)
    lb = [hgrn_lb[dd] for dd in (0, 1)]
    wbs = w_branch_ssm[0].astype(BF16)
    wbh = w_branch_hgrn[0].astype(BF16)
    wo = w_out[0].astype(BF16)
    wr = jnp.pad(w_router[0], ((0, 0), (0, LANES - N_EXPERTS)))
    br = jnp.pad(b_router[0].reshape(1, N_EXPERTS), ((0, 0), (0, LANES - N_EXPERTS)))
    row = lambda v: v.reshape(1, -1)

    cc = jnp.zeros((8, d), F32).at[0].set(c[0]).at[1].set(c_ctx)
    mod = _modulation(cc, w_ada[0], b_ada[0])
    sh1, sc1, g1, sh2, sc2, g2 = [mod[0:1, k * d:(k + 1) * d] for k in range(6)]
    csh1, csc1 = mod[1:2, 0:d], mod[1:2, d:2 * d]

    ssm_vecs = (dtb, alog, amask)
    zs = jnp.zeros((SSM_GROUPS, SSM_STATE, SSM_GROUP_W), F32)
    zh = jnp.zeros((HGRN_HEADS, HGRN_V, HGRN_K), F32)
    nw1 = row(norm1_w[0])

    _, _, ssd_c, hgrn_c = _mixer_scans(ctx[0], nw1, csh1, csc1, w_main, w_dt, conv_w[0], conv_b[0], ssm_vecs,
                                       rep, lb, ctx_len, (zs, zs), (zh, zh))
    ssd_init = (ssd_c[0][1], ssd_c[1][1])
    hgrn_init = (hgrn_c[0][1], hgrn_c[1][1])

    x2 = x[0]
    proj, xs, ssd_l, hgrn_l = _mixer_scans(x2, nw1, sh1, sc1, w_main, w_dt, conv_w[0], conv_b[0], ssm_vecs,
                                           rep, lb, GRID_W, ssd_init, hgrn_init)
    y_ssm = _ssd_post(ssd_l[0][0], ssd_l[1][0], xs, proj, dsk, row(ssm_norm_w[0]))
    y_hgrn = _hgrn_post(hgrn_l[0][0], hgrn_l[1][0], proj, row(hgrn_norm_w[0]))
    y = _merge(y_ssm, y_hgrn, proj, wbs, wbh)
    h, hn2, logits = _outproj(y, x2, wo, g1, row(norm2_w[0]), sh2, sc2, wr, br)

    idx, pos, p, counts = _route(logits)
    counts = counts[0, :N_EXPERTS].astype(I32)
    bm = MOE_BM
    n_assign = seq * TOP_K
    n_blocks = -(-(n_assign + N_EXPERTS * (bm - 1)) // bm)
    n_slots = n_blocks * bm
    padded = (counts + bm - 1) // bm * bm
    pad_end = jnp.cumsum(padded)
    pad_start = pad_end - padded
    slots = (pad_start[idx[:, :TOP_K]] + pos[:, :TOP_K]).reshape(-1).astype(I32)
    blk_start = jnp.arange(n_blocks, dtype=I32) * bm
    blk_e = jnp.minimum(jnp.sum(blk_start[:, None] >= pad_end[None, :], axis=1), N_EXPERTS - 1).astype(I32)
    blk_count = jnp.clip(pad_start[blk_e] + counts[blk_e] - blk_start, 0, bm).astype(I32)
    half_start = jnp.arange(n_slots // MOE_HALF, dtype=I32) * MOE_HALF
    half_e = jnp.repeat(blk_e, bm // MOE_HALF)
    tails = (pad_start[half_e] + counts[half_e] - half_start < MOE_HALF).astype(I32)

    assert d == 2 * HALF_D
    xs_slab = _dispatch(slots, tails, hn2, n_slots)
    yb_slab = _experts(blk_e, blk_count, xs_slab, w_gate_up[0], b_gate_up[0], w_down[0], b_down[0])
    out = _combine(slots, h, p, g2, row(final_norm_w), yb_slab)
    return out.reshape(bsz, seq, d)
```

```python
import functools

import jax
import jax.numpy as jnp
from jax import lax
from jax.experimental import pallas as pl
from jax.experimental.pallas import tpu as pltpu

F32 = jnp.float32
BF16 = jnp.bfloat16
I32 = jnp.int32
U32 = jnp.uint32
HI = lax.Precision.HIGHEST

EPS = 1e-6
GRID_W = 64

LANES = 128
NSL = 8
HALF_D = NSL * LANES
SSM_HEADS = 32
SSM_HEAD_DIM = 64
SSM_GROUPS = 4
SSM_STATE = 128
SSM_INNER = SSM_HEADS * SSM_HEAD_DIM
SSM_GROUP_W = SSM_INNER // SSM_GROUPS
SSM_CHUNK = 128
HGRN_HEADS = 16
HGRN_K = 128
HGRN_V = 128
HGRN_CHUNK = 64
HGRN_SUB = 16
HGRN_HEAD_GROUP = 4
HGRN_SAFE_DECAY = 60.0
N_EXPERTS = 32
TOP_K = 4
SWIGLU_LIMIT = 7.0
SWIGLU_ALPHA = 1.702
MOE_BM = 1024
MOE_HALF = 512
MOE_TF = 512
VMEM_LIMIT = 56 * 1024 * 1024

COL_Z, COL_Q, COL_F, COL_I, COL_G, COL_GS, COL_GH, COL_XS = 0, 1, 2, 4, 5, 6, 7, 8


def _sigmoid(x):
    return 1.0 / (1.0 + jnp.exp(-x))


def _cparams(sem):
    return pltpu.CompilerParams(dimension_semantics=sem, vmem_limit_bytes=VMEM_LIMIT)


def _mod_kernel(c_ref, w_ref, b_ref, o_ref):
    c = c_ref[...]
    s = c * _sigmoid(c)
    o_ref[...] = jnp.dot(s, w_ref[...], precision=HI, preferred_element_type=F32) + b_ref[...]


def _modulation(cc, w_ada, b_ada):
    d, n = w_ada.shape
    tn = 1024
    return pl.pallas_call(
        _mod_kernel,
        grid=(n // tn,),
        in_specs=[pl.BlockSpec((8, d), lambda j: (0, 0)),
                  pl.BlockSpec((d, tn), lambda j: (0, j)),
                  pl.BlockSpec((1, tn), lambda j: (0, j))],
        out_specs=pl.BlockSpec((8, tn), lambda j: (0, j)),
        out_shape=jax.ShapeDtypeStruct((8, n), F32),
        compiler_params=_cparams(("arbitrary",)),
        name="adaln_mod",
    )(cc, w_ada, b_ada.reshape(1, n))


def _inproj_kernel(x_ref, nw_ref, sh_ref, sc_ref, w_ref, wdt_ref, o_ref, odt_ref, hn_ref):
    @pl.when(pl.program_id(1) == 0)
    def _():
        x = x_ref[...]
        ms = jnp.mean(x * x, axis=-1, keepdims=True)
        y = x * lax.rsqrt(ms + EPS) * nw_ref[...]
        hn = (y * (1.0 + sc_ref[...]) + sh_ref[...]).astype(BF16)
        hn_ref[...] = hn
        odt_ref[...] = jnp.dot(hn, wdt_ref[...], preferred_element_type=F32)

    o_ref[...] = jnp.dot(hn_ref[...], w_ref[...], preferred_element_type=F32)


def _inproj(x2, nw, sh, sc, w_main, w_dt):
    L, d = x2.shape
    n = w_main.shape[1]
    tm = min(1024, L)
    tn = 1024
    return pl.pallas_call(
        _inproj_kernel,
        grid=(L // tm, n // tn),
        in_specs=[pl.BlockSpec((tm, d), lambda i, j: (i, 0)),
                  pl.BlockSpec((1, d), lambda i, j: (0, 0)),
                  pl.BlockSpec((1, d), lambda i, j: (0, 0)),
                  pl.BlockSpec((1, d), lambda i, j: (0, 0)),
                  pl.BlockSpec((d, tn), lambda i, j: (0, j)),
                  pl.BlockSpec((d, LANES), lambda i, j: (0, 0))],
        out_specs=[pl.BlockSpec((tm, tn), lambda i, j: (i, j)),
                   pl.BlockSpec((tm, LANES), lambda i, j: (i, 0))],
        out_shape=[jax.ShapeDtypeStruct((L, n), F32), jax.ShapeDtypeStruct((L, LANES), F32)],
        scratch_shapes=[pltpu.VMEM((tm, d), BF16)],
        compiler_params=_cparams(("parallel", "arbitrary")),
        name="inproj",
    )(x2, nw, sh, sc, w_main, w_dt)


def _conv_body(u, w_ref, b_ref, row_len):
    tb = u.shape[0]
    pos = lax.broadcasted_iota(I32, u.shape, 0) & (row_len - 1)
    acc = u * w_ref[2:3, :] + b_ref[...]
    for k, off in ((0, -2), (1, -1), (3, 1)):
        shifted = pltpu.roll(u, shift=(-off) % tb, axis=0)
        valid = (pos + off >= 0) & (pos + off < row_len)
        acc = acc + jnp.where(valid, shifted, 0.0) * w_ref[k:k + 1, :]
    return acc * _sigmoid(acc)


def _conv_kernel(ux_ref, ub_ref, wx_ref, wb_ref, bx_ref, bb_ref, ox_ref, ob_ref, *, row_len):
    ox_ref[...] = _conv_body(ux_ref[...], wx_ref, bx_ref, row_len)
    ob_ref[...] = _conv_body(ub_ref[...], wb_ref, bb_ref, row_len)


def _conv_silu(proj, conv_w, conv_b, row_len):
    L = proj.shape[0]
    d = SSM_INNER
    nbc = 2 * SSM_GROUPS * SSM_STATE
    tb = max(row_len, min(512, L))
    wx, wb = conv_w[:, :d], conv_w[:, d:]
    bx, bb = conv_b[:d].reshape(1, d), conv_b[d:].reshape(1, nbc)
    return pl.pallas_call(
        functools.partial(_conv_kernel, row_len=row_len),
        grid=(L // tb,),
        in_specs=[pl.BlockSpec((tb, d), lambda i: (i, COL_XS)),
                  pl.BlockSpec((tb, nbc), lambda i: (i, COL_XS * 2 + 2)),
                  pl.BlockSpec((4, d), lambda i: (0, 0)),
                  pl.BlockSpec((4, nbc), lambda i: (0, 0)),
                  pl.BlockSpec((1, d), lambda i: (0, 0)),
                  pl.BlockSpec((1, nbc), lambda i: (0, 0))],
        out_specs=[pl.BlockSpec((tb, d), lambda i: (i, 0)),
                   pl.BlockSpec((tb, nbc), lambda i: (i, 0))],
        out_shape=[jax.ShapeDtypeStruct((L, d), F32), jax.ShapeDtypeStruct((L, nbc), F32)],
        compiler_params=_cparams(("parallel",)),
        name="conv_silu",
    )(proj, proj, wx, wb, bx, bb)


def _ssd_kernel(xs_ref, bc_ref, dt_ref, dtb_ref, alog_ref, amask_ref, rep_ref, s0_ref, *rest, rev, lane0, post):
    if post:
        yf_ref, z_ref, dsk_ref, nw_ref, y_ref, sfin_ref, S_ref = rest
    else:
        y_ref, sfin_ref, S_ref = rest
    n = pl.program_id(0)

    @pl.when(n == 0)
    def _():
        S_ref[...] = s0_ref[...]

    Q = xs_ref.shape[0]
    N = SSM_STATE
    GW = SSM_GROUP_W
    ti = lax.broadcasted_iota(I32, (Q, Q), 0)
    si = lax.broadcasted_iota(I32, (Q, Q), 1)
    allowed = (si >= ti) if rev else (si <= ti)
    tri = allowed.astype(F32)

    raw = dt_ref[...] + dtb_ref[...]
    dt = jnp.maximum(raw, 0.0) + jnp.log1p(jnp.exp(-jnp.abs(raw)))
    loga = dt * (-jnp.exp(alog_ref[...]) * amask_ref[...])
    cs = jnp.dot(tri, loga, precision=HI, preferred_element_type=F32)
    csT = cs.T
    tot = cs[0:1, :] if rev else cs[Q - 1:Q, :]
    fac = jnp.concatenate([jnp.exp(cs), jnp.exp(tot - cs), dt], axis=0).astype(BF16)
    tot8 = jnp.broadcast_to(tot, (8, LANES))
    t_hi = tot8.astype(BF16)
    t_r = tot8 - t_hi.astype(F32)
    t_mid = t_r.astype(BF16)
    t_lo = (t_r - t_mid.astype(F32)).astype(BF16)
    tot3 = jnp.concatenate([t_hi, t_mid, t_lo], axis=0)
    lane = lax.broadcasted_iota(I32, (Q, LANES), 1)
    lo_mask = lane < SSM_HEAD_DIM

    for g in range(SSM_GROUPS):
        rep_g = rep_ref[:, g * GW:(g + 1) * GW]
        fx = jnp.dot(fac, rep_g, preferred_element_type=F32)
        e_in, e_out, dt_x = fx[0:Q], fx[Q:2 * Q], fx[2 * Q:3 * Q]
        tx = jnp.dot(tot3, rep_g, preferred_element_type=F32)
        dec = jnp.exp((tx[0:1] + tx[8:9]) + tx[16:17])
        xdt = xs_ref[:, g * GW:(g + 1) * GW] * dt_x
        xw = (xdt * e_out).astype(BF16)
        xdt_b = xdt.astype(BF16)
        Bg = bc_ref[:, g * N:(g + 1) * N].astype(BF16)
        Cg = bc_ref[:, (SSM_GROUPS + g) * N:(SSM_GROUPS + g + 1) * N].astype(BF16)
        cb = lax.dot_general(Cg, Bg, (((1,), (1,)), ((), ())), preferred_element_type=F32)
        Sg = S_ref[g]
        y_off = jnp.dot(Cg, Sg.astype(BF16), preferred_element_type=F32) * e_in
        upd = lax.dot_general(Bg, xw, (((0,), (0,)), ((), ())), preferred_element_type=F32)
        S_ref[g] = Sg * dec + upd
        pieces = []
        for jp in range(GW // LANES):
            h0 = g * (GW // SSM_HEAD_DIM) + 2 * jp
            ms = []
            for hh in (h0, h0 + 1):
                col = cs[:, lane0 + hh:lane0 + hh + 1]
                row = csT[lane0 + hh:lane0 + hh + 1, :]
                dec_ts = jnp.exp(jnp.where(allowed, col - row, -jnp.inf))
                ms.append((cb * dec_ts).astype(BF16))
            m2 = jnp.concatenate(ms, axis=1)
            x2 = xdt_b[:, jp * LANES:(jp + 1) * LANES]
            r2 = jnp.concatenate([jnp.where(lo_mask, x2, jnp.zeros_like(x2)),
                                  jnp.where(lo_mask, jnp.zeros_like(x2), x2)], axis=0)
            y_d = jnp.dot(m2, r2, preferred_element_type=F32)
            pieces.append(y_d + y_off[:, jp * LANES:(jp + 1) * LANES])
        gs = slice(g * GW, (g + 1) * GW)
        y_g = jnp.concatenate(pieces, axis=1)
        if post:
            z_g = z_ref[:, gs]
            yy = (yf_ref[:, gs] + y_g + dsk_ref[:, gs] * xs_ref[:, gs]) * (z_g * _sigmoid(z_g))
            ms = jnp.mean(yy * yy, axis=-1, keepdims=True)
            y_ref[:, gs] = (yy * lax.rsqrt(ms + EPS) * nw_ref[:, gs]).astype(y_ref.dtype)
        else:
            y_ref[:, gs] = y_g

    @pl.when(n == pl.num_programs(0) - 1)
    def _():
        sfin_ref[...] = S_ref[...]


def _ssd_scan(xs, bc, dt, dtb, alog, amask, rep, s0, rev, post_inputs=None):
    L, d = xs.shape
    Q = SSM_CHUNK
    nc = L // Q
    idx = (lambda n: (nc - 1 - n, 0)) if rev else (lambda n: (n, 0))
    const2 = lambda n: (0, 0)
    const3 = lambda n: (0, 0, 0)
    sshape = (SSM_GROUPS, SSM_STATE, SSM_GROUP_W)
    post = post_inputs is not None
    in_specs = [pl.BlockSpec((Q, d), idx),
                pl.BlockSpec((Q, bc.shape[1]), idx),
                pl.BlockSpec((Q, LANES), idx),
                pl.BlockSpec((1, LANES), const2),
                pl.BlockSpec((1, LANES), const2),
                pl.BlockSpec((1, LANES), const2),
                pl.BlockSpec((LANES, d), const2),
                pl.BlockSpec(sshape, const3)]
    args = [xs, bc, dt, dtb, alog, amask, rep, s0]
    if post:
        zcol = (lambda n: (nc - 1 - n, COL_Z)) if rev else (lambda n: (n, COL_Z))
        in_specs += [pl.BlockSpec((Q, d), idx), pl.BlockSpec((Q, d), zcol),
                     pl.BlockSpec((1, d), const2), pl.BlockSpec((1, d), const2)]
        args += list(post_inputs)
    return pl.pallas_call(
        functools.partial(_ssd_kernel, rev=rev, lane0=SSM_HEADS if rev else 0, post=post),
        grid=(nc,),
        in_specs=in_specs,
        out_specs=[pl.BlockSpec((Q, d), idx),
                   pl.BlockSpec(sshape, const3)],
        out_shape=[jax.ShapeDtypeStruct((L, d), BF16 if post else F32), jax.ShapeDtypeStruct(sshape, F32)],
        scratch_shapes=[pltpu.VMEM(sshape, F32)],
        compiler_params=_cparams(("arbitrary",)),
        name="ssd_bwd" if rev else "ssd_fwd",
    )(*args)


class _HgrnDir:
    def __init__(self, refs, rev):
        self.refs = refs
        self.rev = rev

    def build(self):
        (q_ref, f_ref, i_ref, lb_ref, o_ref, ST_ref, kk_ref, bcs_ref, lf_ref, v_ref, vT_ref), rev = self.refs, self.rev
        _hgrn_dir_body(self, q_ref, f_ref, i_ref, lb_ref, o_ref, ST_ref, kk_ref, bcs_ref, lf_ref, v_ref, vT_ref, rev)
        return self


def _hgrn_dir_body(self, q_ref, f_ref, i_ref, lb_ref, o_ref, ST_ref, kk_ref, bcs_ref, lf_ref, v_ref, vT_ref, rev):
    C = q_ref.shape[0]
    c = HGRN_SUB
    nsub = C // c
    K = HGRN_K
    d = HGRN_HEADS * K
    ti = lax.broadcasted_iota(I32, (C, C), 0)
    si = lax.broadcasted_iota(I32, (C, C), 1)
    allowed = (si >= ti) if rev else (si <= ti)
    bt, bs = ti // c, si // c
    dist = (bs - bt) if rev else (bt - bs)
    order = list(range(nsub - 1, -1, -1)) if rev else list(range(nsub))

    def gate(h):
        sl = slice(h * K, (h + 1) * K)
        lbr = lb_ref[:, sl]
        lmax = jnp.max(lbr, axis=0, keepdims=True)
        le = jnp.exp(lbr - lmax)
        lb = le[0:1, :] / jnp.sum(le, axis=0, keepdims=True)
        fr = f_ref[:, sl]
        e = jnp.exp(-jnp.abs(fr))
        rinv = 1.0 / (1.0 + e)
        pos = fr >= 0.0
        sg = jnp.where(pos, rinv, e * rinv)
        sgn = jnp.where(pos, e * rinv, rinv)
        logf = jnp.log(lb + (1.0 - lb) * sg)
        kk_ref[:, sl] = (1.0 - lb) * sgn
        iv = i_ref[:, sl]
        vv = iv * _sigmoid(iv)
        v_ref[:, sl] = vv
        vT_ref[sl, :] = vv.T.astype(BF16)
        hi = logf.astype(BF16)
        r1 = logf - hi.astype(F32)
        mid = r1.astype(BF16)
        lo = (r1 - mid.astype(F32)).astype(BF16)
        lf_ref[:, h * K:(h + 1) * K] = hi
        lf_ref[:, d + h * K:d + (h + 1) * K] = mid
        lf_ref[:, 2 * d + h * K:2 * d + (h + 1) * K] = lo

    def levels(col):
        r_in, r_out = {}, {}
        for p, sc_i in enumerate(order):
            last = sc_i * c + (0 if rev else c - 1)
            r_out[p] = bcs_ref[last:last + 1, col]
            r_in[p] = r_out[p - 1] if p > 0 else jnp.zeros_like(r_out[p])
        return r_in, r_out

    def cumulate():
        cs3 = jnp.dot(allowed.astype(BF16), lf_ref[...], preferred_element_type=F32)
        bcs_ref[...] = (cs3[:, :d] + cs3[:, d:2 * d]) + cs3[:, 2 * d:]
        r_in_all, r_out_all = levels(slice(0, d))
        worst = r_in_all[0] - r_out_all[0]
        for p in range(1, nsub):
            worst = jnp.maximum(worst, r_in_all[p] - r_out_all[p])
        return worst

    def stages(exact_pairwise):
        t16 = lax.broadcasted_iota(I32, (c, K), 0)
        ones_b = jnp.ones((K, K), BF16)
        t2 = lax.broadcasted_iota(I32, (C, 2 * C), 0)
        col2 = lax.broadcasted_iota(I32, (C, 2 * C), 1)
        s2 = col2 & (C - 1)
        dist2 = (s2 // c - t2 // c) if rev else (t2 // c - s2 // c)
        dist2 = jnp.where(col2 < C, dist2, -1)
        same_blk = (col2 >= C) & (t2 // c == s2 // c) & ((s2 >= t2) if rev else (s2 <= t2))
        band = {dd: dist2 == dd for dd in range(1, nsub)}
        def matmuls(h):
            sl = slice(h * K, (h + 1) * K)
            bcs = bcs_ref[:, sl]
            kk = kk_ref[:, sl]
            qq = q_ref[:, sl] * (K ** -0.5)
            tot = bcs[0:1, :] if rev else bcs[C - 1:C, :]

            r_in, r_out = levels(sl)

            def by_sub_chunk(fn):
                rows = [None] * nsub
                for p, sc_i in enumerate(order):
                    rows[sc_i] = jnp.broadcast_to(fn(p), (c, K))
                return jnp.concatenate(rows, axis=0)

            R_in = by_sub_chunk(lambda p: r_in[p])
            R_out = by_sub_chunk(lambda p: r_out[p])
            qh = qq * jnp.exp(bcs - R_in)
            kt = kk * jnp.exp(R_out - bcs)
            ST = ST_ref[h]
            if exact_pairwise:
                q_in = (qq * jnp.exp(bcs)).astype(BF16)
                k_end = (kk * jnp.exp(tot - bcs)).astype(BF16)
            else:
                q_in = (qh * by_sub_chunk(lambda p: jnp.exp(r_in[p]))).astype(BF16)
                k_end = (kt * by_sub_chunk(lambda p: jnp.exp(tot - r_out[p]))).astype(BF16)
            o = lax.dot_general(q_in, ST.astype(BF16), (((1,), (1,)), ((), ())), preferred_element_type=F32)
            upd = jnp.dot(vT_ref[sl, :], k_end, preferred_element_type=F32)
            new_state = ST * jnp.exp(tot) + upd
            lhs = [qh]
            for dd in range(2, nsub):
                rows_x = [None] * nsub
                for p, sc_i in enumerate(order):
                    if p >= dd:
                        rows_x[sc_i] = jnp.broadcast_to(jnp.exp(r_in[p] - r_out[p - dd]), (c, K))
                    else:
                        rows_x[sc_i] = jnp.zeros((c, K), F32)
                lhs.append(qh * jnp.concatenate(rows_x, axis=0))
            lhs = jnp.concatenate(lhs, axis=0).astype(BF16)
            if exact_pairwise:
                rhs = kt.astype(BF16)
            else:
                k_diag = kt * by_sub_chunk(lambda p: jnp.exp(r_in[p] - r_out[p]))
                rhs = jnp.concatenate([kt, k_diag], axis=0).astype(BF16)
            a = lax.dot_general(lhs, rhs, (((1,), (1,)), ((), ())), preferred_element_type=F32)
            return o, a, new_state

        def finish(h, o, a, new_state):
            sl = slice(h * K, (h + 1) * K)
            ST_ref[h] = new_state
            vv = v_ref[:, sl]
            vb = vv.astype(BF16)
            if exact_pairwise:
                bcs = bcs_ref[:, sl]
                kk = kk_ref[:, sl]
                qq = q_ref[:, sl] * (K ** -0.5)
                att = jnp.where(dist == 1, a[0:C], 0.0)
                for dd in range(2, nsub):
                    att = att + jnp.where(dist == dd, a[(dd - 1) * C:dd * C], 0.0)
                o = o + jnp.dot(att.astype(BF16), vb, preferred_element_type=F32)
                for sc_i in range(nsub):
                    r0 = sc_i * c
                    b_i = bcs[r0:r0 + c, :]
                    q_i = qq[r0:r0 + c, :]
                    ws = []
                    for s in range(c):
                        ok = (t16 <= s) if rev else (t16 >= s)
                        ex = jnp.exp(jnp.where(ok, b_i - bcs[r0 + s:r0 + s + 1, :], -jnp.inf))
                        ws.append((q_i * ex * kk[r0 + s:r0 + s + 1, :]).astype(BF16))
                    w_all = jnp.concatenate(ws, axis=0)
                    a_all = jnp.dot(w_all, ones_b, preferred_element_type=F32)
                    o_d = jnp.zeros((c, HGRN_V), F32)
                    for s in range(c):
                        o_d = o_d + a_all[s * c:(s + 1) * c, :] * vv[r0 + s:r0 + s + 1, :]
                    o_ref[r0:r0 + c, sl] = o[r0:r0 + c, :] + o_d
            else:
                att = jnp.zeros((C, 2 * C), F32)
                for dd in range(nsub - 1, 1, -1):
                    att = jnp.where(band[dd], a[(dd - 1) * C:dd * C], att)
                att = jnp.where(band[1] | same_blk, a[0:C], att)
                v2 = jnp.concatenate([vb, vb], axis=0)
                o_ref[:, sl] = o + jnp.dot(att.astype(BF16), v2, preferred_element_type=F32)

        return matmuls, finish

    self.gate, self.cumulate, self.stages = gate, cumulate, stages


def _hgrn_kernel(qf_ref, ff_ref, if_ref, qb_ref, fb_ref, ib_ref, lbf_ref, lbb_ref, s0f_ref, s0b_ref,
                 of_ref, ob_ref, sff_ref, sfb_ref, *scratch):
    n = pl.program_id(0)
    nscr = len(scratch) // 2
    scr_f, scr_b = scratch[:nscr], scratch[nscr:]

    @pl.when(n == 0)
    def _():
        scr_f[0][...] = s0f_ref[...]
        scr_b[0][...] = s0b_ref[...]

    dirs = [_HgrnDir((qf_ref, ff_ref, if_ref, lbf_ref, of_ref) + tuple(scr_f), False).build(),
            _HgrnDir((qb_ref, fb_ref, ib_ref, lbb_ref, ob_ref) + tuple(scr_b), True).build()]
    for h in range(HGRN_HEADS):
        for dr in dirs:
            dr.gate(h)
    worst = jnp.maximum(dirs[0].cumulate(), dirs[1].cumulate())
    safe = jnp.max(worst) < HGRN_SAFE_DECAY

    def heads(exact_pairwise):
        fns = [dr.stages(exact_pairwise) for dr in dirs]
        group = 1 if exact_pairwise else HGRN_HEAD_GROUP
        for h0 in range(0, HGRN_HEADS, group):
            staged = [(fin, h, mm(h)) for h in range(h0, h0 + group) for mm, fin in fns]
            for fin, h, vals in staged:
                fin(h, *vals)

    @pl.when(safe)
    def _():
        heads(False)

    @pl.when(jnp.logical_not(safe))
    def _():
        heads(True)

    @pl.when(n == pl.num_programs(0) - 1)
    def _():
        sff_ref[...] = scr_f[0][...]
        sfb_ref[...] = scr_b[0][...]


def _hgrn_scan(proj, lb, s0):
    L = proj.shape[0]
    d = HGRN_HEADS * HGRN_K
    C = HGRN_CHUNK
    nc = L // C
    sshape = (HGRN_HEADS, HGRN_V, HGRN_K)
    const3 = lambda n: (0, 0, 0)
    fwd = lambda col: pl.BlockSpec((C, d), lambda n: (n, col))
    bwd = lambda col: pl.BlockSpec((C, d), lambda n: (nc - 1 - n, col))
    lb_spec = pl.BlockSpec(lb[0].shape, lambda n: (0, 0))
    st_spec = pl.BlockSpec(sshape, const3)
    scratch = [pltpu.VMEM(sshape, F32), pltpu.VMEM((C, d), F32), pltpu.VMEM((C, d), F32),
               pltpu.VMEM((C, 3 * d), BF16), pltpu.VMEM((C, d), F32), pltpu.VMEM((d, C), BF16)]
    return pl.pallas_call(
        _hgrn_kernel,
        grid=(nc,),
        in_specs=[fwd(COL_Q), fwd(COL_F), fwd(COL_I), bwd(COL_Q), bwd(COL_F + 1), bwd(COL_I),
                  lb_spec, lb_spec, st_spec, st_spec],
        out_specs=[fwd(0), bwd(0), st_spec, st_spec],
        out_shape=[jax.ShapeDtypeStruct((L, d), F32), jax.ShapeDtypeStruct((L, d), F32),
                   jax.ShapeDtypeStruct(sshape, F32), jax.ShapeDtypeStruct(sshape, F32)],
        scratch_shapes=scratch + scratch,
        compiler_params=_cparams(("arbitrary",)),
        name="hgrn",
    )(proj, proj, proj, proj, proj, proj, lb[0], lb[1], s0[0], s0[1])


def _hgrn_post_kernel(of_ref, ob_ref, g_ref, nw_ref, o_ref):
    nw = nw_ref[...]
    for h in range(HGRN_HEADS):
        sl = slice(h * HGRN_V, (h + 1) * HGRN_V)
        o = of_ref[:, sl] + ob_ref[:, sl]
        ms = jnp.mean(o * o, axis=-1, keepdims=True)
        g = g_ref[:, sl]
        o_ref[:, sl] = (o * lax.rsqrt(ms + EPS) * nw * (g * _sigmoid(g))).astype(o_ref.dtype)


def _hgrn_post(of, ob, proj, nw):
    L, d = of.shape
    tm = min(512, L)
    row = lambda i: (i, 0)
    return pl.pallas_call(
        _hgrn_post_kernel,
        grid=(L // tm,),
        in_specs=[pl.BlockSpec((tm, d), row), pl.BlockSpec((tm, d), row),
                  pl.BlockSpec((tm, d), lambda i: (i, COL_G)),
                  pl.BlockSpec((1, HGRN_V), lambda i: (0, 0))],
        out_specs=pl.BlockSpec((tm, d), row),
        out_shape=jax.ShapeDtypeStruct((L, d), BF16),
        compiler_params=_cparams(("parallel",)),
        name="hgrn_post",
    )(of, ob, proj, nw)


def _merge_kernel(ys_ref, yh_ref, gs_ref, gh_ref, wbs_ref, wbh_ref, o_ref):
    a = jnp.dot(ys_ref[...], wbs_ref[...], preferred_element_type=F32)
    b = jnp.dot(yh_ref[...], wbh_ref[...], preferred_element_type=F32)
    o_ref[...] = (_sigmoid(gs_ref[...]) * a + _sigmoid(gh_ref[...]) * b).astype(o_ref.dtype)


def _merge(ys, yh, proj, wbs, wbh):
    L, d = ys.shape
    tm = min(1024, L)
    tn = 512
    nj = d // tn
    return pl.pallas_call(
        _merge_kernel,
        grid=(L // tm, nj),
        in_specs=[pl.BlockSpec((tm, d), lambda i, j: (i, 0)),
                  pl.BlockSpec((tm, d), lambda i, j: (i, 0)),
                  pl.BlockSpec((tm, tn), lambda i, j: (i, COL_GS * nj + j)),
                  pl.BlockSpec((tm, tn), lambda i, j: (i, COL_GH * nj + j)),
                  pl.BlockSpec((d, tn), lambda i, j: (0, j)),
                  pl.BlockSpec((d, tn), lambda i, j: (0, j))],
        out_specs=pl.BlockSpec((tm, tn), lambda i, j: (i, j)),
        out_shape=jax.ShapeDtypeStruct((L, d), BF16),
        compiler_params=_cparams(("parallel", "arbitrary")),
        name="merge",
    )(ys, yh, proj, proj, wbs, wbh)


def _outproj_kernel(y_ref, x_ref, wo_ref, g1_ref, nw_ref, sh_ref, sc_ref, wr_ref, br_ref,
                    h_ref, hn_ref, lg_ref):
    h = x_ref[...] + g1_ref[...] * jnp.dot(y_ref[...], wo_ref[...], preferred_element_type=F32)
    h_ref[...] = h
    ms = jnp.mean(h * h, axis=-1, keepdims=True)
    hn = h * lax.rsqrt(ms + EPS) * nw_ref[...] * (1.0 + sc_ref[...]) + sh_ref[...]
    h_hi = hn.astype(BF16)
    h_lo = (hn - h_hi.astype(F32)).astype(BF16)
    wr = wr_ref[...]
    w_hi = wr.astype(BF16)
    w_lo = (wr - w_hi.astype(F32)).astype(BF16)
    lg = jnp.dot(h_hi, w_hi, preferred_element_type=F32)
    lg = lg + (jnp.dot(h_hi, w_lo, preferred_element_type=F32) + jnp.dot(h_lo, w_hi, preferred_element_type=F32))
    lg_ref[...] = lg + br_ref[...]
    _store_packed_slabs(hn_ref, hn)


def _outproj(y, x2, wo, g1, nw, sh, sc, wr, br):
    L, d = x2.shape
    tm = min(256, L)
    row = lambda i: (i, 0)
    const = lambda i: (0, 0)
    vec = pl.BlockSpec((1, d), const)
    return pl.pallas_call(
        _outproj_kernel,
        grid=(L // tm,),
        in_specs=[pl.BlockSpec((tm, d), row), pl.BlockSpec((tm, d), row),
                  pl.BlockSpec((d, d), const), vec, vec, vec, vec,
                  pl.BlockSpec((d, LANES), const), pl.BlockSpec((1, LANES), const)],
        out_specs=[pl.BlockSpec((tm, d), row), pl.BlockSpec((tm * NSL, LANES), row), pl.BlockSpec((tm, LANES), row)],
        out_shape=[jax.ShapeDtypeStruct((L, d), F32), jax.ShapeDtypeStruct((L * NSL, LANES), U32),
                   jax.ShapeDtypeStruct((L, LANES), F32)],
        compiler_params=_cparams(("parallel",)),
        name="outproj",
    )(y, x2, wo, g1, nw, sh, sc, wr, br)


def _route_kernel(lg_ref, idx_ref, pos_ref, p_ref, cnt_ref, carry_ref):
    n = pl.program_id(0)

    @pl.when(n == 0)
    def _():
        carry_ref[...] = jnp.zeros_like(carry_ref)

    tm = lg_ref.shape[0]
    lane_i = lax.broadcasted_iota(I32, (tm, LANES), 1)
    lane = lane_i.astype(F32)
    neg = jnp.float32(-jnp.inf)
    l = jnp.where(lane_i < N_EXPERTS, lg_ref[...], neg)
    vals, sels, ids = [], [], []
    for _ in range(TOP_K):
        m = jnp.max(l, axis=-1, keepdims=True)
        i_k = jnp.min(jnp.where(l == m, lane, float(LANES)), axis=-1, keepdims=True)
        sel = lane == i_k
        l = jnp.where(sel, neg, l)
        vals.append(m)
        sels.append(sel)
        ids.append(i_k)
    es = [jnp.exp(v - vals[0]) for v in vals]
    den = es[0]
    for e in es[1:]:
        den = den + e
    onehot = jnp.zeros((tm, LANES), F32)
    for sel in sels:
        onehot = onehot + jnp.where(sel, 1.0, 0.0)
    ti = lax.broadcasted_iota(I32, (tm, tm), 0)
    si = lax.broadcasted_iota(I32, (tm, tm), 1)
    strict = jnp.where(si < ti, 1.0, 0.0).astype(BF16)
    before = jnp.dot(strict, onehot.astype(BF16), preferred_element_type=F32) + carry_ref[...]
    idx_out = jnp.zeros((tm, LANES), F32)
    pos_out = jnp.zeros((tm, LANES), F32)
    p_out = jnp.zeros((tm, LANES), F32)
    for k in range(TOP_K):
        pos_k = jnp.sum(jnp.where(sels[k], before, 0.0), axis=-1, keepdims=True)
        idx_out = jnp.where(lane_i == k, ids[k], idx_out)
        pos_out = jnp.where(lane_i == k, pos_k, pos_out)
        p_out = jnp.where(lane_i == k, es[k] / den, p_out)
    idx_ref[...] = idx_out.astype(I32)
    pos_ref[...] = pos_out.astype(I32)
    p_ref[...] = p_out
    carry_ref[...] = carry_ref[...] + jnp.sum(onehot, axis=0, keepdims=True)
    cnt_ref[...] = carry_ref[...]


def _route(logits):
    L = logits.shape[0]
    tm = min(256, L)
    row = lambda i: (i, 0)
    return pl.pallas_call(
        _route_kernel,
        grid=(L // tm,),
        in_specs=[pl.BlockSpec((tm, LANES), row)],
        out_specs=[pl.BlockSpec((tm, LANES), row), pl.BlockSpec((tm, LANES), row),
                   pl.BlockSpec((tm, LANES), row), pl.BlockSpec((1, LANES), lambda i: (0, 0))],
        out_shape=[jax.ShapeDtypeStruct((L, LANES), I32), jax.ShapeDtypeStruct((L, LANES), I32),
                   jax.ShapeDtypeStruct((L, LANES), F32), jax.ShapeDtypeStruct((1, LANES), F32)],
        scratch_shapes=[pltpu.VMEM((1, LANES), F32)],
        compiler_params=_cparams(("arbitrary",)),
        name="route",
    )(logits)


def _slab_rows(r):
    return pl.ds(pl.multiple_of(r * NSL, NSL), NSL)


def _pack_pair(lo, hi):
    def bf16_bits(v):
        b = lax.bitcast_convert_type(v, U32)
        return (b + jnp.uint32(0x7FFF) + ((b >> 16) & jnp.uint32(1))) >> 16
    return (bf16_bits(hi) << 16) | bf16_bits(lo)


def _unpack_pair(w):
    lo = lax.bitcast_convert_type(w << 16, F32)
    hi = lax.bitcast_convert_type(w & jnp.uint32(0xFFFF0000), F32)
    return lo, hi


def _store_packed_slabs(ref, val, row0=0):
    m = val.shape[0]
    for s in range(NSL):
        lo = val[:, s * LANES:(s + 1) * LANES]
        hi = val[:, HALF_D + s * LANES:HALF_D + (s + 1) * LANES]
        ref[pl.ds(row0 + s, m, stride=NSL), :] = _pack_pair(lo, hi)


def _dispatch_kernel(slot_ref, tail_ref, x_ref, o_ref, zero_ref, sem):
    tm = x_ref.shape[0] // NSL
    base = pl.program_id(0) * tm

    @pl.when(pl.program_id(0) == 0)
    def _():
        zero_ref[...] = jnp.zeros_like(zero_ref)
        n_half = o_ref.shape[0] // (MOE_HALF * NSL)

        def fill_copy(hb):
            start = pl.multiple_of(hb * (MOE_HALF * NSL), MOE_HALF * NSL)
            return pltpu.make_async_copy(zero_ref, o_ref.at[pl.ds(start, MOE_HALF * NSL)], sem)

        def start_fill(hb, carry):
            @pl.when(tail_ref[hb] > 0)
            def _():
                fill_copy(hb).start()
            return carry

        def wait_fill(hb, carry):
            @pl.when(tail_ref[hb] > 0)
            def _():
                fill_copy(hb).wait()
            return carry

        lax.fori_loop(0, n_half, start_fill, 0)
        lax.fori_loop(0, n_half, wait_fill, 0)

    def row_copy(t, s):
        return pltpu.make_async_copy(x_ref.at[_slab_rows(t)], o_ref.at[_slab_rows(s)], sem)

    def issue(t, carry):
        for k in range(TOP_K):
            row_copy(t, slot_ref[(base + t) * TOP_K + k]).start()
        return carry

    lax.fori_loop(0, tm, issue, 0)

    def drain(t, carry):
        for k in range(TOP_K):
            row_copy(t, slot_ref[(base + t) * TOP_K + k]).wait()
        return carry

    lax.fori_loop(0, tm, drain, 0)


def _dispatch(slots, tails, x_slab, n_slots):
    L = x_slab.shape[0] // NSL
    tm = min(256, L)
    grid_spec = pltpu.PrefetchScalarGridSpec(
        num_scalar_prefetch=2,
        grid=(L // tm,),
        in_specs=[pl.BlockSpec((tm * NSL, LANES), lambda i, sl, tl: (i, 0))],
        out_specs=pl.BlockSpec(memory_space=pl.ANY),
        scratch_shapes=[pltpu.VMEM((MOE_HALF * NSL, LANES), x_slab.dtype), pltpu.SemaphoreType.DMA(())],
    )
    return pl.pallas_call(
        _dispatch_kernel,
        grid_spec=grid_spec,
        out_shape=jax.ShapeDtypeStruct((n_slots * NSL, LANES), x_slab.dtype),
        compiler_params=_cparams(("arbitrary",)),
        name="dispatch",
    )(slots, tails, x_slab)


def _expert_kernel(be_ref, bc_ref, x_ref, wg_ref, wu_ref, bg_ref, bu_ref, wd_ref, bd_ref, o_ref,
                   xb_ref, acc_ref):
    i = pl.program_id(0)
    j = pl.program_id(1)
    nj = pl.num_programs(1)
    hm = MOE_HALF
    count = bc_ref[i]

    for half in range(MOE_BM // hm):
        rows = slice(half * hm, (half + 1) * hm)
        slab_rows = slice(half * hm * NSL, (half + 1) * hm * NSL)
        live = count > half * hm

        @pl.when(live & (j == 0))
        def _():
            for s in range(NSL):
                lo, hi = _unpack_pair(x_ref[pl.ds(half * hm * NSL + s, hm, stride=NSL), :])
                xb_ref[rows, s * LANES:(s + 1) * LANES] = lo.astype(BF16)
                xb_ref[rows, HALF_D + s * LANES:HALF_D + (s + 1) * LANES] = hi.astype(BF16)
            acc_ref[rows, :] = jnp.zeros((hm, acc_ref.shape[1]), F32)

        @pl.when(live)
        def _():
            xb = xb_ref[rows, :]
            gate = jnp.dot(xb, wg_ref[0].astype(BF16), preferred_element_type=F32) + bg_ref[0]
            up = jnp.dot(xb, wu_ref[0].astype(BF16), preferred_element_type=F32) + bu_ref[0]
            gate = jnp.minimum(gate, SWIGLU_LIMIT)
            up = jnp.clip(up, -SWIGLU_LIMIT, SWIGLU_LIMIT)
            glu = gate * _sigmoid(SWIGLU_ALPHA * gate)
            act = ((up + 1.0) * glu).astype(BF16)
            acc_ref[rows, :] += jnp.dot(act, wd_ref[0].astype(BF16), preferred_element_type=F32)

        @pl.when(live & (j == nj - 1))
        def _():
            _store_packed_slabs(o_ref, acc_ref[rows, :] + bd_ref[0], row0=half * hm * NSL)

        @pl.when(jnp.logical_not(live) & (j == nj - 1))
        def _():
            o_ref[slab_rows, :] = jnp.zeros((hm * NSL, LANES), o_ref.dtype)


def _experts(blk_e, blk_count, xs_slab, w_gu, b_gu, w_dn, b_dn):
    n_slots = xs_slab.shape[0] // NSL
    ne, d, two_ff = w_gu.shape
    dff = two_ff // 2
    bm, tf = MOE_BM, MOE_TF
    nj = dff // tf
    nb = n_slots // bm

    def jeff(j, bv, i):
        return jnp.where(bv[i] > 0, j, nj - 1)

    grid_spec = pltpu.PrefetchScalarGridSpec(
        num_scalar_prefetch=2,
        grid=(nb, nj),
        in_specs=[pl.BlockSpec((bm * NSL, LANES), lambda i, j, be, bv: (jnp.where(bv[i] > 0, i, 0), 0)),
                  pl.BlockSpec((1, d, tf), lambda i, j, be, bv: (be[i], 0, jeff(j, bv, i))),
                  pl.BlockSpec((1, d, tf), lambda i, j, be, bv: (be[i], 0, nj + jeff(j, bv, i))),
                  pl.BlockSpec((1, 1, tf), lambda i, j, be, bv: (be[i], 0, jeff(j, bv, i))),
                  pl.BlockSpec((1, 1, tf), lambda i, j, be, bv: (be[i], 0, nj + jeff(j, bv, i))),
                  pl.BlockSpec((1, tf, d), lambda i, j, be, bv: (be[i], jeff(j, bv, i), 0)),
                  pl.BlockSpec((1, 1, d), lambda i, j, be, bv: (be[i], 0, 0))],
        out_specs=pl.BlockSpec((bm * NSL, LANES), lambda i, j, be, bv: (i, 0)),
        scratch_shapes=[pltpu.VMEM((bm, d), BF16), pltpu.VMEM((bm, d), F32)],
    )
    return pl.pallas_call(
        _expert_kernel,
        grid_spec=grid_spec,
        out_shape=jax.ShapeDtypeStruct((n_slots * NSL, LANES), U32),
        compiler_params=_cparams(("arbitrary", "arbitrary")),
        name="experts",
    )(blk_e, blk_count, xs_slab, w_gu, w_gu, b_gu.reshape(ne, 1, two_ff), b_gu.reshape(ne, 1, two_ff),
      w_dn, b_dn.reshape(ne, 1, d))


def _combine_kernel(slot_ref, h_ref, p_ref, g2_ref, fw_ref, yb_ref, o_ref, *scratch):
    buf_sets = (scratch[0:TOP_K], scratch[TOP_K:2 * TOP_K])
    sems = scratch[2 * TOP_K]
    tm = h_ref.shape[0]
    i = pl.program_id(0)

    def gather(tile, which, wait):
        bufs, sem = buf_sets[which], sems.at[which]

        def body(t, carry):
            for k in range(TOP_K):
                s = slot_ref[(tile * tm + t) * TOP_K + k]
                cp = pltpu.make_async_copy(yb_ref.at[_slab_rows(s)], bufs[k].at[_slab_rows(t)], sem)
                if wait:
                    cp.wait()
                else:
                    cp.start()
            return carry

        lax.fori_loop(0, tm, body, 0)

    for which in (0, 1):
        @pl.when(lax.rem(i, 2) == which)
        def _():
            if which == 0:
                @pl.when(i == 0)
                def _():
                    gather(i, 0, wait=False)

            @pl.when(i + 1 < pl.num_programs(0))
            def _():
                gather(i + 1, 1 - which, wait=False)

            gather(i, which, wait=True)
            _combine_reduce(buf_sets[which], h_ref, p_ref, g2_ref, fw_ref, o_ref)


def _combine_reduce(gbufs, h_ref, p_ref, g2_ref, fw_ref, o_ref):
    tm = h_ref.shape[0]
    p = p_ref[...]
    ssq = jnp.zeros((tm, 1), F32)
    for s in range(NSL):
        moe_lo = jnp.zeros((tm, LANES), F32)
        moe_hi = jnp.zeros((tm, LANES), F32)
        for k in range(TOP_K):
            lo, hi = _unpack_pair(gbufs[k][pl.ds(s, tm, stride=NSL), :])
            moe_lo = moe_lo + p[:, k:k + 1] * lo
            moe_hi = moe_hi + p[:, k:k + 1] * hi
        for moe, c0 in ((moe_lo, s * LANES), (moe_hi, HALF_D + s * LANES)):
            cs = slice(c0, c0 + LANES)
            hs = h_ref[:, cs] + g2_ref[:, cs] * moe
            o_ref[:, cs] = hs
            ssq = ssq + jnp.sum(hs * hs, axis=-1, keepdims=True)
    scale = lax.rsqrt(ssq / h_ref.shape[1] + EPS)
    o_ref[...] = o_ref[...] * scale * fw_ref[...]


def _combine(slots, h, p, g2, fw, yb_slab):
    L, d = h.shape
    tm = min(256, L)
    grid_spec = pltpu.PrefetchScalarGridSpec(
        num_scalar_prefetch=1,
        grid=(L // tm,),
        in_specs=[pl.BlockSpec((tm, d), lambda i, sl: (i, 0)),
                  pl.BlockSpec((tm, LANES), lambda i, sl: (i, 0)),
                  pl.BlockSpec((1, d), lambda i, sl: (0, 0)),
                  pl.BlockSpec((1, d), lambda i, sl: (0, 0)),
                  pl.BlockSpec(memory_space=pl.ANY)],
        out_specs=pl.BlockSpec((tm, d), lambda i, sl: (i, 0)),
        scratch_shapes=[pltpu.VMEM((tm * NSL, LANES), U32) for _ in range(2 * TOP_K)]
        + [pltpu.SemaphoreType.DMA((2,))],
    )
    return pl.pallas_call(
        _combine_kernel,
        grid_spec=grid_spec,
        out_shape=jax.ShapeDtypeStruct((L, d), F32),
        compiler_params=_cparams(("arbitrary",)),
        name="combine",
    )(slots, h, p, g2, fw, yb_slab)


def _mixer_scans(x2, nw1, sh1, sc1, w_main, w_dt, conv_w, conv_b, ssm_vecs, rep, lb, row_len, ssd_init, hgrn_init,
                 ssd_post=None):
    proj, dtr = _inproj(x2, nw1, sh1, sc1, w_main, w_dt)
    xs, bc = _conv_silu(proj, conv_w, conv_b, row_len)
    dtb, alog, amask = ssm_vecs
    ssd_f = _ssd_scan(xs, bc, dtr, dtb, alog, amask[0], rep[0], ssd_init[0], rev=False)
    post_inputs = None if ssd_post is None else (ssd_f[0], proj) + tuple(ssd_post)
    ssd_b = _ssd_scan(xs, bc, dtr, dtb, alog, amask[1], rep[1], ssd_init[1], rev=True, post_inputs=post_inputs)
    o_f, o_b, st_f, st_b = _hgrn_scan(proj, lb, hgrn_init)
    return proj, [ssd_f, ssd_b], [(o_f, st_f), (o_b, st_b)]


def kernel(x, c, ctx, c_ctx, w_ada, b_ada, norm1_w, norm2_w, w_in, conv_w, conv_b, dt_bias, a_log, d_skip,
           ssm_norm_w, hgrn_lb, hgrn_norm_w, w_branch_ssm, w_branch_hgrn, w_out, w_router, b_router,
           w_gate_up, b_gate_up, w_down, b_down, final_norm_w):
    bsz, seq, d = x.shape
    assert bsz == 1 and w_ada.shape[0] == 1
    ctx_len = ctx.shape[1]
    H = SSM_HEADS

    w = w_in[0]
    o_z = SSM_INNER
    o_xbc = o_z + SSM_INNER + 2 * SSM_GROUPS * SSM_STATE
    o_dt = o_xbc + 2 * H
    w_main = jnp.concatenate([w[:, :o_z].astype(BF16), w[:, o_dt:].astype(BF16), w[:, o_z:o_xbc].astype(BF16)], axis=1)
    w_dt = jnp.pad(w[:, o_xbc:o_dt], ((0, 0), (0, LANES - 2 * H))).astype(BF16)
    dtb = jnp.pad(dt_bias[0].reshape(1, 2 * H), ((0, 0), (0, LANES - 2 * H)))
    alog = jnp.pad(a_log[0].reshape(1, 2 * H), ((0, 0), (0, LANES - 2 * H)))
    lane = jnp.arange(LANES)
    amask = [((lane >= dd * H) & (lane < (dd + 1) * H)).astype(F32).reshape(1, LANES) for dd in (0, 1)]
    chan_head = jnp.arange(SSM_INNER) // SSM_HEAD_DIM
    rep = [(lane[:, None] == (chan_head[None, :] + dd * H)).astype(BF16) for dd in (0, 1)]
    dsk = jnp.repeat(d_skip[0], SSM_HEAD_DIM).reshape(1, SSM_INNER)
    lb = [hgrn_lb[dd] for dd in (0, 1)]
    wbs = w_branch_ssm[0].astype(BF16)
    wbh = w_branch_hgrn[0].astype(BF16)
    wo = w_out[0].astype(BF16)
    wr = jnp.pad(w_router[0], ((0, 0), (0, LANES - N_EXPERTS)))
    br = jnp.pad(b_router[0].reshape(1, N_EXPERTS), ((0, 0), (0, LANES - N_EXPERTS)))
    row = lambda v: v.reshape(1, -1)

    cc = jnp.zeros((8, d), F32).at[0].set(c[0]).at[1].set(c_ctx)
    mod = _modulation(cc, w_ada[0], b_ada[0])
    sh1, sc1, g1, sh2, sc2, g2 = [mod[0:1, k * d:(k + 1) * d] for k in range(6)]
    csh1, csc1 = mod[1:2, 0:d], mod[1:2, d:2 * d]

    ssm_vecs = (dtb, alog, amask)
    zs = jnp.zeros((SSM_GROUPS, SSM_STATE, SSM_GROUP_W), F32)
    zh = jnp.zeros((HGRN_HEADS, HGRN_V, HGRN_K), F32)
    nw1 = row(norm1_w[0])

    _, ssd_c, hgrn_c = _mixer_scans(ctx[0], nw1, csh1, csc1, w_main, w_dt, conv_w[0], conv_b[0], ssm_vecs,
                                    rep, lb, ctx_len, (zs, zs), (zh, zh))
    ssd_init = (ssd_c[0][1], ssd_c[1][1])
    hgrn_init = (hgrn_c[0][1], hgrn_c[1][1])

    x2 = x[0]
    proj, ssd_l, hgrn_l = _mixer_scans(x2, nw1, sh1, sc1, w_main, w_dt, conv_w[0], conv_b[0], ssm_vecs,
                                       rep, lb, GRID_W, ssd_init, hgrn_init, ssd_post=(dsk, row(ssm_norm_w[0])))
    y_ssm = ssd_l[1][0]
    y_hgrn = _hgrn_post(hgrn_l[0][0], hgrn_l[1][0], proj, row(hgrn_norm_w[0]))
    y = _merge(y_ssm, y_hgrn, proj, wbs, wbh)
    h, hn2, logits = _outproj(y, x2, wo, g1, row(norm2_w[0]), sh2, sc2, wr, br)

    idx, pos, p, counts = _route(logits)
    counts = counts[0, :N_EXPERTS].astype(I32)
    bm = MOE_BM
    n_assign = seq * TOP_K
    n_blocks = -(-(n_assign + N_EXPERTS * (bm - 1)) // bm)
    n_slots = n_blocks * bm
    padded = (counts + bm - 1) // bm * bm
    pad_end = jnp.cumsum(padded)
    pad_start = pad_end - padded
    slots = (pad_start[idx[:, :TOP_K]] + pos[:, :TOP_K]).reshape(-1).astype(I32)
    blk_start = jnp.arange(n_blocks, dtype=I32) * bm
    blk_e = jnp.minimum(jnp.sum(blk_start[:, None] >= pad_end[None, :], axis=1), N_EXPERTS - 1).astype(I32)
    blk_count = jnp.clip(pad_start[blk_e] + counts[blk_e] - blk_start, 0, bm).astype(I32)
    half_start = jnp.arange(n_slots // MOE_HALF, dtype=I32) * MOE_HALF
    half_e = jnp.repeat(blk_e, bm // MOE_HALF)
    tails = (pad_start[half_e] + counts[half_e] - half_start < MOE_HALF).astype(I32)

    assert d == 2 * HALF_D
    xs_slab = _dispatch(slots, tails, hn2, n_slots)
    yb_slab = _experts(blk_e, blk_count, xs_slab, w_gate_up[0], b_gate_up[0], w_down[0], b_down[0])
    out = _combine(slots, h, p, g2, row(final_norm_w), yb_slab)
    return out.reshape(bsz, seq, d)
```

```python
import functools

import jax
import jax.numpy as jnp
from jax import lax
from jax.experimental import pallas as pl
from jax.experimental.pallas import tpu as pltpu

F32 = jnp.float32
BF16 = jnp.bfloat16
I32 = jnp.int32
U32 = jnp.uint32
HI = lax.Precision.HIGHEST

EPS = 1e-6
GRID_W = 64

LANES = 128
NSL = 8
HALF_D = NSL * LANES
SSM_HEADS = 32
SSM_HEAD_DIM = 64
SSM_GROUPS = 4
SSM_STATE = 128
SSM_INNER = SSM_HEADS * SSM_HEAD_DIM
SSM_GROUP_W = SSM_INNER // SSM_GROUPS
SSM_CHUNK = 128
HGRN_HEADS = 16
HGRN_K = 128
HGRN_V = 128
HGRN_CHUNK = 64
HGRN_SUB = 16
HGRN_HEAD_GROUP = 4
HGRN_SAFE_DECAY = 60.0
N_EXPERTS = 32
TOP_K = 4
SWIGLU_LIMIT = 7.0
SWIGLU_ALPHA = 1.702
MOE_BM = 1024
MOE_HALF = 512
MOE_TF = 512
VMEM_LIMIT = 56 * 1024 * 1024

COL_Z, COL_Q, COL_F, COL_I, COL_G, COL_GS, COL_GH, COL_XS = 0, 1, 2, 4, 5, 6, 7, 8


def _sigmoid(x):
    return 1.0 / (1.0 + jnp.exp(-x))


def _cparams(sem):
    return pltpu.CompilerParams(dimension_semantics=sem, vmem_limit_bytes=VMEM_LIMIT)


def _mod_kernel(c_ref, w_ref, b_ref, o_ref):
    c = c_ref[...]
    s = c * _sigmoid(c)
    o_ref[...] = jnp.dot(s, w_ref[...], precision=HI, preferred_element_type=F32) + b_ref[...]


def _modulation(cc, w_ada, b_ada):
    d, n = w_ada.shape
    tn = 1024
    return pl.pallas_call(
        _mod_kernel,
        grid=(n // tn,),
        in_specs=[pl.BlockSpec((8, d), lambda j: (0, 0)),
                  pl.BlockSpec((d, tn), lambda j: (0, j)),
                  pl.BlockSpec((1, tn), lambda j: (0, j))],
        out_specs=pl.BlockSpec((8, tn), lambda j: (0, j)),
        out_shape=jax.ShapeDtypeStruct((8, n), F32),
        compiler_params=_cparams(("arbitrary",)),
        name="adaln_mod",
    )(cc, w_ada, b_ada.reshape(1, n))


def _inproj_kernel(x_ref, nw_ref, sh_ref, sc_ref, w_ref, wdt_ref, o_ref, odt_ref, hn_ref):
    @pl.when(pl.program_id(1) == 0)
    def _():
        x = x_ref[...]
        ms = jnp.mean(x * x, axis=-1, keepdims=True)
        y = x * lax.rsqrt(ms + EPS) * nw_ref[...]
        hn = (y * (1.0 + sc_ref[...]) + sh_ref[...]).astype(BF16)
        hn_ref[...] = hn
        odt_ref[...] = jnp.dot(hn, wdt_ref[...], preferred_element_type=F32)

    o_ref[...] = jnp.dot(hn_ref[...], w_ref[...], preferred_element_type=F32)


def _inproj(x2, nw, sh, sc, w_main, w_dt):
    L, d = x2.shape
    n = w_main.shape[1]
    tm = min(1024, L)
    tn = 1024
    return pl.pallas_call(
        _inproj_kernel,
        grid=(L // tm, n // tn),
        in_specs=[pl.BlockSpec((tm, d), lambda i, j: (i, 0)),
                  pl.BlockSpec((1, d), lambda i, j: (0, 0)),
                  pl.BlockSpec((1, d), lambda i, j: (0, 0)),
                  pl.BlockSpec((1, d), lambda i, j: (0, 0)),
                  pl.BlockSpec((d, tn), lambda i, j: (0, j)),
                  pl.BlockSpec((d, LANES), lambda i, j: (0, 0))],
        out_specs=[pl.BlockSpec((tm, tn), lambda i, j: (i, j)),
                   pl.BlockSpec((tm, LANES), lambda i, j: (i, 0))],
        out_shape=[jax.ShapeDtypeStruct((L, n), F32), jax.ShapeDtypeStruct((L, LANES), F32)],
        scratch_shapes=[pltpu.VMEM((tm, d), BF16)],
        compiler_params=_cparams(("parallel", "arbitrary")),
        name="inproj",
    )(x2, nw, sh, sc, w_main, w_dt)


def _conv_body(u, w_ref, b_ref, row_len):
    tb = u.shape[0]
    pos = lax.broadcasted_iota(I32, u.shape, 0) & (row_len - 1)
    acc = u * w_ref[2:3, :] + b_ref[...]
    for k, off in ((0, -2), (1, -1), (3, 1)):
        shifted = pltpu.roll(u, shift=(-off) % tb, axis=0)
        valid = (pos + off >= 0) & (pos + off < row_len)
        acc = acc + jnp.where(valid, shifted, 0.0) * w_ref[k:k + 1, :]
    return acc * _sigmoid(acc)


def _conv_kernel(ux_ref, ub_ref, wx_ref, wb_ref, bx_ref, bb_ref, ox_ref, ob_ref, *, row_len):
    ox_ref[...] = _conv_body(ux_ref[...], wx_ref, bx_ref, row_len)
    ob_ref[...] = _conv_body(ub_ref[...], wb_ref, bb_ref, row_len)


def _conv_silu(proj, conv_w, conv_b, row_len):
    L = proj.shape[0]
    d = SSM_INNER
    nbc = 2 * SSM_GROUPS * SSM_STATE
    tb = max(row_len, min(512, L))
    wx, wb = conv_w[:, :d], conv_w[:, d:]
    bx, bb = conv_b[:d].reshape(1, d), conv_b[d:].reshape(1, nbc)
    return pl.pallas_call(
        functools.partial(_conv_kernel, row_len=row_len),
        grid=(L // tb,),
        in_specs=[pl.BlockSpec((tb, d), lambda i: (i, COL_XS)),
                  pl.BlockSpec((tb, nbc), lambda i: (i, COL_XS * 2 + 2)),
                  pl.BlockSpec((4, d), lambda i: (0, 0)),
                  pl.BlockSpec((4, nbc), lambda i: (0, 0)),
                  pl.BlockSpec((1, d), lambda i: (0, 0)),
                  pl.BlockSpec((1, nbc), lambda i: (0, 0))],
        out_specs=[pl.BlockSpec((tb, d), lambda i: (i, 0)),
                   pl.BlockSpec((tb, nbc), lambda i: (i, 0))],
        out_shape=[jax.ShapeDtypeStruct((L, d), F32), jax.ShapeDtypeStruct((L, nbc), F32)],
        compiler_params=_cparams(("parallel",)),
        name="conv_silu",
    )(proj, proj, wx, wb, bx, bb)


def _ssd_kernel(xs_ref, bc_ref, dt_ref, dtb_ref, alog_ref, amask_ref, rep_ref, s0_ref, *rest, rev, lane0, post):
    if post:
        yf_ref, z_ref, dsk_ref, nw_ref, y_ref, sfin_ref, S_ref = rest
    else:
        y_ref, sfin_ref, S_ref = rest
    n = pl.program_id(0)

    @pl.when(n == 0)
    def _():
        S_ref[...] = s0_ref[...]

    Q = xs_ref.shape[0]
    N = SSM_STATE
    GW = SSM_GROUP_W
    ti = lax.broadcasted_iota(I32, (Q, Q), 0)
    si = lax.broadcasted_iota(I32, (Q, Q), 1)
    allowed = (si >= ti) if rev else (si <= ti)
    tri = allowed.astype(F32)

    raw = dt_ref[...] + dtb_ref[...]
    dt = jnp.maximum(raw, 0.0) + jnp.log1p(jnp.exp(-jnp.abs(raw)))
    loga = dt * (-jnp.exp(alog_ref[...]) * amask_ref[...])
    cs = jnp.dot(tri, loga, precision=HI, preferred_element_type=F32)
    csT = cs.T
    tot = cs[0:1, :] if rev else cs[Q - 1:Q, :]
    fac = jnp.concatenate([jnp.exp(cs), jnp.exp(tot - cs), dt], axis=0).astype(BF16)
    tot8 = jnp.broadcast_to(tot, (8, LANES))
    t_hi = tot8.astype(BF16)
    t_r = tot8 - t_hi.astype(F32)
    t_mid = t_r.astype(BF16)
    t_lo = (t_r - t_mid.astype(F32)).astype(BF16)
    tot3 = jnp.concatenate([t_hi, t_mid, t_lo], axis=0)
    lane = lax.broadcasted_iota(I32, (Q, LANES), 1)
    lo_mask = lane < SSM_HEAD_DIM

    for g in range(SSM_GROUPS):
        rep_g = rep_ref[:, g * GW:(g + 1) * GW]
        fx = jnp.dot(fac, rep_g, preferred_element_type=F32)
        e_in, e_out, dt_x = fx[0:Q], fx[Q:2 * Q], fx[2 * Q:3 * Q]
        tx = jnp.dot(tot3, rep_g, preferred_element_type=F32)
        dec = jnp.exp((tx[0:1] + tx[8:9]) + tx[16:17])
        xdt = xs_ref[:, g * GW:(g + 1) * GW] * dt_x
        xw = (xdt * e_out).astype(BF16)
        xdt_b = xdt.astype(BF16)
        Bg = bc_ref[:, g * N:(g + 1) * N].astype(BF16)
        Cg = bc_ref[:, (SSM_GROUPS + g) * N:(SSM_GROUPS + g + 1) * N].astype(BF16)
        cb = lax.dot_general(Cg, Bg, (((1,), (1,)), ((), ())), preferred_element_type=F32)
        Sg = S_ref[g]
        y_off = jnp.dot(Cg, Sg.astype(BF16), preferred_element_type=F32) * e_in
        upd = lax.dot_general(Bg, xw, (((0,), (0,)), ((), ())), preferred_element_type=F32)
        S_ref[g] = Sg * dec + upd
        pieces = []
        for jp in range(GW // LANES):
            h0 = g * (GW // SSM_HEAD_DIM) + 2 * jp
            ms = []
            for hh in (h0, h0 + 1):
                col = cs[:, lane0 + hh:lane0 + hh + 1]
                row = csT[lane0 + hh:lane0 + hh + 1, :]
                dec_ts = jnp.exp(jnp.where(allowed, col - row, -jnp.inf))
                ms.append((cb * dec_ts).astype(BF16))
            m2 = jnp.concatenate(ms, axis=1)
            x2 = xdt_b[:, jp * LANES:(jp + 1) * LANES]
            r2 = jnp.concatenate([jnp.where(lo_mask, x2, jnp.zeros_like(x2)),
                                  jnp.where(lo_mask, jnp.zeros_like(x2), x2)], axis=0)
            y_d = jnp.dot(m2, r2, preferred_element_type=F32)
            pieces.append(y_d + y_off[:, jp * LANES:(jp + 1) * LANES])
        gs = slice(g * GW, (g + 1) * GW)
        y_g = jnp.concatenate(pieces, axis=1)
        if post:
            z_g = z_ref[:, gs]
            yy = (yf_ref[:, gs].astype(F32) + y_g + dsk_ref[:, gs] * xs_ref[:, gs]) * (z_g * _sigmoid(z_g))
            ms = jnp.mean(yy * yy, axis=-1, keepdims=True)
            y_ref[:, gs] = (yy * lax.rsqrt(ms + EPS) * nw_ref[:, gs]).astype(y_ref.dtype)
        else:
            y_ref[:, gs] = y_g.astype(y_ref.dtype)

    @pl.when(n == pl.num_programs(0) - 1)
    def _():
        sfin_ref[...] = S_ref[...]


def _ssd_scan(xs, bc, dt, dtb, alog, amask, rep, s0, rev, post_inputs=None):
    L, d = xs.shape
    Q = SSM_CHUNK
    nc = L // Q
    idx = (lambda n: (nc - 1 - n, 0)) if rev else (lambda n: (n, 0))
    const2 = lambda n: (0, 0)
    const3 = lambda n: (0, 0, 0)
    sshape = (SSM_GROUPS, SSM_STATE, SSM_GROUP_W)
    post = post_inputs is not None
    in_specs = [pl.BlockSpec((Q, d), idx),
                pl.BlockSpec((Q, bc.shape[1]), idx),
                pl.BlockSpec((Q, LANES), idx),
                pl.BlockSpec((1, LANES), const2),
                pl.BlockSpec((1, LANES), const2),
                pl.BlockSpec((1, LANES), const2),
                pl.BlockSpec((LANES, d), const2),
                pl.BlockSpec(sshape, const3)]
    args = [xs, bc, dt, dtb, alog, amask, rep, s0]
    if post:
        zcol = (lambda n: (nc - 1 - n, COL_Z)) if rev else (lambda n: (n, COL_Z))
        in_specs += [pl.BlockSpec((Q, d), idx), pl.BlockSpec((Q, d), zcol),
                     pl.BlockSpec((1, d), const2), pl.BlockSpec((1, d), const2)]
        args += list(post_inputs)
    return pl.pallas_call(
        functools.partial(_ssd_kernel, rev=rev, lane0=SSM_HEADS if rev else 0, post=post),
        grid=(nc,),
        in_specs=in_specs,
        out_specs=[pl.BlockSpec((Q, d), idx),
                   pl.BlockSpec(sshape, const3)],
        out_shape=[jax.ShapeDtypeStruct((L, d), BF16), jax.ShapeDtypeStruct(sshape, F32)],
        scratch_shapes=[pltpu.VMEM(sshape, F32)],
        compiler_params=_cparams(("arbitrary",)),
        name="ssd_bwd" if rev else "ssd_fwd",
    )(*args)


class _HgrnDir:
    def __init__(self, refs, rev):
        self.refs = refs
        self.rev = rev

    def build(self):
        (q_ref, f_ref, i_ref, lb_ref, o_ref, ST_ref, kk_ref, bcs_ref, lf_ref, v_ref, vT_ref), rev = self.refs, self.rev
        _hgrn_dir_body(self, q_ref, f_ref, i_ref, lb_ref, o_ref, ST_ref, kk_ref, bcs_ref, lf_ref, v_ref, vT_ref, rev)
        return self


def _hgrn_dir_body(self, q_ref, f_ref, i_ref, lb_ref, o_ref, ST_ref, kk_ref, bcs_ref, lf_ref, v_ref, vT_ref, rev):
    C = q_ref.shape[0]
    c = HGRN_SUB
    nsub = C // c
    K = HGRN_K
    d = HGRN_HEADS * K
    ti = lax.broadcasted_iota(I32, (C, C), 0)
    si = lax.broadcasted_iota(I32, (C, C), 1)
    allowed = (si >= ti) if rev else (si <= ti)
    bt, bs = ti // c, si // c
    dist = (bs - bt) if rev else (bt - bs)
    order = list(range(nsub - 1, -1, -1)) if rev else list(range(nsub))

    def gate(h):
        sl = slice(h * K, (h + 1) * K)
        lbr = lb_ref[:, sl]
        lmax = jnp.max(lbr, axis=0, keepdims=True)
        le = jnp.exp(lbr - lmax)
        lb = le[0:1, :] / jnp.sum(le, axis=0, keepdims=True)
        fr = f_ref[:, sl]
        e = jnp.exp(-jnp.abs(fr))
        rinv = 1.0 / (1.0 + e)
        pos = fr >= 0.0
        sg = jnp.where(pos, rinv, e * rinv)
        sgn = jnp.where(pos, e * rinv, rinv)
        logf = jnp.log(lb + (1.0 - lb) * sg)
        kk_ref[:, sl] = (1.0 - lb) * sgn
        iv = i_ref[:, sl]
        vv = iv * _sigmoid(iv)
        v_ref[:, sl] = vv
        vT_ref[sl, :] = vv.T.astype(BF16)
        hi = logf.astype(BF16)
        r1 = logf - hi.astype(F32)
        mid = r1.astype(BF16)
        lo = (r1 - mid.astype(F32)).astype(BF16)
        lf_ref[:, h * K:(h + 1) * K] = hi
        lf_ref[:, d + h * K:d + (h + 1) * K] = mid
        lf_ref[:, 2 * d + h * K:2 * d + (h + 1) * K] = lo

    def levels(col):
        r_in, r_out = {}, {}
        for p, sc_i in enumerate(order):
            last = sc_i * c + (0 if rev else c - 1)
            r_out[p] = bcs_ref[last:last + 1, col]
            r_in[p] = r_out[p - 1] if p > 0 else jnp.zeros_like(r_out[p])
        return r_in, r_out

    def cumulate():
        cs3 = jnp.dot(allowed.astype(BF16), lf_ref[...], preferred_element_type=F32)
        bcs_ref[...] = (cs3[:, :d] + cs3[:, d:2 * d]) + cs3[:, 2 * d:]
        r_in_all, r_out_all = levels(slice(0, d))
        worst = r_in_all[0] - r_out_all[0]
        for p in range(1, nsub):
            worst = jnp.maximum(worst, r_in_all[p] - r_out_all[p])
        return worst

    def stages(exact_pairwise):
        t16 = lax.broadcasted_iota(I32, (c, K), 0)
        ones_b = jnp.ones((K, K), BF16)
        t2 = lax.broadcasted_iota(I32, (C, 2 * C), 0)
        col2 = lax.broadcasted_iota(I32, (C, 2 * C), 1)
        s2 = col2 & (C - 1)
        dist2 = (s2 // c - t2 // c) if rev else (t2 // c - s2 // c)
        dist2 = jnp.where(col2 < C, dist2, -1)
        same_blk = (col2 >= C) & (t2 // c == s2 // c) & ((s2 >= t2) if rev else (s2 <= t2))
        band = {dd: dist2 == dd for dd in range(1, nsub)}
        def matmuls(h):
            sl = slice(h * K, (h + 1) * K)
            bcs = bcs_ref[:, sl]
            kk = kk_ref[:, sl]
            qq = q_ref[:, sl] * (K ** -0.5)
            tot = bcs[0:1, :] if rev else bcs[C - 1:C, :]

            r_in, r_out = levels(sl)

            def by_sub_chunk(fn):
                rows = [None] * nsub
                for p, sc_i in enumerate(order):
                    rows[sc_i] = jnp.broadcast_to(fn(p), (c, K))
                return jnp.concatenate(rows, axis=0)

            R_in = by_sub_chunk(lambda p: r_in[p])
            R_out = by_sub_chunk(lambda p: r_out[p])
            qh = qq * jnp.exp(bcs - R_in)
            kt = kk * jnp.exp(R_out - bcs)
            ST = ST_ref[h]
            if exact_pairwise:
                q_in = (qq * jnp.exp(bcs)).astype(BF16)
                k_end = (kk * jnp.exp(tot - bcs)).astype(BF16)
            else:
                q_in = (qh * by_sub_chunk(lambda p: jnp.exp(r_in[p]))).astype(BF16)
                k_end = (kt * by_sub_chunk(lambda p: jnp.exp(tot - r_out[p]))).astype(BF16)
            o = lax.dot_general(q_in, ST.astype(BF16), (((1,), (1,)), ((), ())), preferred_element_type=F32)
            upd = jnp.dot(vT_ref[sl, :], k_end, preferred_element_type=F32)
            new_state = ST * jnp.exp(tot) + upd
            lhs = [qh]
            for dd in range(2, nsub):
                rows_x = [None] * nsub
                for p, sc_i in enumerate(order):
                    if p >= dd:
                        rows_x[sc_i] = jnp.broadcast_to(jnp.exp(r_in[p] - r_out[p - dd]), (c, K))
                    else:
                        rows_x[sc_i] = jnp.zeros((c, K), F32)
                lhs.append(qh * jnp.concatenate(rows_x, axis=0))
            lhs = jnp.concatenate(lhs, axis=0).astype(BF16)
            if exact_pairwise:
                rhs = kt.astype(BF16)
            else:
                k_diag = kt * by_sub_chunk(lambda p: jnp.exp(r_in[p] - r_out[p]))
                rhs = jnp.concatenate([kt, k_diag], axis=0).astype(BF16)
            a = lax.dot_general(lhs, rhs, (((1,), (1,)), ((), ())), preferred_element_type=F32)
            return o, a, new_state

        def finish(h, o, a, new_state):
            sl = slice(h * K, (h + 1) * K)
            ST_ref[h] = new_state
            vv = v_ref[:, sl]
            vb = vv.astype(BF16)
            if exact_pairwise:
                bcs = bcs_ref[:, sl]
                kk = kk_ref[:, sl]
                qq = q_ref[:, sl] * (K ** -0.5)
                att = jnp.where(dist == 1, a[0:C], 0.0)
                for dd in range(2, nsub):
                    att = att + jnp.where(dist == dd, a[(dd - 1) * C:dd * C], 0.0)
                o = o + jnp.dot(att.astype(BF16), vb, preferred_element_type=F32)
                for sc_i in range(nsub):
                    r0 = sc_i * c
                    b_i = bcs[r0:r0 + c, :]
                    q_i = qq[r0:r0 + c, :]
                    ws = []
                    for s in range(c):
                        ok = (t16 <= s) if rev else (t16 >= s)
                        ex = jnp.exp(jnp.where(ok, b_i - bcs[r0 + s:r0 + s + 1, :], -jnp.inf))
                        ws.append((q_i * ex * kk[r0 + s:r0 + s + 1, :]).astype(BF16))
                    w_all = jnp.concatenate(ws, axis=0)
                    a_all = jnp.dot(w_all, ones_b, preferred_element_type=F32)
                    o_d = jnp.zeros((c, HGRN_V), F32)
                    for s in range(c):
                        o_d = o_d + a_all[s * c:(s + 1) * c, :] * vv[r0 + s:r0 + s + 1, :]
                    o_ref[r0:r0 + c, sl] = (o[r0:r0 + c, :] + o_d).astype(o_ref.dtype)
            else:
                att = jnp.zeros((C, 2 * C), F32)
                for dd in range(nsub - 1, 1, -1):
                    att = jnp.where(band[dd], a[(dd - 1) * C:dd * C], att)
                att = jnp.where(band[1] | same_blk, a[0:C], att)
                v2 = jnp.concatenate([vb, vb], axis=0)
                o_ref[:, sl] = (o + jnp.dot(att.astype(BF16), v2, preferred_element_type=F32)).astype(o_ref.dtype)

        return matmuls, finish

    self.gate, self.cumulate, self.stages = gate, cumulate, stages


def _hgrn_kernel(qf_ref, ff_ref, if_ref, qb_ref, fb_ref, ib_ref, lbf_ref, lbb_ref, s0f_ref, s0b_ref,
                 of_ref, ob_ref, sff_ref, sfb_ref, *scratch):
    n = pl.program_id(0)
    nscr = len(scratch) // 2
    scr_f, scr_b = scratch[:nscr], scratch[nscr:]

    @pl.when(n == 0)
    def _():
        scr_f[0][...] = s0f_ref[...]
        scr_b[0][...] = s0b_ref[...]

    dirs = [_HgrnDir((qf_ref, ff_ref, if_ref, lbf_ref, of_ref) + tuple(scr_f), False).build(),
            _HgrnDir((qb_ref, fb_ref, ib_ref, lbb_ref, ob_ref) + tuple(scr_b), True).build()]
    for h in range(HGRN_HEADS):
        for dr in dirs:
            dr.gate(h)
    worst = jnp.maximum(dirs[0].cumulate(), dirs[1].cumulate())
    safe = jnp.max(worst) < HGRN_SAFE_DECAY

    def heads(exact_pairwise):
        fns = [dr.stages(exact_pairwise) for dr in dirs]
        group = 1 if exact_pairwise else HGRN_HEAD_GROUP
        for h0 in range(0, HGRN_HEADS, group):
            staged = [(fin, h, mm(h)) for h in range(h0, h0 + group) for mm, fin in fns]
            for fin, h, vals in staged:
                fin(h, *vals)

    @pl.when(safe)
    def _():
        heads(False)

    @pl.when(jnp.logical_not(safe))
    def _():
        heads(True)

    @pl.when(n == pl.num_programs(0) - 1)
    def _():
        sff_ref[...] = scr_f[0][...]
        sfb_ref[...] = scr_b[0][...]


def _hgrn_scan(proj, lb, s0):
    L = proj.shape[0]
    d = HGRN_HEADS * HGRN_K
    C = HGRN_CHUNK
    nc = L // C
    sshape = (HGRN_HEADS, HGRN_V, HGRN_K)
    const3 = lambda n: (0, 0, 0)
    fwd = lambda col: pl.BlockSpec((C, d), lambda n: (n, col))
    bwd = lambda col: pl.BlockSpec((C, d), lambda n: (nc - 1 - n, col))
    lb_spec = pl.BlockSpec(lb[0].shape, lambda n: (0, 0))
    st_spec = pl.BlockSpec(sshape, const3)
    scratch = [pltpu.VMEM(sshape, F32), pltpu.VMEM((C, d), F32), pltpu.VMEM((C, d), F32),
               pltpu.VMEM((C, 3 * d), BF16), pltpu.VMEM((C, d), F32), pltpu.VMEM((d, C), BF16)]
    return pl.pallas_call(
        _hgrn_kernel,
        grid=(nc,),
        in_specs=[fwd(COL_Q), fwd(COL_F), fwd(COL_I), bwd(COL_Q), bwd(COL_F + 1), bwd(COL_I),
                  lb_spec, lb_spec, st_spec, st_spec],
        out_specs=[fwd(0), bwd(0), st_spec, st_spec],
        out_shape=[jax.ShapeDtypeStruct((L, d), BF16), jax.ShapeDtypeStruct((L, d), BF16),
                   jax.ShapeDtypeStruct(sshape, F32), jax.ShapeDtypeStruct(sshape, F32)],
        scratch_shapes=scratch + scratch,
        compiler_params=_cparams(("arbitrary",)),
        name="hgrn",
    )(proj, proj, proj, proj, proj, proj, lb[0], lb[1], s0[0], s0[1])


def _hgrn_post_kernel(of_ref, ob_ref, g_ref, nw_ref, o_ref):
    nw = nw_ref[...]
    for h in range(HGRN_HEADS):
        sl = slice(h * HGRN_V, (h + 1) * HGRN_V)
        o = of_ref[:, sl].astype(F32) + ob_ref[:, sl].astype(F32)
        ms = jnp.mean(o * o, axis=-1, keepdims=True)
        g = g_ref[:, sl]
        o_ref[:, sl] = (o * lax.rsqrt(ms + EPS) * nw * (g * _sigmoid(g))).astype(o_ref.dtype)


def _hgrn_post(of, ob, proj, nw):
    L, d = of.shape
    tm = min(512, L)
    row = lambda i: (i, 0)
    return pl.pallas_call(
        _hgrn_post_kernel,
        grid=(L // tm,),
        in_specs=[pl.BlockSpec((tm, d), row), pl.BlockSpec((tm, d), row),
                  pl.BlockSpec((tm, d), lambda i: (i, COL_G)),
                  pl.BlockSpec((1, HGRN_V), lambda i: (0, 0))],
        out_specs=pl.BlockSpec((tm, d), row),
        out_shape=jax.ShapeDtypeStruct((L, d), BF16),
        compiler_params=_cparams(("parallel",)),
        name="hgrn_post",
    )(of, ob, proj, nw)


def _merge_kernel(ys_ref, yh_ref, gs_ref, gh_ref, wbs_ref, wbh_ref, o_ref):
    a = jnp.dot(ys_ref[...], wbs_ref[...], preferred_element_type=F32)
    b = jnp.dot(yh_ref[...], wbh_ref[...], preferred_element_type=F32)
    o_ref[...] = (_sigmoid(gs_ref[...]) * a + _sigmoid(gh_ref[...]) * b).astype(o_ref.dtype)


def _merge(ys, yh, proj, wbs, wbh):
    L, d = ys.shape
    tm = min(1024, L)
    tn = 512
    nj = d // tn
    return pl.pallas_call(
        _merge_kernel,
        grid=(L // tm, nj),
        in_specs=[pl.BlockSpec((tm, d), lambda i, j: (i, 0)),
                  pl.BlockSpec((tm, d), lambda i, j: (i, 0)),
                  pl.BlockSpec((tm, tn), lambda i, j: (i, COL_GS * nj + j)),
                  pl.BlockSpec((tm, tn), lambda i, j: (i, COL_GH * nj + j)),
                  pl.BlockSpec((d, tn), lambda i, j: (0, j)),
                  pl.BlockSpec((d, tn), lambda i, j: (0, j))],
        out_specs=pl.BlockSpec((tm, tn), lambda i, j: (i, j)),
        out_shape=jax.ShapeDtypeStruct((L, d), BF16),
        compiler_params=_cparams(("parallel", "arbitrary")),
        name="merge",
    )(ys, yh, proj, proj, wbs, wbh)


def _outproj_kernel(y_ref, x_ref, wo_ref, g1_ref, nw_ref, sh_ref, sc_ref, wr_ref, br_ref,
                    h_ref, hn_ref, lg_ref):
    h = x_ref[...] + g1_ref[...] * jnp.dot(y_ref[...], wo_ref[...], preferred_element_type=F32)
    h_ref[...] = h
    ms = jnp.mean(h * h, axis=-1, keepdims=True)
    hn = h * lax.rsqrt(ms + EPS) * nw_ref[...] * (1.0 + sc_ref[...]) + sh_ref[...]
    h_hi = hn.astype(BF16)
    h_lo = (hn - h_hi.astype(F32)).astype(BF16)
    wr = wr_ref[...]
    w_hi = wr.astype(BF16)
    w_lo = (wr - w_hi.astype(F32)).astype(BF16)
    lg = jnp.dot(h_hi, w_hi, preferred_element_type=F32)
    lg = lg + (jnp.dot(h_hi, w_lo, preferred_element_type=F32) + jnp.dot(h_lo, w_hi, preferred_element_type=F32))
    lg_ref[...] = lg + br_ref[...]
    _store_packed_slabs(hn_ref, hn)


def _outproj(y, x2, wo, g1, nw, sh, sc, wr, br):
    L, d = x2.shape
    tm = min(256, L)
    row = lambda i: (i, 0)
    const = lambda i: (0, 0)
    vec = pl.BlockSpec((1, d), const)
    return pl.pallas_call(
        _outproj_kernel,
        grid=(L // tm,),
        in_specs=[pl.BlockSpec((tm, d), row), pl.BlockSpec((tm, d), row),
                  pl.BlockSpec((d, d), const), vec, vec, vec, vec,
                  pl.BlockSpec((d, LANES), const), pl.BlockSpec((1, LANES), const)],
        out_specs=[pl.BlockSpec((tm, d), row), pl.BlockSpec((tm * NSL, LANES), row), pl.BlockSpec((tm, LANES), row)],
        out_shape=[jax.ShapeDtypeStruct((L, d), F32), jax.ShapeDtypeStruct((L * NSL, LANES), U32),
                   jax.ShapeDtypeStruct((L, LANES), F32)],
        compiler_params=_cparams(("parallel",)),
        name="outproj",
    )(y, x2, wo, g1, nw, sh, sc, wr, br)


def _route_kernel(lg_ref, idx_ref, pos_ref, p_ref, cnt_ref, carry_ref):
    n = pl.program_id(0)

    @pl.when(n == 0)
    def _():
        carry_ref[...] = jnp.zeros_like(carry_ref)

    tm = lg_ref.shape[0]
    lane_i = lax.broadcasted_iota(I32, (tm, LANES), 1)
    lane = lane_i.astype(F32)
    neg = jnp.float32(-jnp.inf)
    l = jnp.where(lane_i < N_EXPERTS, lg_ref[...], neg)
    vals, sels, ids = [], [], []
    for _ in range(TOP_K):
        m = jnp.max(l, axis=-1, keepdims=True)
        i_k = jnp.min(jnp.where(l == m, lane, float(LANES)), axis=-1, keepdims=True)
        sel = lane == i_k
        l = jnp.where(sel, neg, l)
        vals.append(m)
        sels.append(sel)
        ids.append(i_k)
    es = [jnp.exp(v - vals[0]) for v in vals]
    den = es[0]
    for e in es[1:]:
        den = den + e
    onehot = jnp.zeros((tm, LANES), F32)
    for sel in sels:
        onehot = onehot + jnp.where(sel, 1.0, 0.0)
    ti = lax.broadcasted_iota(I32, (tm, tm), 0)
    si = lax.broadcasted_iota(I32, (tm, tm), 1)
    strict = jnp.where(si < ti, 1.0, 0.0).astype(BF16)
    before = jnp.dot(strict, onehot.astype(BF16), preferred_element_type=F32) + carry_ref[...]
    idx_out = jnp.zeros((tm, LANES), F32)
    pos_out = jnp.zeros((tm, LANES), F32)
    p_out = jnp.zeros((tm, LANES), F32)
    for k in range(TOP_K):
        pos_k = jnp.sum(jnp.where(sels[k], before, 0.0), axis=-1, keepdims=True)
        idx_out = jnp.where(lane_i == k, ids[k], idx_out)
        pos_out = jnp.where(lane_i == k, pos_k, pos_out)
        p_out = jnp.where(lane_i == k, es[k] / den, p_out)
    idx_ref[...] = idx_out.astype(I32)
    pos_ref[...] = pos_out.astype(I32)
    p_ref[...] = p_out
    carry_ref[...] = carry_ref[...] + jnp.sum(onehot, axis=0, keepdims=True)
    cnt_ref[...] = carry_ref[...]


def _route(logits):
    L = logits.shape[0]
    tm = min(256, L)
    row = lambda i: (i, 0)
    return pl.pallas_call(
        _route_kernel,
        grid=(L // tm,),
        in_specs=[pl.BlockSpec((tm, LANES), row)],
        out_specs=[pl.BlockSpec((tm, LANES), row), pl.BlockSpec((tm, LANES), row),
                   pl.BlockSpec((tm, LANES), row), pl.BlockSpec((1, LANES), lambda i: (0, 0))],
        out_shape=[jax.ShapeDtypeStruct((L, LANES), I32), jax.ShapeDtypeStruct((L, LANES), I32),
                   jax.ShapeDtypeStruct((L, LANES), F32), jax.ShapeDtypeStruct((1, LANES), F32)],
        scratch_shapes=[pltpu.VMEM((1, LANES), F32)],
        compiler_params=_cparams(("arbitrary",)),
        name="route",
    )(logits)


def _slab_rows(r):
    return pl.ds(pl.multiple_of(r * NSL, NSL), NSL)


def _pack_pair(lo, hi):
    def bf16_bits(v):
        b = lax.bitcast_convert_type(v, U32)
        return (b + jnp.uint32(0x7FFF) + ((b >> 16) & jnp.uint32(1))) >> 16
    return (bf16_bits(hi) << 16) | bf16_bits(lo)


def _unpack_pair(w):
    lo = lax.bitcast_convert_type(w << 16, F32)
    hi = lax.bitcast_convert_type(w & jnp.uint32(0xFFFF0000), F32)
    return lo, hi


def _store_packed_slabs(ref, val, row0=0):
    m = val.shape[0]
    for s in range(NSL):
        lo = val[:, s * LANES:(s + 1) * LANES]
        hi = val[:, HALF_D + s * LANES:HALF_D + (s + 1) * LANES]
        ref[pl.ds(row0 + s, m, stride=NSL), :] = _pack_pair(lo, hi)


def _dispatch_kernel(slot_ref, tail_ref, x_ref, o_ref, zero_ref, sem):
    tm = x_ref.shape[0] // NSL
    base = pl.program_id(0) * tm

    @pl.when(pl.program_id(0) == 0)
    def _():
        zero_ref[...] = jnp.zeros_like(zero_ref)
        n_half = o_ref.shape[0] // (MOE_HALF * NSL)

        def fill_copy(hb):
            start = pl.multiple_of(hb * (MOE_HALF * NSL), MOE_HALF * NSL)
            return pltpu.make_async_copy(zero_ref, o_ref.at[pl.ds(start, MOE_HALF * NSL)], sem)

        def start_fill(hb, carry):
            @pl.when(tail_ref[hb] > 0)
            def _():
                fill_copy(hb).start()
            return carry

        def wait_fill(hb, carry):
            @pl.when(tail_ref[hb] > 0)
            def _():
                fill_copy(hb).wait()
            return carry

        lax.fori_loop(0, n_half, start_fill, 0)
        lax.fori_loop(0, n_half, wait_fill, 0)

    def row_copy(t, s):
        return pltpu.make_async_copy(x_ref.at[_slab_rows(t)], o_ref.at[_slab_rows(s)], sem)

    def issue(t, carry):
        for k in range(TOP_K):
            row_copy(t, slot_ref[(base + t) * TOP_K + k]).start()
        return carry

    lax.fori_loop(0, tm, issue, 0)

    def drain(t, carry):
        for k in range(TOP_K):
            row_copy(t, slot_ref[(base + t) * TOP_K + k]).wait()
        return carry

    lax.fori_loop(0, tm, drain, 0)


def _dispatch(slots, tails, x_slab, n_slots):
    L = x_slab.shape[0] // NSL
    tm = min(256, L)
    grid_spec = pltpu.PrefetchScalarGridSpec(
        num_scalar_prefetch=2,
        grid=(L // tm,),
        in_specs=[pl.BlockSpec((tm * NSL, LANES), lambda i, sl, tl: (i, 0))],
        out_specs=pl.BlockSpec(memory_space=pl.ANY),
        scratch_shapes=[pltpu.VMEM((MOE_HALF * NSL, LANES), x_slab.dtype), pltpu.SemaphoreType.DMA(())],
    )
    return pl.pallas_call(
        _dispatch_kernel,
        grid_spec=grid_spec,
        out_shape=jax.ShapeDtypeStruct((n_slots * NSL, LANES), x_slab.dtype),
        compiler_params=_cparams(("arbitrary",)),
        name="dispatch",
    )(slots, tails, x_slab)


def _expert_kernel(be_ref, bc_ref, x_ref, wg_ref, wu_ref, bg_ref, bu_ref, wd_ref, bd_ref, o_ref,
                   xb_ref, acc_ref):
    i = pl.program_id(0)
    j = pl.program_id(1)
    nj = pl.num_programs(1)
    hm = MOE_HALF
    count = bc_ref[i]

    for half in range(MOE_BM // hm):
        rows = slice(half * hm, (half + 1) * hm)
        slab_rows = slice(half * hm * NSL, (half + 1) * hm * NSL)
        live = count > half * hm

        @pl.when(live & (j == 0))
        def _():
            for s in range(NSL):
                lo, hi = _unpack_pair(x_ref[pl.ds(half * hm * NSL + s, hm, stride=NSL), :])
                xb_ref[rows, s * LANES:(s + 1) * LANES] = lo.astype(BF16)
                xb_ref[rows, HALF_D + s * LANES:HALF_D + (s + 1) * LANES] = hi.astype(BF16)
            acc_ref[rows, :] = jnp.zeros((hm, acc_ref.shape[1]), F32)

        @pl.when(live)
        def _():
            xb = xb_ref[rows, :]
            gate = jnp.dot(xb, wg_ref[0].astype(BF16), preferred_element_type=F32) + bg_ref[0]
            up = jnp.dot(xb, wu_ref[0].astype(BF16), preferred_element_type=F32) + bu_ref[0]
            gate = jnp.minimum(gate, SWIGLU_LIMIT)
            up = jnp.clip(up, -SWIGLU_LIMIT, SWIGLU_LIMIT)
            glu = gate * _sigmoid(SWIGLU_ALPHA * gate)
            act = ((up + 1.0) * glu).astype(BF16)
            acc_ref[rows, :] += jnp.dot(act, wd_ref[0].astype(BF16), preferred_element_type=F32)

        @pl.when(live & (j == nj - 1))
        def _():
            _store_packed_slabs(o_ref, acc_ref[rows, :] + bd_ref[0], row0=half * hm * NSL)

        @pl.when(jnp.logical_not(live) & (j == nj - 1))
        def _():
            o_ref[slab_rows, :] = jnp.zeros((hm * NSL, LANES), o_ref.dtype)


def _experts(blk_e, blk_count, xs_slab, w_gu, b_gu, w_dn, b_dn):
    n_slots = xs_slab.shape[0] // NSL
    ne, d, two_ff = w_gu.shape
    dff = two_ff // 2
    bm, tf = MOE_BM, MOE_TF
    nj = dff // tf
    nb = n_slots // bm

    def jeff(j, bv, i):
        return jnp.where(bv[i] > 0, j, nj - 1)

    grid_spec = pltpu.PrefetchScalarGridSpec(
        num_scalar_prefetch=2,
        grid=(nb, nj),
        in_specs=[pl.BlockSpec((bm * NSL, LANES), lambda i, j, be, bv: (jnp.where(bv[i] > 0, i, 0), 0)),
                  pl.BlockSpec((1, d, tf), lambda i, j, be, bv: (be[i], 0, jeff(j, bv, i))),
                  pl.BlockSpec((1, d, tf), lambda i, j, be, bv: (be[i], 0, nj + jeff(j, bv, i))),
                  pl.BlockSpec((1, 1, tf), lambda i, j, be, bv: (be[i], 0, jeff(j, bv, i))),
                  pl.BlockSpec((1, 1, tf), lambda i, j, be, bv: (be[i], 0, nj + jeff(j, bv, i))),
                  pl.BlockSpec((1, tf, d), lambda i, j, be, bv: (be[i], jeff(j, bv, i), 0)),
                  pl.BlockSpec((1, 1, d), lambda i, j, be, bv: (be[i], 0, 0))],
        out_specs=pl.BlockSpec((bm * NSL, LANES), lambda i, j, be, bv: (i, 0)),
        scratch_shapes=[pltpu.VMEM((bm, d), BF16), pltpu.VMEM((bm, d), F32)],
    )
    return pl.pallas_call(
        _expert_kernel,
        grid_spec=grid_spec,
        out_shape=jax.ShapeDtypeStruct((n_slots * NSL, LANES), U32),
        compiler_params=_cparams(("arbitrary", "arbitrary")),
        name="experts",
    )(blk_e, blk_count, xs_slab, w_gu, w_gu, b_gu.reshape(ne, 1, two_ff), b_gu.reshape(ne, 1, two_ff),
      w_dn, b_dn.reshape(ne, 1, d))


def _combine_kernel(slot_ref, h_ref, p_ref, g2_ref, fw_ref, yb_ref, o_ref, *scratch):
    buf_sets = (scratch[0:TOP_K], scratch[TOP_K:2 * TOP_K])
    sems = scratch[2 * TOP_K]
    tm = h_ref.shape[0]
    i = pl.program_id(0)

    def gather(tile, which, wait):
        bufs, sem = buf_sets[which], sems.at[which]

        def body(t, carry):
            for k in range(TOP_K):
                s = slot_ref[(tile * tm + t) * TOP_K + k]
                cp = pltpu.make_async_copy(yb_ref.at[_slab_rows(s)], bufs[k].at[_slab_rows(t)], sem)
                if wait:
                    cp.wait()
                else:
                    cp.start()
            return carry

        lax.fori_loop(0, tm, body, 0)

    for which in (0, 1):
        @pl.when(lax.rem(i, 2) == which)
        def _():
            if which == 0:
                @pl.when(i == 0)
                def _():
                    gather(i, 0, wait=False)

            @pl.when(i + 1 < pl.num_programs(0))
            def _():
                gather(i + 1, 1 - which, wait=False)

            gather(i, which, wait=True)
            _combine_reduce(buf_sets[which], h_ref, p_ref, g2_ref, fw_ref, o_ref)


def _combine_reduce(gbufs, h_ref, p_ref, g2_ref, fw_ref, o_ref):
    tm = h_ref.shape[0]
    p = p_ref[...]
    ssq = jnp.zeros((tm, 1), F32)
    for s in range(NSL):
        moe_lo = jnp.zeros((tm, LANES), F32)
        moe_hi = jnp.zeros((tm, LANES), F32)
        for k in range(TOP_K):
            lo, hi = _unpack_pair(gbufs[k][pl.ds(s, tm, stride=NSL), :])
            moe_lo = moe_lo + p[:, k:k + 1] * lo
            moe_hi = moe_hi + p[:, k:k + 1] * hi
        for moe, c0 in ((moe_lo, s * LANES), (moe_hi, HALF_D + s * LANES)):
            cs = slice(c0, c0 + LANES)
            hs = h_ref[:, cs] + g2_ref[:, cs] * moe
            o_ref[:, cs] = hs
            ssq = ssq + jnp.sum(hs * hs, axis=-1, keepdims=True)
    scale = lax.rsqrt(ssq / h_ref.shape[1] + EPS)
    o_ref[...] = o_ref[...] * scale * fw_ref[...]


def _combine(slots, h, p, g2, fw, yb_slab):
    L, d = h.shape
    tm = min(256, L)
    grid_spec = pltpu.PrefetchScalarGridSpec(
        num_scalar_prefetch=1,
        grid=(L // tm,),
        in_specs=[pl.BlockSpec((tm, d), lambda i, sl: (i, 0)),
                  pl.BlockSpec((tm, LANES), lambda i, sl: (i, 0)),
                  pl.BlockSpec((1, d), lambda i, sl: (0, 0)),
                  pl.BlockSpec((1, d), lambda i, sl: (0, 0)),
                  pl.BlockSpec(memory_space=pl.ANY)],
        out_specs=pl.BlockSpec((tm, d), lambda i, sl: (i, 0)),
        scratch_shapes=[pltpu.VMEM((tm * NSL, LANES), U32) for _ in range(2 * TOP_K)]
        + [pltpu.SemaphoreType.DMA((2,))],
    )
    return pl.pallas_call(
        _combine_kernel,
        grid_spec=grid_spec,
        out_shape=jax.ShapeDtypeStruct((L, d), F32),
        compiler_params=_cparams(("arbitrary",)),
        name="combine",
    )(slots, h, p, g2, fw, yb_slab)


def _mixer_scans(x2, nw1, sh1, sc1, w_main, w_dt, conv_w, conv_b, ssm_vecs, rep, lb, row_len, ssd_init, hgrn_init,
                 ssd_post=None):
    proj, dtr = _inproj(x2, nw1, sh1, sc1, w_main, w_dt)
    xs, bc = _conv_silu(proj, conv_w, conv_b, row_len)
    dtb, alog, amask = ssm_vecs
    ssd_f = _ssd_scan(xs, bc, dtr, dtb, alog, amask[0], rep[0], ssd_init[0], rev=False)
    post_inputs = None if ssd_post is None else (ssd_f[0], proj) + tuple(ssd_post)
    ssd_b = _ssd_scan(xs, bc, dtr, dtb, alog, amask[1], rep[1], ssd_init[1], rev=True, post_inputs=post_inputs)
    o_f, o_b, st_f, st_b = _hgrn_scan(proj, lb, hgrn_init)
    return proj, [ssd_f, ssd_b], [(o_f, st_f), (o_b, st_b)]


def kernel(x, c, ctx, c_ctx, w_ada, b_ada, norm1_w, norm2_w, w_in, conv_w, conv_b, dt_bias, a_log, d_skip,
           ssm_norm_w, hgrn_lb, hgrn_norm_w, w_branch_ssm, w_branch_hgrn, w_out, w_router, b_router,
           w_gate_up, b_gate_up, w_down, b_down, final_norm_w):
    bsz, seq, d = x.shape
    assert bsz == 1 and w_ada.shape[0] == 1
    ctx_len = ctx.shape[1]
    H = SSM_HEADS

    w = w_in[0]
    o_z = SSM_INNER
    o_xbc = o_z + SSM_INNER + 2 * SSM_GROUPS * SSM_STATE
    o_dt = o_xbc + 2 * H
    w_main = jnp.concatenate([w[:, :o_z].astype(BF16), w[:, o_dt:].astype(BF16), w[:, o_z:o_xbc].astype(BF16)], axis=1)
    w_dt = jnp.pad(w[:, o_xbc:o_dt], ((0, 0), (0, LANES - 2 * H))).astype(BF16)
    dtb = jnp.pad(dt_bias[0].reshape(1, 2 * H), ((0, 0), (0, LANES - 2 * H)))
    alog = jnp.pad(a_log[0].reshape(1, 2 * H), ((0, 0), (0, LANES - 2 * H)))
    lane = jnp.arange(LANES)
    amask = [((lane >= dd * H) & (lane < (dd + 1) * H)).astype(F32).reshape(1, LANES) for dd in (0, 1)]
    chan_head = jnp.arange(SSM_INNER) // SSM_HEAD_DIM
    rep = [(lane[:, None] == (chan_head[None, :] + dd * H)).astype(BF16) for dd in (0, 1)]
    dsk = jnp.repeat(d_skip[0], SSM_HEAD_DIM).reshape(1, SSM_INNER)
    lb = [hgrn_lb[dd] for dd in (0, 1)]
    wbs = w_branch_ssm[0].astype(BF16)
    wbh = w_branch_hgrn[0].astype(BF16)
    wo = w_out[0].astype(BF16)
    wr = jnp.pad(w_router[0], ((0, 0), (0, LANES - N_EXPERTS)))
    br = jnp.pad(b_router[0].reshape(1, N_EXPERTS), ((0, 0), (0, LANES - N_EXPERTS)))
    row = lambda v: v.reshape(1, -1)

    cc = jnp.zeros((8, d), F32).at[0].set(c[0]).at[1].set(c_ctx)
    mod = _modulation(cc, w_ada[0], b_ada[0])
    sh1, sc1, g1, sh2, sc2, g2 = [mod[0:1, k * d:(k + 1) * d] for k in range(6)]
    csh1, csc1 = mod[1:2, 0:d], mod[1:2, d:2 * d]

    ssm_vecs = (dtb, alog, amask)
    zs = jnp.zeros((SSM_GROUPS, SSM_STATE, SSM_GROUP_W), F32)
    zh = jnp.zeros((HGRN_HEADS, HGRN_V, HGRN_K), F32)
    nw1 = row(norm1_w[0])

    _, ssd_c, hgrn_c = _mixer_scans(ctx[0], nw1, csh1, csc1, w_main, w_dt, conv_w[0], conv_b[0], ssm_vecs,
                                    rep, lb, ctx_len, (zs, zs), (zh, zh))
    ssd_init = (ssd_c[0][1], ssd_c[1][1])
    hgrn_init = (hgrn_c[0][1], hgrn_c[1][1])

    x2 = x[0]
    proj, ssd_l, hgrn_l = _mixer_scans(x2, nw1, sh1, sc1, w_main, w_dt, conv_w[0], conv_b[0], ssm_vecs,
                                       rep, lb, GRID_W, ssd_init, hgrn_init, ssd_post=(dsk, row(ssm_norm_w[0])))
    y_ssm = ssd_l[1][0]
    y_hgrn = _hgrn_post(hgrn_l[0][0], hgrn_l[1][0], proj, row(hgrn_norm_w[0]))
    y = _merge(y_ssm, y_hgrn, proj, wbs, wbh)
    h, hn2, logits = _outproj(y, x2, wo, g1, row(norm2_w[0]), sh2, sc2, wr, br)

    idx, pos, p, counts = _route(logits)
    counts = counts[0, :N_EXPERTS].astype(I32)
    bm = MOE_BM
    n_assign = seq * TOP_K
    n_blocks = -(-(n_assign + N_EXPERTS * (bm - 1)) // bm)
    n_slots = n_blocks * bm
    padded = (counts + bm - 1) // bm * bm
    pad_end = jnp.cumsum(padded)
    pad_start = pad_end - padded
    slots = (pad_start[idx[:, :TOP_K]] + pos[:, :TOP_K]).reshape(-1).astype(I32)
    blk_start = jnp.arange(n_blocks, dtype=I32) * bm
    blk_e = jnp.minimum(jnp.sum(blk_start[:, None] >= pad_end[None, :], axis=1), N_EXPERTS - 1).astype(I32)
    blk_count = jnp.clip(pad_start[blk_e] + counts[blk_e] - blk_start, 0, bm).astype(I32)
    half_start = jnp.arange(n_slots // MOE_HALF, dtype=I32) * MOE_HALF
    half_e = jnp.repeat(blk_e, bm // MOE_HALF)
    tails = (pad_start[half_e] + counts[half_e] - half_start < MOE_HALF).astype(I32)

    assert d == 2 * HALF_D
    xs_slab = _dispatch(slots, tails, hn2, n_slots)
    yb_slab = _experts(blk_e, blk_count, xs_slab, w_gate_up[0], b_gate_up[0], w_down[0], b_down[0])
    out = _combine(slots, h, p, g2, row(final_norm_w), yb_slab)
    return out.reshape(bsz, seq, d)
```

```python
import functools

import jax
import jax.numpy as jnp
from jax import lax
from jax.experimental import pallas as pl
from jax.experimental.pallas import tpu as pltpu

F32 = jnp.float32
BF16 = jnp.bfloat16
I32 = jnp.int32
U32 = jnp.uint32
HI = lax.Precision.HIGHEST

EPS = 1e-6
GRID_W = 64

LANES = 128
NSL = 8
HALF_D = NSL * LANES
SSM_HEADS = 32
SSM_HEAD_DIM = 64
SSM_GROUPS = 4
SSM_STATE = 128
SSM_INNER = SSM_HEADS * SSM_HEAD_DIM
SSM_GROUP_W = SSM_INNER // SSM_GROUPS
SSM_CHUNK = 128
HGRN_HEADS = 16
HGRN_K = 128
HGRN_V = 128
HGRN_CHUNK = 64
HGRN_SUB = 16
HGRN_HEAD_GROUP = 4
HGRN_SAFE_DECAY = 60.0
N_EXPERTS = 32
TOP_K = 4
SWIGLU_LIMIT = 7.0
SWIGLU_ALPHA = 1.702
MOE_BM = 1024
MOE_HALF = 512
MOE_TF = 512
VMEM_LIMIT = 56 * 1024 * 1024

COL_Z, COL_Q, COL_F, COL_I, COL_G, COL_GS, COL_GH, COL_XS = 0, 1, 2, 4, 5, 6, 7, 8


def _sigmoid(x):
    return 1.0 / (1.0 + jnp.exp(-x))


def _cparams(sem):
    return pltpu.CompilerParams(dimension_semantics=sem, vmem_limit_bytes=VMEM_LIMIT)


def _mod_kernel(c_ref, w_ref, b_ref, o_ref):
    c = c_ref[...]
    s = c * _sigmoid(c)
    o_ref[...] = jnp.dot(s, w_ref[...], precision=HI, preferred_element_type=F32) + b_ref[...]


def _modulation(cc, w_ada, b_ada):
    d, n = w_ada.shape
    tn = 1024
    return pl.pallas_call(
        _mod_kernel,
        grid=(n // tn,),
        in_specs=[pl.BlockSpec((8, d), lambda j: (0, 0)),
                  pl.BlockSpec((d, tn), lambda j: (0, j)),
                  pl.BlockSpec((1, tn), lambda j: (0, j))],
        out_specs=pl.BlockSpec((8, tn), lambda j: (0, j)),
        out_shape=jax.ShapeDtypeStruct((8, n), F32),
        compiler_params=_cparams(("arbitrary",)),
        name="adaln_mod",
    )(cc, w_ada, b_ada.reshape(1, n))


def _inproj_kernel(x_ref, nw_ref, sh_ref, sc_ref, w_ref, wdt_ref, o_ref, odt_ref, hn_ref):
    @pl.when(pl.program_id(1) == 0)
    def _():
        x = x_ref[...]
        ms = jnp.mean(x * x, axis=-1, keepdims=True)
        y = x * lax.rsqrt(ms + EPS) * nw_ref[...]
        hn = (y * (1.0 + sc_ref[...]) + sh_ref[...]).astype(BF16)
        hn_ref[...] = hn
        odt_ref[...] = jnp.dot(hn, wdt_ref[...], preferred_element_type=F32)

    o_ref[...] = jnp.dot(hn_ref[...], w_ref[...], preferred_element_type=F32)


def _inproj(x2, nw, sh, sc, w_main, w_dt):
    L, d = x2.shape
    n = w_main.shape[1]
    tm = min(1024, L)
    tn = 1024
    return pl.pallas_call(
        _inproj_kernel,
        grid=(L // tm, n // tn),
        in_specs=[pl.BlockSpec((tm, d), lambda i, j: (i, 0)),
                  pl.BlockSpec((1, d), lambda i, j: (0, 0)),
                  pl.BlockSpec((1, d), lambda i, j: (0, 0)),
                  pl.BlockSpec((1, d), lambda i, j: (0, 0)),
                  pl.BlockSpec((d, tn), lambda i, j: (0, j)),
                  pl.BlockSpec((d, LANES), lambda i, j: (0, 0))],
        out_specs=[pl.BlockSpec((tm, tn), lambda i, j: (i, j)),
                   pl.BlockSpec((tm, LANES), lambda i, j: (i, 0))],
        out_shape=[jax.ShapeDtypeStruct((L, n), F32), jax.ShapeDtypeStruct((L, LANES), F32)],
        scratch_shapes=[pltpu.VMEM((tm, d), BF16)],
        compiler_params=_cparams(("parallel", "arbitrary")),
        name="inproj",
    )(x2, nw, sh, sc, w_main, w_dt)


def _conv_body(u, w_ref, b_ref, row_len):
    tb = u.shape[0]
    pos = lax.broadcasted_iota(I32, u.shape, 0) & (row_len - 1)
    acc = u * w_ref[2:3, :] + b_ref[...]
    for k, off in ((0, -2), (1, -1), (3, 1)):
        shifted = pltpu.roll(u, shift=(-off) % tb, axis=0)
        valid = (pos + off >= 0) & (pos + off < row_len)
        acc = acc + jnp.where(valid, shifted, 0.0) * w_ref[k:k + 1, :]
    return acc * _sigmoid(acc)


def _conv_kernel(ux_ref, ub_ref, wx_ref, wb_ref, bx_ref, bb_ref, ox_ref, ob_ref, *, row_len):
    ox_ref[...] = _conv_body(ux_ref[...], wx_ref, bx_ref, row_len)
    ob_ref[...] = _conv_body(ub_ref[...], wb_ref, bb_ref, row_len)


def _conv_silu(proj, conv_w, conv_b, row_len):
    L = proj.shape[0]
    d = SSM_INNER
    nbc = 2 * SSM_GROUPS * SSM_STATE
    tb = max(row_len, min(512, L))
    wx, wb = conv_w[:, :d], conv_w[:, d:]
    bx, bb = conv_b[:d].reshape(1, d), conv_b[d:].reshape(1, nbc)
    return pl.pallas_call(
        functools.partial(_conv_kernel, row_len=row_len),
        grid=(L // tb,),
        in_specs=[pl.BlockSpec((tb, d), lambda i: (i, COL_XS)),
                  pl.BlockSpec((tb, nbc), lambda i: (i, COL_XS * 2 + 2)),
                  pl.BlockSpec((4, d), lambda i: (0, 0)),
                  pl.BlockSpec((4, nbc), lambda i: (0, 0)),
                  pl.BlockSpec((1, d), lambda i: (0, 0)),
                  pl.BlockSpec((1, nbc), lambda i: (0, 0))],
        out_specs=[pl.BlockSpec((tb, d), lambda i: (i, 0)),
                   pl.BlockSpec((tb, nbc), lambda i: (i, 0))],
        out_shape=[jax.ShapeDtypeStruct((L, d), F32), jax.ShapeDtypeStruct((L, nbc), F32)],
        compiler_params=_cparams(("parallel",)),
        name="conv_silu",
    )(proj, proj, wx, wb, bx, bb)


def _ssd_kernel(xs_ref, bc_ref, dt_ref, dtb_ref, alog_ref, amask_ref, rep_ref, s0_ref, *rest, rev, lane0, post):
    if post:
        yf_ref, z_ref, dsk_ref, nw_ref, y_ref, sfin_ref, S_ref = rest
    else:
        y_ref, sfin_ref, S_ref = rest
    n = pl.program_id(0)

    @pl.when(n == 0)
    def _():
        S_ref[...] = s0_ref[...]

    Q = xs_ref.shape[0]
    N = SSM_STATE
    GW = SSM_GROUP_W
    ti = lax.broadcasted_iota(I32, (Q, Q), 0)
    si = lax.broadcasted_iota(I32, (Q, Q), 1)
    allowed = (si >= ti) if rev else (si <= ti)
    tri = allowed.astype(F32)

    raw = dt_ref[...] + dtb_ref[...]
    dt = jnp.maximum(raw, 0.0) + jnp.log1p(jnp.exp(-jnp.abs(raw)))
    loga = dt * (-jnp.exp(alog_ref[...]) * amask_ref[...])
    cs = jnp.dot(tri, loga, precision=HI, preferred_element_type=F32)
    csT = cs.T
    tot = cs[0:1, :] if rev else cs[Q - 1:Q, :]
    fac = jnp.concatenate([jnp.exp(cs), jnp.exp(tot - cs), dt], axis=0).astype(BF16)
    tot8 = jnp.broadcast_to(tot, (8, LANES))
    t_hi = tot8.astype(BF16)
    t_r = tot8 - t_hi.astype(F32)
    t_mid = t_r.astype(BF16)
    t_lo = (t_r - t_mid.astype(F32)).astype(BF16)
    tot3 = jnp.concatenate([t_hi, t_mid, t_lo], axis=0)
    lane = lax.broadcasted_iota(I32, (Q, LANES), 1)
    lo_mask = lane < SSM_HEAD_DIM

    for g in range(SSM_GROUPS):
        rep_g = rep_ref[:, g * GW:(g + 1) * GW]
        fx = jnp.dot(fac, rep_g, preferred_element_type=F32)
        e_in, e_out, dt_x = fx[0:Q], fx[Q:2 * Q], fx[2 * Q:3 * Q]
        tx = jnp.dot(tot3, rep_g, preferred_element_type=F32)
        dec = jnp.exp((tx[0:1] + tx[8:9]) + tx[16:17])
        xdt = xs_ref[:, g * GW:(g + 1) * GW] * dt_x
        xw = (xdt * e_out).astype(BF16)
        xdt_b = xdt.astype(BF16)
        Bg = bc_ref[:, g * N:(g + 1) * N].astype(BF16)
        Cg = bc_ref[:, (SSM_GROUPS + g) * N:(SSM_GROUPS + g + 1) * N].astype(BF16)
        cb = lax.dot_general(Cg, Bg, (((1,), (1,)), ((), ())), preferred_element_type=F32)
        Sg = S_ref[g]
        y_off = jnp.dot(Cg, Sg.astype(BF16), preferred_element_type=F32) * e_in
        upd = lax.dot_general(Bg, xw, (((0,), (0,)), ((), ())), preferred_element_type=F32)
        S_ref[g] = Sg * dec + upd
        pieces = []
        for jp in range(GW // LANES):
            h0 = g * (GW // SSM_HEAD_DIM) + 2 * jp
            ms = []
            for hh in (h0, h0 + 1):
                col = cs[:, lane0 + hh:lane0 + hh + 1]
                row = csT[lane0 + hh:lane0 + hh + 1, :]
                dec_ts = jnp.exp(jnp.where(allowed, col - row, -jnp.inf))
                ms.append((cb * dec_ts).astype(BF16))
            m2 = jnp.concatenate(ms, axis=1)
            x2 = xdt_b[:, jp * LANES:(jp + 1) * LANES]
            r2 = jnp.concatenate([jnp.where(lo_mask, x2, jnp.zeros_like(x2)),
                                  jnp.where(lo_mask, jnp.zeros_like(x2), x2)], axis=0)
            y_d = jnp.dot(m2, r2, preferred_element_type=F32)
            pieces.append(y_d + y_off[:, jp * LANES:(jp + 1) * LANES])
        gs = slice(g * GW, (g + 1) * GW)
        y_g = jnp.concatenate(pieces, axis=1)
        if post:
            z_g = z_ref[:, gs]
            yy = (yf_ref[:, gs] + y_g + dsk_ref[:, gs] * xs_ref[:, gs]) * (z_g * _sigmoid(z_g))
            ms = jnp.mean(yy * yy, axis=-1, keepdims=True)
            y_ref[:, gs] = (yy * lax.rsqrt(ms + EPS) * nw_ref[:, gs]).astype(y_ref.dtype)
        else:
            y_ref[:, gs] = y_g

    @pl.when(n == pl.num_programs(0) - 1)
    def _():
        sfin_ref[...] = S_ref[...]


def _ssd_scan(xs, bc, dt, dtb, alog, amask, rep, s0, rev, post_inputs=None):
    L, d = xs.shape
    Q = SSM_CHUNK
    nc = L // Q
    idx = (lambda n: (nc - 1 - n, 0)) if rev else (lambda n: (n, 0))
    const2 = lambda n: (0, 0)
    const3 = lambda n: (0, 0, 0)
    sshape = (SSM_GROUPS, SSM_STATE, SSM_GROUP_W)
    post = post_inputs is not None
    in_specs = [pl.BlockSpec((Q, d), idx),
                pl.BlockSpec((Q, bc.shape[1]), idx),
                pl.BlockSpec((Q, LANES), idx),
                pl.BlockSpec((1, LANES), const2),
                pl.BlockSpec((1, LANES), const2),
                pl.BlockSpec((1, LANES), const2),
                pl.BlockSpec((LANES, d), const2),
                pl.BlockSpec(sshape, const3)]
    args = [xs, bc, dt, dtb, alog, amask, rep, s0]
    if post:
        zcol = (lambda n: (nc - 1 - n, COL_Z)) if rev else (lambda n: (n, COL_Z))
        in_specs += [pl.BlockSpec((Q, d), idx), pl.BlockSpec((Q, d), zcol),
                     pl.BlockSpec((1, d), const2), pl.BlockSpec((1, d), const2)]
        args += list(post_inputs)
    return pl.pallas_call(
        functools.partial(_ssd_kernel, rev=rev, lane0=SSM_HEADS if rev else 0, post=post),
        grid=(nc,),
        in_specs=in_specs,
        out_specs=[pl.BlockSpec((Q, d), idx),
                   pl.BlockSpec(sshape, const3)],
        out_shape=[jax.ShapeDtypeStruct((L, d), BF16 if post else F32), jax.ShapeDtypeStruct(sshape, F32)],
        scratch_shapes=[pltpu.VMEM(sshape, F32)],
        compiler_params=_cparams(("arbitrary",)),
        name="ssd_bwd" if rev else "ssd_fwd",
    )(*args)


class _HgrnDir:
    def __init__(self, refs, rev):
        self.refs = refs
        self.rev = rev

    def build(self):
        (q_ref, f_ref, i_ref, lb_ref, o_ref, ST_ref, kk_ref, bcs_ref, lf_ref, v_ref, vT_ref), rev = self.refs, self.rev
        _hgrn_dir_body(self, q_ref, f_ref, i_ref, lb_ref, o_ref, ST_ref, kk_ref, bcs_ref, lf_ref, v_ref, vT_ref, rev)
        return self


def _hgrn_dir_body(self, q_ref, f_ref, i_ref, lb_ref, o_ref, ST_ref, kk_ref, bcs_ref, lf_ref, v_ref, vT_ref, rev):
    C = q_ref.shape[0]
    c = HGRN_SUB
    nsub = C // c
    K = HGRN_K
    d = HGRN_HEADS * K
    ti = lax.broadcasted_iota(I32, (C, C), 0)
    si = lax.broadcasted_iota(I32, (C, C), 1)
    allowed = (si >= ti) if rev else (si <= ti)
    bt, bs = ti // c, si // c
    dist = (bs - bt) if rev else (bt - bs)
    order = list(range(nsub - 1, -1, -1)) if rev else list(range(nsub))

    def gate(h):
        sl = slice(h * K, (h + 1) * K)
        lbr = lb_ref[:, sl]
        lmax = jnp.max(lbr, axis=0, keepdims=True)
        le = jnp.exp(lbr - lmax)
        lb = le[0:1, :] / jnp.sum(le, axis=0, keepdims=True)
        fr = f_ref[:, sl]
        e = jnp.exp(-jnp.abs(fr))
        rinv = 1.0 / (1.0 + e)
        pos = fr >= 0.0
        sg = jnp.where(pos, rinv, e * rinv)
        sgn = jnp.where(pos, e * rinv, rinv)
        logf = jnp.log(lb + (1.0 - lb) * sg)
        kk_ref[:, sl] = (1.0 - lb) * sgn
        iv = i_ref[:, sl]
        vv = iv * _sigmoid(iv)
        v_ref[:, sl] = vv
        vT_ref[sl, :] = vv.T.astype(BF16)
        hi = logf.astype(BF16)
        r1 = logf - hi.astype(F32)
        mid = r1.astype(BF16)
        lo = (r1 - mid.astype(F32)).astype(BF16)
        lf_ref[:, h * K:(h + 1) * K] = hi
        lf_ref[:, d + h * K:d + (h + 1) * K] = mid
        lf_ref[:, 2 * d + h * K:2 * d + (h + 1) * K] = lo

    def levels(col):
        r_in, r_out = {}, {}
        for p, sc_i in enumerate(order):
            last = sc_i * c + (0 if rev else c - 1)
            r_out[p] = bcs_ref[last:last + 1, col]
            r_in[p] = r_out[p - 1] if p > 0 else jnp.zeros_like(r_out[p])
        return r_in, r_out

    def cumulate():
        cs3 = jnp.dot(allowed.astype(BF16), lf_ref[...], preferred_element_type=F32)
        bcs_ref[...] = (cs3[:, :d] + cs3[:, d:2 * d]) + cs3[:, 2 * d:]
        r_in_all, r_out_all = levels(slice(0, d))
        worst = r_in_all[0] - r_out_all[0]
        for p in range(1, nsub):
            worst = jnp.maximum(worst, r_in_all[p] - r_out_all[p])
        return worst

    def stages(exact_pairwise):
        t16 = lax.broadcasted_iota(I32, (c, K), 0)
        ones_b = jnp.ones((K, K), BF16)
        t2 = lax.broadcasted_iota(I32, (C, 2 * C), 0)
        col2 = lax.broadcasted_iota(I32, (C, 2 * C), 1)
        s2 = col2 & (C - 1)
        dist2 = (s2 // c - t2 // c) if rev else (t2 // c - s2 // c)
        dist2 = jnp.where(col2 < C, dist2, -1)
        same_blk = (col2 >= C) & (t2 // c == s2 // c) & ((s2 >= t2) if rev else (s2 <= t2))
        band = {dd: dist2 == dd for dd in range(1, nsub)}
        def matmuls(h):
            sl = slice(h * K, (h + 1) * K)
            bcs = bcs_ref[:, sl]
            kk = kk_ref[:, sl]
            qq = q_ref[:, sl] * (K ** -0.5)
            tot = bcs[0:1, :] if rev else bcs[C - 1:C, :]

            r_in, r_out = levels(sl)

            def by_sub_chunk(fn):
                rows = [None] * nsub
                for p, sc_i in enumerate(order):
                    rows[sc_i] = jnp.broadcast_to(fn(p), (c, K))
                return jnp.concatenate(rows, axis=0)

            R_in = by_sub_chunk(lambda p: r_in[p])
            R_out = by_sub_chunk(lambda p: r_out[p])
            qh = qq * jnp.exp(bcs - R_in)
            kt = kk * jnp.exp(R_out - bcs)
            ST = ST_ref[h]
            if exact_pairwise:
                q_in = (qq * jnp.exp(bcs)).astype(BF16)
                k_end = (kk * jnp.exp(tot - bcs)).astype(BF16)
            else:
                q_in = (qh * by_sub_chunk(lambda p: jnp.exp(r_in[p]))).astype(BF16)
                k_end = (kt * by_sub_chunk(lambda p: jnp.exp(tot - r_out[p]))).astype(BF16)
            o = lax.dot_general(q_in, ST.astype(BF16), (((1,), (1,)), ((), ())), preferred_element_type=F32)
            upd = jnp.dot(vT_ref[sl, :], k_end, preferred_element_type=F32)
            new_state = ST * jnp.exp(tot) + upd
            lhs = [qh]
            for dd in range(2, nsub):
                rows_x = [None] * nsub
                for p, sc_i in enumerate(order):
                    if p >= dd:
                        rows_x[sc_i] = jnp.broadcast_to(jnp.exp(r_in[p] - r_out[p - dd]), (c, K))
                    else:
                        rows_x[sc_i] = jnp.zeros((c, K), F32)
                lhs.append(qh * jnp.concatenate(rows_x, axis=0))
            lhs = jnp.concatenate(lhs, axis=0).astype(BF16)
            if exact_pairwise:
                rhs = kt.astype(BF16)
            else:
                k_diag = kt * by_sub_chunk(lambda p: jnp.exp(r_in[p] - r_out[p]))
                rhs = jnp.concatenate([kt, k_diag], axis=0).astype(BF16)
            a = lax.dot_general(lhs, rhs, (((1,), (1,)), ((), ())), preferred_element_type=F32)
            return o, a, new_state

        def finish(h, o, a, new_state):
            sl = slice(h * K, (h + 1) * K)
            ST_ref[h] = new_state
            vv = v_ref[:, sl]
            vb = vv.astype(BF16)
            if exact_pairwise:
                bcs = bcs_ref[:, sl]
                kk = kk_ref[:, sl]
                qq = q_ref[:, sl] * (K ** -0.5)
                att = jnp.where(dist == 1, a[0:C], 0.0)
                for dd in range(2, nsub):
                    att = att + jnp.where(dist == dd, a[(dd - 1) * C:dd * C], 0.0)
                o = o + jnp.dot(att.astype(BF16), vb, preferred_element_type=F32)
                for sc_i in range(nsub):
                    r0 = sc_i * c
                    b_i = bcs[r0:r0 + c, :]
                    q_i = qq[r0:r0 + c, :]
                    ws = []
                    for s in range(c):
                        ok = (t16 <= s) if rev else (t16 >= s)
                        ex = jnp.exp(jnp.where(ok, b_i - bcs[r0 + s:r0 + s + 1, :], -jnp.inf))
                        ws.append((q_i * ex * kk[r0 + s:r0 + s + 1, :]).astype(BF16))
                    w_all = jnp.concatenate(ws, axis=0)
                    a_all = jnp.dot(w_all, ones_b, preferred_element_type=F32)
                    o_d = jnp.zeros((c, HGRN_V), F32)
                    for s in range(c):
                        o_d = o_d + a_all[s * c:(s + 1) * c, :] * vv[r0 + s:r0 + s + 1, :]
                    o_ref[r0:r0 + c, sl] = o[r0:r0 + c, :] + o_d
            else:
                att = jnp.zeros((C, 2 * C), F32)
                for dd in range(nsub - 1, 1, -1):
                    att = jnp.where(band[dd], a[(dd - 1) * C:dd * C], att)
                att = jnp.where(band[1] | same_blk, a[0:C], att)
                v2 = jnp.concatenate([vb, vb], axis=0)
                o_ref[:, sl] = o + jnp.dot(att.astype(BF16), v2, preferred_element_type=F32)

        return matmuls, finish

    self.gate, self.cumulate, self.stages = gate, cumulate, stages


def _hgrn_kernel(qf_ref, ff_ref, if_ref, qb_ref, fb_ref, ib_ref, lbf_ref, lbb_ref, s0f_ref, s0b_ref,
                 of_ref, ob_ref, sff_ref, sfb_ref, *scratch):
    n = pl.program_id(0)
    nscr = len(scratch) // 2
    scr_f, scr_b = scratch[:nscr], scratch[nscr:]

    @pl.when(n == 0)
    def _():
        scr_f[0][...] = s0f_ref[...]
        scr_b[0][...] = s0b_ref[...]

    dirs = [_HgrnDir((qf_ref, ff_ref, if_ref, lbf_ref, of_ref) + tuple(scr_f), False).build(),
            _HgrnDir((qb_ref, fb_ref, ib_ref, lbb_ref, ob_ref) + tuple(scr_b), True).build()]
    for h in range(HGRN_HEADS):
        for dr in dirs:
            dr.gate(h)
    worst = jnp.maximum(dirs[0].cumulate(), dirs[1].cumulate())
    safe = jnp.max(worst) < HGRN_SAFE_DECAY

    def heads(exact_pairwise):
        fns = [dr.stages(exact_pairwise) for dr in dirs]
        group = 1 if exact_pairwise else HGRN_HEAD_GROUP
        for h0 in range(0, HGRN_HEADS, group):
            staged = [(fin, h, mm(h)) for h in range(h0, h0 + group) for mm, fin in fns]
            for fin, h, vals in staged:
                fin(h, *vals)

    @pl.when(safe)
    def _():
        heads(False)

    @pl.when(jnp.logical_not(safe))
    def _():
        heads(True)

    @pl.when(n == pl.num_programs(0) - 1)
    def _():
        sff_ref[...] = scr_f[0][...]
        sfb_ref[...] = scr_b[0][...]


def _hgrn_scan(proj, lb, s0):
    L = proj.shape[0]
    d = HGRN_HEADS * HGRN_K
    C = HGRN_CHUNK
    nc = L // C
    sshape = (HGRN_HEADS, HGRN_V, HGRN_K)
    const3 = lambda n: (0, 0, 0)
    fwd = lambda col: pl.BlockSpec((C, d), lambda n: (n, col))
    bwd = lambda col: pl.BlockSpec((C, d), lambda n: (nc - 1 - n, col))
    lb_spec = pl.BlockSpec(lb[0].shape, lambda n: (0, 0))
    st_spec = pl.BlockSpec(sshape, const3)
    scratch = [pltpu.VMEM(sshape, F32), pltpu.VMEM((C, d), F32), pltpu.VMEM((C, d), F32),
               pltpu.VMEM((C, 3 * d), BF16), pltpu.VMEM((C, d), F32), pltpu.VMEM((d, C), BF16)]
    return pl.pallas_call(
        _hgrn_kernel,
        grid=(nc,),
        in_specs=[fwd(COL_Q), fwd(COL_F), fwd(COL_I), bwd(COL_Q), bwd(COL_F + 1), bwd(COL_I),
                  lb_spec, lb_spec, st_spec, st_spec],
        out_specs=[fwd(0), bwd(0), st_spec, st_spec],
        out_shape=[jax.ShapeDtypeStruct((L, d), F32), jax.ShapeDtypeStruct((L, d), F32),
                   jax.ShapeDtypeStruct(sshape, F32), jax.ShapeDtypeStruct(sshape, F32)],
        scratch_shapes=scratch + scratch,
        compiler_params=_cparams(("arbitrary",)),
        name="hgrn",
    )(proj, proj, proj, proj, proj, proj, lb[0], lb[1], s0[0], s0[1])


def _hgrn_post_kernel(of_ref, ob_ref, g_ref, nw_ref, o_ref):
    nw = nw_ref[...]
    for h in range(HGRN_HEADS):
        sl = slice(h * HGRN_V, (h + 1) * HGRN_V)
        o = of_ref[:, sl] + ob_ref[:, sl]
        ms = jnp.mean(o * o, axis=-1, keepdims=True)
        g = g_ref[:, sl]
        o_ref[:, sl] = (o * lax.rsqrt(ms + EPS) * nw * (g * _sigmoid(g))).astype(o_ref.dtype)


def _hgrn_post(of, ob, proj, nw):
    L, d = of.shape
    tm = min(512, L)
    row = lambda i: (i, 0)
    return pl.pallas_call(
        _hgrn_post_kernel,
        grid=(L // tm,),
        in_specs=[pl.BlockSpec((tm, d), row), pl.BlockSpec((tm, d), row),
                  pl.BlockSpec((tm, d), lambda i: (i, COL_G)),
                  pl.BlockSpec((1, HGRN_V), lambda i: (0, 0))],
        out_specs=pl.BlockSpec((tm, d), row),
        out_shape=jax.ShapeDtypeStruct((L, d), BF16),
        compiler_params=_cparams(("parallel",)),
        name="hgrn_post",
    )(of, ob, proj, nw)


def _merge_kernel(ys_ref, yh_ref, gs_ref, gh_ref, wbs_ref, wbh_ref, o_ref):
    a = jnp.dot(ys_ref[...], wbs_ref[...], preferred_element_type=F32)
    b = jnp.dot(yh_ref[...], wbh_ref[...], preferred_element_type=F32)
    o_ref[...] = (_sigmoid(gs_ref[...]) * a + _sigmoid(gh_ref[...]) * b).astype(o_ref.dtype)


def _merge(ys, yh, proj, wbs, wbh):
    L, d = ys.shape
    tm = min(1024, L)
    tn = 512
    nj = d // tn
    return pl.pallas_call(
        _merge_kernel,
        grid=(L // tm, nj),
        in_specs=[pl.BlockSpec((tm, d), lambda i, j: (i, 0)),
                  pl.BlockSpec((tm, d), lambda i, j: (i, 0)),
                  pl.BlockSpec((tm, tn), lambda i, j: (i, COL_GS * nj + j)),
                  pl.BlockSpec((tm, tn), lambda i, j: (i, COL_GH * nj + j)),
                  pl.BlockSpec((d, tn), lambda i, j: (0, j)),
                  pl.BlockSpec((d, tn), lambda i, j: (0, j))],
        out_specs=pl.BlockSpec((tm, tn), lambda i, j: (i, j)),
        out_shape=jax.ShapeDtypeStruct((L, d), BF16),
        compiler_params=_cparams(("parallel", "arbitrary")),
        name="merge",
    )(ys, yh, proj, proj, wbs, wbh)


def _outproj_kernel(y_ref, x_ref, wo_ref, g1_ref, nw_ref, sh_ref, sc_ref, wr_ref, br_ref,
                    h_ref, hn_ref, lg_ref):
    h = x_ref[...] + g1_ref[...] * jnp.dot(y_ref[...], wo_ref[...], preferred_element_type=F32)
    h_ref[...] = h
    ms = jnp.mean(h * h, axis=-1, keepdims=True)
    hn = h * lax.rsqrt(ms + EPS) * nw_ref[...] * (1.0 + sc_ref[...]) + sh_ref[...]
    h_hi = hn.astype(BF16)
    h_lo = (hn - h_hi.astype(F32)).astype(BF16)
    wr = wr_ref[...]
    w_hi = wr.astype(BF16)
    w_lo = (wr - w_hi.astype(F32)).astype(BF16)
    lg = jnp.dot(h_hi, w_hi, preferred_element_type=F32)
    lg = lg + (jnp.dot(h_hi, w_lo, preferred_element_type=F32) + jnp.dot(h_lo, w_hi, preferred_element_type=F32))
    lg_ref[...] = lg + br_ref[...]
    _store_packed_slabs(hn_ref, hn)


def _outproj(y, x2, wo, g1, nw, sh, sc, wr, br):
    L, d = x2.shape
    tm = min(256, L)
    row = lambda i: (i, 0)
    const = lambda i: (0, 0)
    vec = pl.BlockSpec((1, d), const)
    return pl.pallas_call(
        _outproj_kernel,
        grid=(L // tm,),
        in_specs=[pl.BlockSpec((tm, d), row), pl.BlockSpec((tm, d), row),
                  pl.BlockSpec((d, d), const), vec, vec, vec, vec,
                  pl.BlockSpec((d, LANES), const), pl.BlockSpec((1, LANES), const)],
        out_specs=[pl.BlockSpec((tm, d), row), pl.BlockSpec((tm * NSL, LANES), row), pl.BlockSpec((tm, LANES), row)],
        out_shape=[jax.ShapeDtypeStruct((L, d), F32), jax.ShapeDtypeStruct((L * NSL, LANES), U32),
                   jax.ShapeDtypeStruct((L, LANES), F32)],
        compiler_params=_cparams(("parallel",)),
        name="outproj",
    )(y, x2, wo, g1, nw, sh, sc, wr, br)


def _route_kernel(lg_ref, idx_ref, pos_ref, p_ref, cnt_ref, carry_ref):
    n = pl.program_id(0)

    @pl.when(n == 0)
    def _():
        carry_ref[...] = jnp.zeros_like(carry_ref)

    tm = lg_ref.shape[0]
    lane_i = lax.broadcasted_iota(I32, (tm, LANES), 1)
    lane = lane_i.astype(F32)
    neg = jnp.float32(-jnp.inf)
    l = jnp.where(lane_i < N_EXPERTS, lg_ref[...], neg)
    vals, sels, ids = [], [], []
    for _ in range(TOP_K):
        m = jnp.max(l, axis=-1, keepdims=True)
        i_k = jnp.min(jnp.where(l == m, lane, float(LANES)), axis=-1, keepdims=True)
        sel = lane == i_k
        l = jnp.where(sel, neg, l)
        vals.append(m)
        sels.append(sel)
        ids.append(i_k)
    es = [jnp.exp(v - vals[0]) for v in vals]
    den = es[0]
    for e in es[1:]:
        den = den + e
    onehot = jnp.zeros((tm, LANES), F32)
    for sel in sels:
        onehot = onehot + jnp.where(sel, 1.0, 0.0)
    ti = lax.broadcasted_iota(I32, (tm, tm), 0)
    si = lax.broadcasted_iota(I32, (tm, tm), 1)
    strict = jnp.where(si < ti, 1.0, 0.0).astype(BF16)
    before = jnp.dot(strict, onehot.astype(BF16), preferred_element_type=F32) + carry_ref[...]
    idx_out = jnp.zeros((tm, LANES), F32)
    pos_out = jnp.zeros((tm, LANES), F32)
    p_out = jnp.zeros((tm, LANES), F32)
    for k in range(TOP_K):
        pos_k = jnp.sum(jnp.where(sels[k], before, 0.0), axis=-1, keepdims=True)
        idx_out = jnp.where(lane_i == k, ids[k], idx_out)
        pos_out = jnp.where(lane_i == k, pos_k, pos_out)
        p_out = jnp.where(lane_i == k, es[k] / den, p_out)
    idx_ref[...] = idx_out.astype(I32)
    pos_ref[...] = pos_out.astype(I32)
    p_ref[...] = p_out
    carry_ref[...] = carry_ref[...] + jnp.sum(onehot, axis=0, keepdims=True)
    cnt_ref[...] = carry_ref[...]


def _route(logits):
    L = logits.shape[0]
    tm = min(256, L)
    row = lambda i: (i, 0)
    return pl.pallas_call(
        _route_kernel,
        grid=(L // tm,),
        in_specs=[pl.BlockSpec((tm, LANES), row)],
        out_specs=[pl.BlockSpec((tm, LANES), row), pl.BlockSpec((tm, LANES), row),
                   pl.BlockSpec((tm, LANES), row), pl.BlockSpec((1, LANES), lambda i: (0, 0))],
        out_shape=[jax.ShapeDtypeStruct((L, LANES), I32), jax.ShapeDtypeStruct((L, LANES), I32),
                   jax.ShapeDtypeStruct((L, LANES), F32), jax.ShapeDtypeStruct((1, LANES), F32)],
        scratch_shapes=[pltpu.VMEM((1, LANES), F32)],
        compiler_params=_cparams(("arbitrary",)),
        name="route",
    )(logits)


def _slab_rows(r):
    return pl.ds(pl.multiple_of(r * NSL, NSL), NSL)


def _pack_pair(lo, hi):
    def bf16_bits(v):
        b = lax.bitcast_convert_type(v, U32)
        return (b + jnp.uint32(0x7FFF) + ((b >> 16) & jnp.uint32(1))) >> 16
    return (bf16_bits(hi) << 16) | bf16_bits(lo)


def _unpack_pair(w):
    lo = lax.bitcast_convert_type(w << 16, F32)
    hi = lax.bitcast_convert_type(w & jnp.uint32(0xFFFF0000), F32)
    return lo, hi


def _store_packed_slabs(ref, val, row0=0):
    m = val.shape[0]
    for s in range(NSL):
        lo = val[:, s * LANES:(s + 1) * LANES]
        hi = val[:, HALF_D + s * LANES:HALF_D + (s + 1) * LANES]
        ref[pl.ds(row0 + s, m, stride=NSL), :] = _pack_pair(lo, hi)


def _dispatch_kernel(slot_ref, tail_ref, x_ref, o_ref, zero_ref, sem):
    tm = x_ref.shape[0] // NSL
    base = pl.program_id(0) * tm

    @pl.when(pl.program_id(0) == 0)
    def _():
        zero_ref[...] = jnp.zeros_like(zero_ref)
        n_half = o_ref.shape[0] // (MOE_HALF * NSL)

        def fill_copy(hb):
            start = pl.multiple_of(hb * (MOE_HALF * NSL), MOE_HALF * NSL)
            return pltpu.make_async_copy(zero_ref, o_ref.at[pl.ds(start, MOE_HALF * NSL)], sem)

        def start_fill(hb, carry):
            @pl.when(tail_ref[hb] > 0)
            def _():
                fill_copy(hb).start()
            return carry

        def wait_fill(hb, carry):
            @pl.when(tail_ref[hb] > 0)
            def _():
                fill_copy(hb).wait()
            return carry

        lax.fori_loop(0, n_half, start_fill, 0)
        lax.fori_loop(0, n_half, wait_fill, 0)

    def row_copy(t, s):
        return pltpu.make_async_copy(x_ref.at[_slab_rows(t)], o_ref.at[_slab_rows(s)], sem)

    def issue(t, carry):
        for k in range(TOP_K):
            row_copy(t, slot_ref[(base + t) * TOP_K + k]).start()
        return carry

    lax.fori_loop(0, tm, issue, 0)

    def drain(t, carry):
        for k in range(TOP_K):
            row_copy(t, slot_ref[(base + t) * TOP_K + k]).wait()
        return carry

    lax.fori_loop(0, tm, drain, 0)


def _dispatch(slots, tails, x_slab, n_slots):
    L = x_slab.shape[0] // NSL
    tm = min(256, L)
    grid_spec = pltpu.PrefetchScalarGridSpec(
        num_scalar_prefetch=2,
        grid=(L // tm,),
        in_specs=[pl.BlockSpec((tm * NSL, LANES), lambda i, sl, tl: (i, 0))],
        out_specs=pl.BlockSpec(memory_space=pl.ANY),
        scratch_shapes=[pltpu.VMEM((MOE_HALF * NSL, LANES), x_slab.dtype), pltpu.SemaphoreType.DMA(())],
    )
    return pl.pallas_call(
        _dispatch_kernel,
        grid_spec=grid_spec,
        out_shape=jax.ShapeDtypeStruct((n_slots * NSL, LANES), x_slab.dtype),
        compiler_params=_cparams(("arbitrary",)),
        name="dispatch",
    )(slots, tails, x_slab)


def _expert_kernel(be_ref, bc_ref, bo_ref, x_ref, wg_ref, wu_ref, bg_ref, bu_ref, wd_ref, bd_ref, o_ref,
                   xb_ref, acc_ref):
    i = pl.program_id(0)
    j = pl.program_id(1)
    nj = pl.num_programs(1)
    hm = MOE_HALF
    count = bc_ref[i]

    for half in range(MOE_BM // hm):
        rows = slice(half * hm, (half + 1) * hm)
        slab_rows = slice(half * hm * NSL, (half + 1) * hm * NSL)
        live = count > half * hm

        @pl.when(live & (j == 0))
        def _():
            for s in range(NSL):
                lo, hi = _unpack_pair(x_ref[pl.ds(half * hm * NSL + s, hm, stride=NSL), :])
                xb_ref[rows, s * LANES:(s + 1) * LANES] = lo.astype(BF16)
                xb_ref[rows, HALF_D + s * LANES:HALF_D + (s + 1) * LANES] = hi.astype(BF16)
            acc_ref[rows, :] = jnp.zeros((hm, acc_ref.shape[1]), F32)

        @pl.when(live)
        def _():
            xb = xb_ref[rows, :]
            gate = jnp.dot(xb, wg_ref[0].astype(BF16), preferred_element_type=F32) + bg_ref[0]
            up = jnp.dot(xb, wu_ref[0].astype(BF16), preferred_element_type=F32) + bu_ref[0]
            gate = jnp.minimum(gate, SWIGLU_LIMIT)
            up = jnp.clip(up, -SWIGLU_LIMIT, SWIGLU_LIMIT)
            glu = gate * _sigmoid(SWIGLU_ALPHA * gate)
            act = ((up + 1.0) * glu).astype(BF16)
            acc_ref[rows, :] += jnp.dot(act, wd_ref[0].astype(BF16), preferred_element_type=F32)

        @pl.when(live & (j == nj - 1))
        def _():
            _store_packed_slabs(o_ref, acc_ref[rows, :] + bd_ref[0], row0=half * hm * NSL)

        @pl.when(jnp.logical_not(live) & (j == nj - 1))
        def _():
            o_ref[slab_rows, :] = jnp.zeros((hm * NSL, LANES), o_ref.dtype)


def _experts(blk_e, blk_count, xs_slab, w_gu, b_gu, w_dn, b_dn):
    n_slots = xs_slab.shape[0] // NSL
    ne, d, two_ff = w_gu.shape
    dff = two_ff // 2
    bm, tf = MOE_BM, MOE_TF
    nj = dff // tf
    nb = n_slots // bm

    valid = blk_count > 0
    last_parity = (jnp.sum(valid.astype(I32)) - 1) % 2
    blk_order = jnp.where(valid, jnp.arange(nb, dtype=I32) % 2, 2 + last_parity).astype(I32)

    def jeff(j, bo, i):
        code = bo[i]
        return jnp.where(code == 0, j, jnp.where(code == 1, nj - 1 - j, jnp.where(code == 2, nj - 1, 0)))

    grid_spec = pltpu.PrefetchScalarGridSpec(
        num_scalar_prefetch=3,
        grid=(nb, nj),
        in_specs=[pl.BlockSpec((bm * NSL, LANES), lambda i, j, be, bv, bo: (jnp.where(bv[i] > 0, i, 0), 0)),
                  pl.BlockSpec((1, d, tf), lambda i, j, be, bv, bo: (be[i], 0, jeff(j, bo, i))),
                  pl.BlockSpec((1, d, tf), lambda i, j, be, bv, bo: (be[i], 0, nj + jeff(j, bo, i))),
                  pl.BlockSpec((1, 1, tf), lambda i, j, be, bv, bo: (be[i], 0, jeff(j, bo, i))),
                  pl.BlockSpec((1, 1, tf), lambda i, j, be, bv, bo: (be[i], 0, nj + jeff(j, bo, i))),
                  pl.BlockSpec((1, tf, d), lambda i, j, be, bv, bo: (be[i], jeff(j, bo, i), 0)),
                  pl.BlockSpec((1, 1, d), lambda i, j, be, bv, bo: (be[i], 0, 0))],
        out_specs=pl.BlockSpec((bm * NSL, LANES), lambda i, j, be, bv, bo: (i, 0)),
        scratch_shapes=[pltpu.VMEM((bm, d), BF16), pltpu.VMEM((bm, d), F32)],
    )
    return pl.pallas_call(
        _expert_kernel,
        grid_spec=grid_spec,
        out_shape=jax.ShapeDtypeStruct((n_slots * NSL, LANES), U32),
        compiler_params=_cparams(("arbitrary", "arbitrary")),
        name="experts",
    )(blk_e, blk_count, blk_order, xs_slab, w_gu, w_gu, b_gu.reshape(ne, 1, two_ff),
      b_gu.reshape(ne, 1, two_ff), w_dn, b_dn.reshape(ne, 1, d))


def _combine_kernel(slot_ref, h_ref, p_ref, g2_ref, fw_ref, yb_ref, o_ref, *scratch):
    buf_sets = (scratch[0:TOP_K], scratch[TOP_K:2 * TOP_K])
    sems = scratch[2 * TOP_K]
    tm = h_ref.shape[0]
    i = pl.program_id(0)

    def gather(tile, which, wait):
        bufs, sem = buf_sets[which], sems.at[which]

        def body(t, carry):
            for k in range(TOP_K):
                s = slot_ref[(tile * tm + t) * TOP_K + k]
                cp = pltpu.make_async_copy(yb_ref.at[_slab_rows(s)], bufs[k].at[_slab_rows(t)], sem)
                if wait:
                    cp.wait()
                else:
                    cp.start()
            return carry

        lax.fori_loop(0, tm, body, 0)

    for which in (0, 1):
        @pl.when(lax.rem(i, 2) == which)
        def _():
            if which == 0:
                @pl.when(i == 0)
                def _():
                    gather(i, 0, wait=False)

            @pl.when(i + 1 < pl.num_programs(0))
            def _():
                gather(i + 1, 1 - which, wait=False)

            gather(i, which, wait=True)
            _combine_reduce(buf_sets[which], h_ref, p_ref, g2_ref, fw_ref, o_ref)


def _combine_reduce(gbufs, h_ref, p_ref, g2_ref, fw_ref, o_ref):
    tm = h_ref.shape[0]
    p = p_ref[...]
    ssq = jnp.zeros((tm, 1), F32)
    for s in range(NSL):
        moe_lo = jnp.zeros((tm, LANES), F32)
        moe_hi = jnp.zeros((tm, LANES), F32)
        for k in range(TOP_K):
            lo, hi = _unpack_pair(gbufs[k][pl.ds(s, tm, stride=NSL), :])
            moe_lo = moe_lo + p[:, k:k + 1] * lo
            moe_hi = moe_hi + p[:, k:k + 1] * hi
        for moe, c0 in ((moe_lo, s * LANES), (moe_hi, HALF_D + s * LANES)):
            cs = slice(c0, c0 + LANES)
            hs = h_ref[:, cs] + g2_ref[:, cs] * moe
            o_ref[:, cs] = hs
            ssq = ssq + jnp.sum(hs * hs, axis=-1, keepdims=True)
    scale = lax.rsqrt(ssq / h_ref.shape[1] + EPS)
    o_ref[...] = o_ref[...] * scale * fw_ref[...]


def _combine(slots, h, p, g2, fw, yb_slab):
    L, d = h.shape
    tm = min(256, L)
    grid_spec = pltpu.PrefetchScalarGridSpec(
        num_scalar_prefetch=1,
        grid=(L // tm,),
        in_specs=[pl.BlockSpec((tm, d), lambda i, sl: (i, 0)),
                  pl.BlockSpec((tm, LANES), lambda i, sl: (i, 0)),
                  pl.BlockSpec((1, d), lambda i, sl: (0, 0)),
                  pl.BlockSpec((1, d), lambda i, sl: (0, 0)),
                  pl.BlockSpec(memory_space=pl.ANY)],
        out_specs=pl.BlockSpec((tm, d), lambda i, sl: (i, 0)),
        scratch_shapes=[pltpu.VMEM((tm * NSL, LANES), U32) for _ in range(2 * TOP_K)]
        + [pltpu.SemaphoreType.DMA((2,))],
    )
    return pl.pallas_call(
        _combine_kernel,
        grid_spec=grid_spec,
        out_shape=jax.ShapeDtypeStruct((L, d), F32),
        compiler_params=_cparams(("arbitrary",)),
        name="combine",
    )(slots, h, p, g2, fw, yb_slab)


def _mixer_scans(x2, nw1, sh1, sc1, w_main, w_dt, conv_w, conv_b, ssm_vecs, rep, lb, row_len, ssd_init, hgrn_init,
                 ssd_post=None):
    proj, dtr = _inproj(x2, nw1, sh1, sc1, w_main, w_dt)
    xs, bc = _conv_silu(proj, conv_w, conv_b, row_len)
    dtb, alog, amask = ssm_vecs
    ssd_f = _ssd_scan(xs, bc, dtr, dtb, alog, amask[0], rep[0], ssd_init[0], rev=False)
    post_inputs = None if ssd_post is None else (ssd_f[0], proj) + tuple(ssd_post)
    ssd_b = _ssd_scan(xs, bc, dtr, dtb, alog, amask[1], rep[1], ssd_init[1], rev=True, post_inputs=post_inputs)
    o_f, o_b, st_f, st_b = _hgrn_scan(proj, lb, hgrn_init)
    return proj, [ssd_f, ssd_b], [(o_f, st_f), (o_b, st_b)]


def kernel(x, c, ctx, c_ctx, w_ada, b_ada, norm1_w, norm2_w, w_in, conv_w, conv_b, dt_bias, a_log, d_skip,
           ssm_norm_w, hgrn_lb, hgrn_norm_w, w_branch_ssm, w_branch_hgrn, w_out, w_router, b_router,
           w_gate_up, b_gate_up, w_down, b_down, final_norm_w):
    bsz, seq, d = x.shape
    assert bsz == 1 and w_ada.shape[0] == 1
    ctx_len = ctx.shape[1]
    H = SSM_HEADS

    w = w_in[0]
    o_z = SSM_INNER
    o_xbc = o_z + SSM_INNER + 2 * SSM_GROUPS * SSM_STATE
    o_dt = o_xbc + 2 * H
    w_main = jnp.concatenate([w[:, :o_z].astype(BF16), w[:, o_dt:].astype(BF16), w[:, o_z:o_xbc].astype(BF16)], axis=1)
    w_dt = jnp.pad(w[:, o_xbc:o_dt], ((0, 0), (0, LANES - 2 * H))).astype(BF16)
    dtb = jnp.pad(dt_bias[0].reshape(1, 2 * H), ((0, 0), (0, LANES - 2 * H)))
    alog = jnp.pad(a_log[0].reshape(1, 2 * H), ((0, 0), (0, LANES - 2 * H)))
    lane = jnp.arange(LANES)
    amask = [((lane >= dd * H) & (lane < (dd + 1) * H)).astype(F32).reshape(1, LANES) for dd in (0, 1)]
    chan_head = jnp.arange(SSM_INNER) // SSM_HEAD_DIM
    rep = [(lane[:, None] == (chan_head[None, :] + dd * H)).astype(BF16) for dd in (0, 1)]
    dsk = jnp.repeat(d_skip[0], SSM_HEAD_DIM).reshape(1, SSM_INNER)
    lb = [hgrn_lb[dd] for dd in (0, 1)]
    wbs = w_branch_ssm[0].astype(BF16)
    wbh = w_branch_hgrn[0].astype(BF16)
    wo = w_out[0].astype(BF16)
    wr = jnp.pad(w_router[0], ((0, 0), (0, LANES - N_EXPERTS)))
    br = jnp.pad(b_router[0].reshape(1, N_EXPERTS), ((0, 0), (0, LANES - N_EXPERTS)))
    row = lambda v: v.reshape(1, -1)

    cc = jnp.zeros((8, d), F32).at[0].set(c[0]).at[1].set(c_ctx)
    mod = _modulation(cc, w_ada[0], b_ada[0])
    sh1, sc1, g1, sh2, sc2, g2 = [mod[0:1, k * d:(k + 1) * d] for k in range(6)]
    csh1, csc1 = mod[1:2, 0:d], mod[1:2, d:2 * d]

    ssm_vecs = (dtb, alog, amask)
    zs = jnp.zeros((SSM_GROUPS, SSM_STATE, SSM_GROUP_W), F32)
    zh = jnp.zeros((HGRN_HEADS, HGRN_V, HGRN_K), F32)
    nw1 = row(norm1_w[0])

    _, ssd_c, hgrn_c = _mixer_scans(ctx[0], nw1, csh1, csc1, w_main, w_dt, conv_w[0], conv_b[0], ssm_vecs,
                                    rep, lb, ctx_len, (zs, zs), (zh, zh))
    ssd_init = (ssd_c[0][1], ssd_c[1][1])
    hgrn_init = (hgrn_c[0][1], hgrn_c[1][1])

    x2 = x[0]
    proj, ssd_l, hgrn_l = _mixer_scans(x2, nw1, sh1, sc1, w_main, w_dt, conv_w[0], conv_b[0], ssm_vecs,
                                       rep, lb, GRID_W, ssd_init, hgrn_init, ssd_post=(dsk, row(ssm_norm_w[0])))
    y_ssm = ssd_l[1][0]
    y_hgrn = _hgrn_post(hgrn_l[0][0], hgrn_l[1][0], proj, row(hgrn_norm_w[0]))
    y = _merge(y_ssm, y_hgrn, proj, wbs, wbh)
    h, hn2, logits = _outproj(y, x2, wo, g1, row(norm2_w[0]), sh2, sc2, wr, br)

    idx, pos, p, counts = _route(logits)
    counts = counts[0, :N_EXPERTS].astype(I32)
    bm = MOE_BM
    n_assign = seq * TOP_K
    n_blocks = -(-(n_assign + N_EXPERTS * (bm - 1)) // bm)
    n_slots = n_blocks * bm
    padded = (counts + bm - 1) // bm * bm
    pad_end = jnp.cumsum(padded)
    pad_start = pad_end - padded
    slots = (pad_start[idx[:, :TOP_K]] + pos[:, :TOP_K]).reshape(-1).astype(I32)
    blk_start = jnp.arange(n_blocks, dtype=I32) * bm
    blk_e = jnp.minimum(jnp.sum(blk_start[:, None] >= pad_end[None, :], axis=1), N_EXPERTS - 1).astype(I32)
    blk_count = jnp.clip(pad_start[blk_e] + counts[blk_e] - blk_start, 0, bm).astype(I32)
    half_start = jnp.arange(n_slots // MOE_HALF, dtype=I32) * MOE_HALF
    half_e = jnp.repeat(blk_e, bm // MOE_HALF)
    tails = (pad_start[half_e] + counts[half_e] - half_start < MOE_HALF).astype(I32)

    assert d == 2 * HALF_D
    xs_slab = _dispatch(slots, tails, hn2, n_slots)
    yb_slab = _experts(blk_e, blk_count, xs_slab, w_gate_up[0], b_gate_up[0], w_down[0], b_down[0])
    out = _combine(slots, h, p, g2, row(final_norm_w), yb_slab)
    return out.reshape(bsz, seq, d)
```

```python
import functools

import jax
import jax.numpy as jnp
from jax import lax
from jax.experimental import pallas as pl
from jax.experimental.pallas import tpu as pltpu

F32 = jnp.float32
BF16 = jnp.bfloat16
I32 = jnp.int32
U32 = jnp.uint32
HI = lax.Precision.HIGHEST

EPS = 1e-6
GRID_W = 64

LANES = 128
NSL = 8
HALF_D = NSL * LANES
SSM_HEADS = 32
SSM_HEAD_DIM = 64
SSM_GROUPS = 4
SSM_STATE = 128
SSM_INNER = SSM_HEADS * SSM_HEAD_DIM
SSM_GROUP_W = SSM_INNER // SSM_GROUPS
SSM_CHUNK = 128
HGRN_HEADS = 16
HGRN_K = 128
HGRN_V = 128
HGRN_CHUNK = 64
HGRN_SUB = 16
HGRN_HEAD_GROUP = 4
HGRN_SAFE_DECAY = 60.0
N_EXPERTS = 32
TOP_K = 4
SWIGLU_LIMIT = 7.0
SWIGLU_ALPHA = 1.702
MOE_BM = 1024
MOE_HALF = 512
MOE_TF = 512
VMEM_LIMIT = 56 * 1024 * 1024

COL_Z, COL_Q, COL_F, COL_I, COL_G, COL_GS, COL_GH, COL_XS = 0, 1, 2, 4, 5, 6, 7, 8


def _sigmoid(x):
    return 1.0 / (1.0 + jnp.exp(-x))


def _cparams(sem):
    return pltpu.CompilerParams(dimension_semantics=sem, vmem_limit_bytes=VMEM_LIMIT)


def _mod_kernel(c_ref, w_ref, b_ref, o_ref):
    c = c_ref[...]
    s = c * _sigmoid(c)
    o_ref[...] = jnp.dot(s, w_ref[...], precision=HI, preferred_element_type=F32) + b_ref[...]


def _modulation(cc, w_ada, b_ada):
    d, n = w_ada.shape
    tn = 1024
    return pl.pallas_call(
        _mod_kernel,
        grid=(n // tn,),
        in_specs=[pl.BlockSpec((8, d), lambda j: (0, 0)),
                  pl.BlockSpec((d, tn), lambda j: (0, j)),
                  pl.BlockSpec((1, tn), lambda j: (0, j))],
        out_specs=pl.BlockSpec((8, tn), lambda j: (0, j)),
        out_shape=jax.ShapeDtypeStruct((8, n), F32),
        compiler_params=_cparams(("arbitrary",)),
        name="adaln_mod",
    )(cc, w_ada, b_ada.reshape(1, n))


def _inproj_kernel(x_ref, nw_ref, sh_ref, sc_ref, w_ref, wdt_ref, o_ref, odt_ref, hn_ref):
    @pl.when(pl.program_id(1) == 0)
    def _():
        x = x_ref[...]
        ms = jnp.mean(x * x, axis=-1, keepdims=True)
        y = x * lax.rsqrt(ms + EPS) * nw_ref[...]
        hn = (y * (1.0 + sc_ref[...]) + sh_ref[...]).astype(BF16)
        hn_ref[...] = hn
        odt_ref[...] = jnp.dot(hn, wdt_ref[...], preferred_element_type=F32)

    o_ref[...] = jnp.dot(hn_ref[...], w_ref[...], preferred_element_type=F32)


def _inproj(x2, nw, sh, sc, w_main, w_dt):
    L, d = x2.shape
    n = w_main.shape[1]
    tm = min(1024, L)
    tn = 1024
    return pl.pallas_call(
        _inproj_kernel,
        grid=(L // tm, n // tn),
        in_specs=[pl.BlockSpec((tm, d), lambda i, j: (i, 0)),
                  pl.BlockSpec((1, d), lambda i, j: (0, 0)),
                  pl.BlockSpec((1, d), lambda i, j: (0, 0)),
                  pl.BlockSpec((1, d), lambda i, j: (0, 0)),
                  pl.BlockSpec((d, tn), lambda i, j: (0, j)),
                  pl.BlockSpec((d, LANES), lambda i, j: (0, 0))],
        out_specs=[pl.BlockSpec((tm, tn), lambda i, j: (i, j)),
                   pl.BlockSpec((tm, LANES), lambda i, j: (i, 0))],
        out_shape=[jax.ShapeDtypeStruct((L, n), F32), jax.ShapeDtypeStruct((L, LANES), F32)],
        scratch_shapes=[pltpu.VMEM((tm, d), BF16)],
        compiler_params=_cparams(("parallel", "arbitrary")),
        name="inproj",
    )(x2, nw, sh, sc, w_main, w_dt)


def _conv_body(u, w_ref, b_ref, row_len):
    tb = u.shape[0]
    pos = lax.broadcasted_iota(I32, u.shape, 0) & (row_len - 1)
    acc = u * w_ref[2:3, :] + b_ref[...]
    for k, off in ((0, -2), (1, -1), (3, 1)):
        shifted = pltpu.roll(u, shift=(-off) % tb, axis=0)
        valid = (pos + off >= 0) & (pos + off < row_len)
        acc = acc + jnp.where(valid, shifted, 0.0) * w_ref[k:k + 1, :]
    return acc * _sigmoid(acc)


def _conv_kernel(ux_ref, ub_ref, wx_ref, wb_ref, bx_ref, bb_ref, ox_ref, ob_ref, *, row_len):
    ox_ref[...] = _conv_body(ux_ref[...], wx_ref, bx_ref, row_len)
    ob_ref[...] = _conv_body(ub_ref[...], wb_ref, bb_ref, row_len)


def _conv_silu(proj, conv_w, conv_b, row_len):
    L = proj.shape[0]
    d = SSM_INNER
    nbc = 2 * SSM_GROUPS * SSM_STATE
    tb = max(row_len, min(512, L))
    wx, wb = conv_w[:, :d], conv_w[:, d:]
    bx, bb = conv_b[:d].reshape(1, d), conv_b[d:].reshape(1, nbc)
    return pl.pallas_call(
        functools.partial(_conv_kernel, row_len=row_len),
        grid=(L // tb,),
        in_specs=[pl.BlockSpec((tb, d), lambda i: (i, COL_XS)),
                  pl.BlockSpec((tb, nbc), lambda i: (i, COL_XS * 2 + 2)),
                  pl.BlockSpec((4, d), lambda i: (0, 0)),
                  pl.BlockSpec((4, nbc), lambda i: (0, 0)),
                  pl.BlockSpec((1, d), lambda i: (0, 0)),
                  pl.BlockSpec((1, nbc), lambda i: (0, 0))],
        out_specs=[pl.BlockSpec((tb, d), lambda i: (i, 0)),
                   pl.BlockSpec((tb, nbc), lambda i: (i, 0))],
        out_shape=[jax.ShapeDtypeStruct((L, d), F32), jax.ShapeDtypeStruct((L, nbc), F32)],
        compiler_params=_cparams(("parallel",)),
        name="conv_silu",
    )(proj, proj, wx, wb, bx, bb)


def _ssd_kernel(xs_ref, bc_ref, dt_ref, dtb_ref, alog_ref, amask_ref, rep_ref, s0_ref, *rest, rev, lane0, post):
    if post:
        yf_ref, z_ref, dsk_ref, nw_ref, y_ref, sfin_ref, S_ref = rest
    else:
        y_ref, sfin_ref, S_ref = rest
    n = pl.program_id(0)

    @pl.when(n == 0)
    def _():
        S_ref[...] = s0_ref[...]

    Q = xs_ref.shape[0]
    N = SSM_STATE
    GW = SSM_GROUP_W
    ti = lax.broadcasted_iota(I32, (Q, Q), 0)
    si = lax.broadcasted_iota(I32, (Q, Q), 1)
    allowed = (si >= ti) if rev else (si <= ti)
    tri = allowed.astype(F32)

    raw = dt_ref[...] + dtb_ref[...]
    dt = jnp.maximum(raw, 0.0) + jnp.log1p(jnp.exp(-jnp.abs(raw)))
    loga = dt * (-jnp.exp(alog_ref[...]) * amask_ref[...])
    cs = jnp.dot(tri, loga, precision=HI, preferred_element_type=F32)
    csT = cs.T
    tot = cs[0:1, :] if rev else cs[Q - 1:Q, :]
    fac = jnp.concatenate([jnp.exp(cs), jnp.exp(tot - cs), dt], axis=0).astype(BF16)
    tot8 = jnp.broadcast_to(tot, (8, LANES))
    t_hi = tot8.astype(BF16)
    t_r = tot8 - t_hi.astype(F32)
    t_mid = t_r.astype(BF16)
    t_lo = (t_r - t_mid.astype(F32)).astype(BF16)
    tot3 = jnp.concatenate([t_hi, t_mid, t_lo], axis=0)
    lane = lax.broadcasted_iota(I32, (Q, LANES), 1)
    lo_mask = lane < SSM_HEAD_DIM

    for g in range(SSM_GROUPS):
        rep_g = rep_ref[:, g * GW:(g + 1) * GW]
        fx = jnp.dot(fac, rep_g, preferred_element_type=F32)
        e_in, e_out, dt_x = fx[0:Q], fx[Q:2 * Q], fx[2 * Q:3 * Q]
        tx = jnp.dot(tot3, rep_g, preferred_element_type=F32)
        dec = jnp.exp((tx[0:1] + tx[8:9]) + tx[16:17])
        xdt = xs_ref[:, g * GW:(g + 1) * GW] * dt_x
        xw = (xdt * e_out).astype(BF16)
        xdt_b = xdt.astype(BF16)
        Bg = bc_ref[:, g * N:(g + 1) * N].astype(BF16)
        Cg = bc_ref[:, (SSM_GROUPS + g) * N:(SSM_GROUPS + g + 1) * N].astype(BF16)
        cb = lax.dot_general(Cg, Bg, (((1,), (1,)), ((), ())), preferred_element_type=F32)
        Sg = S_ref[g]
        y_off = jnp.dot(Cg, Sg.astype(BF16), preferred_element_type=F32) * e_in
        upd = lax.dot_general(Bg, xw, (((0,), (0,)), ((), ())), preferred_element_type=F32)
        S_ref[g] = Sg * dec + upd
        pieces = []
        for jp in range(GW // LANES):
            h0 = g * (GW // SSM_HEAD_DIM) + 2 * jp
            ms = []
            for hh in (h0, h0 + 1):
                col = cs[:, lane0 + hh:lane0 + hh + 1]
                row = csT[lane0 + hh:lane0 + hh + 1, :]
                dec_ts = jnp.exp(jnp.where(allowed, col - row, -jnp.inf))
                ms.append((cb * dec_ts).astype(BF16))
            m2 = jnp.concatenate(ms, axis=1)
            x2 = xdt_b[:, jp * LANES:(jp + 1) * LANES]
            r2 = jnp.concatenate([jnp.where(lo_mask, x2, jnp.zeros_like(x2)),
                                  jnp.where(lo_mask, jnp.zeros_like(x2), x2)], axis=0)
            y_d = jnp.dot(m2, r2, preferred_element_type=F32)
            pieces.append(y_d + y_off[:, jp * LANES:(jp + 1) * LANES])
        gs = slice(g * GW, (g + 1) * GW)
        y_g = jnp.concatenate(pieces, axis=1)
        if post:
            z_g = z_ref[:, gs]
            yy = (yf_ref[:, gs] + y_g + dsk_ref[:, gs] * xs_ref[:, gs]) * (z_g * _sigmoid(z_g))
            ms = jnp.mean(yy * yy, axis=-1, keepdims=True)
            y_ref[:, gs] = (yy * lax.rsqrt(ms + EPS) * nw_ref[:, gs]).astype(y_ref.dtype)
        else:
            y_ref[:, gs] = y_g

    @pl.when(n == pl.num_programs(0) - 1)
    def _():
        sfin_ref[...] = S_ref[...]


def _ssd_scan(xs, bc, dt, dtb, alog, amask, rep, s0, rev, post_inputs=None):
    L, d = xs.shape
    Q = SSM_CHUNK
    nc = L // Q
    idx = (lambda n: (nc - 1 - n, 0)) if rev else (lambda n: (n, 0))
    const2 = lambda n: (0, 0)
    const3 = lambda n: (0, 0, 0)
    sshape = (SSM_GROUPS, SSM_STATE, SSM_GROUP_W)
    post = post_inputs is not None
    in_specs = [pl.BlockSpec((Q, d), idx),
                pl.BlockSpec((Q, bc.shape[1]), idx),
                pl.BlockSpec((Q, LANES), idx),
                pl.BlockSpec((1, LANES), const2),
                pl.BlockSpec((1, LANES), const2),
                pl.BlockSpec((1, LANES), const2),
                pl.BlockSpec((LANES, d), const2),
                pl.BlockSpec(sshape, const3)]
    args = [xs, bc, dt, dtb, alog, amask, rep, s0]
    if post:
        zcol = (lambda n: (nc - 1 - n, COL_Z)) if rev else (lambda n: (n, COL_Z))
        in_specs += [pl.BlockSpec((Q, d), idx), pl.BlockSpec((Q, d), zcol),
                     pl.BlockSpec((1, d), const2), pl.BlockSpec((1, d), const2)]
        args += list(post_inputs)
    return pl.pallas_call(
        functools.partial(_ssd_kernel, rev=rev, lane0=SSM_HEADS if rev else 0, post=post),
        grid=(nc,),
        in_specs=in_specs,
        out_specs=[pl.BlockSpec((Q, d), idx),
                   pl.BlockSpec(sshape, const3)],
        out_shape=[jax.ShapeDtypeStruct((L, d), BF16 if post else F32), jax.ShapeDtypeStruct(sshape, F32)],
        scratch_shapes=[pltpu.VMEM(sshape, F32)],
        compiler_params=_cparams(("arbitrary",)),
        name="ssd_bwd" if rev else "ssd_fwd",
    )(*args)


class _HgrnDir:
    def __init__(self, refs, rev):
        self.refs = refs
        self.rev = rev

    def build(self):
        (q_ref, f_ref, i_ref, lb_ref, o_ref, ST_ref, kk_ref, bcs_ref, lf_ref, v_ref, vT_ref), rev = self.refs, self.rev
        _hgrn_dir_body(self, q_ref, f_ref, i_ref, lb_ref, o_ref, ST_ref, kk_ref, bcs_ref, lf_ref, v_ref, vT_ref, rev)
        return self


def _hgrn_dir_body(self, q_ref, f_ref, i_ref, lb_ref, o_ref, ST_ref, kk_ref, bcs_ref, lf_ref, v_ref, vT_ref, rev):
    C = q_ref.shape[0]
    c = HGRN_SUB
    nsub = C // c
    K = HGRN_K
    d = HGRN_HEADS * K
    ti = lax.broadcasted_iota(I32, (C, C), 0)
    si = lax.broadcasted_iota(I32, (C, C), 1)
    allowed = (si >= ti) if rev else (si <= ti)
    bt, bs = ti // c, si // c
    dist = (bs - bt) if rev else (bt - bs)
    order = list(range(nsub - 1, -1, -1)) if rev else list(range(nsub))

    def gate(h):
        sl = slice(h * K, (h + 1) * K)
        lbr = lb_ref[:, sl]
        lmax = jnp.max(lbr, axis=0, keepdims=True)
        le = jnp.exp(lbr - lmax)
        lb = le[0:1, :] / jnp.sum(le, axis=0, keepdims=True)
        fr = f_ref[:, sl]
        e = jnp.exp(-jnp.abs(fr))
        rinv = 1.0 / (1.0 + e)
        pos = fr >= 0.0
        sg = jnp.where(pos, rinv, e * rinv)
        sgn = jnp.where(pos, e * rinv, rinv)
        logf = jnp.log(lb + (1.0 - lb) * sg)
        kk_ref[:, sl] = (1.0 - lb) * sgn
        iv = i_ref[:, sl]
        vv = iv * _sigmoid(iv)
        v_ref[:, sl] = vv
        vT_ref[sl, :] = vv.T.astype(BF16)
        hi = logf.astype(BF16)
        r1 = logf - hi.astype(F32)
        mid = r1.astype(BF16)
        lo = (r1 - mid.astype(F32)).astype(BF16)
        lf_ref[:, h * K:(h + 1) * K] = hi
        lf_ref[:, d + h * K:d + (h + 1) * K] = mid
        lf_ref[:, 2 * d + h * K:2 * d + (h + 1) * K] = lo

    def levels(col):
        r_in, r_out = {}, {}
        for p, sc_i in enumerate(order):
            last = sc_i * c + (0 if rev else c - 1)
            r_out[p] = bcs_ref[last:last + 1, col]
            r_in[p] = r_out[p - 1] if p > 0 else jnp.zeros_like(r_out[p])
        return r_in, r_out

    def cumulate():
        cs3 = jnp.dot(allowed.astype(BF16), lf_ref[...], preferred_element_type=F32)
        bcs_ref[...] = (cs3[:, :d] + cs3[:, d:2 * d]) + cs3[:, 2 * d:]
        r_in_all, r_out_all = levels(slice(0, d))
        worst = r_in_all[0] - r_out_all[0]
        for p in range(1, nsub):
            worst = jnp.maximum(worst, r_in_all[p] - r_out_all[p])
        return worst

    def stages(exact_pairwise):
        t16 = lax.broadcasted_iota(I32, (c, K), 0)
        ones_b = jnp.ones((K, K), BF16)
        t2 = lax.broadcasted_iota(I32, (C, 2 * C), 0)
        col2 = lax.broadcasted_iota(I32, (C, 2 * C), 1)
        s2 = col2 & (C - 1)
        dist2 = (s2 // c - t2 // c) if rev else (t2 // c - s2 // c)
        dist2 = jnp.where(col2 < C, dist2, -1)
        same_blk = (col2 >= C) & (t2 // c == s2 // c) & ((s2 >= t2) if rev else (s2 <= t2))
        band = {dd: dist2 == dd for dd in range(1, nsub)}
        def matmuls(h):
            sl = slice(h * K, (h + 1) * K)
            bcs = bcs_ref[:, sl]
            kk = kk_ref[:, sl]
            qq = q_ref[:, sl] * (K ** -0.5)
            tot = bcs[0:1, :] if rev else bcs[C - 1:C, :]

            r_in, r_out = levels(sl)

            def by_sub_chunk(fn):
                rows = [None] * nsub
                for p, sc_i in enumerate(order):
                    rows[sc_i] = jnp.broadcast_to(fn(p), (c, K))
                return jnp.concatenate(rows, axis=0)

            R_in = by_sub_chunk(lambda p: r_in[p])
            R_out = by_sub_chunk(lambda p: r_out[p])
            qh = qq * jnp.exp(bcs - R_in)
            kt = kk * jnp.exp(R_out - bcs)
            ST = ST_ref[h]
            if exact_pairwise:
                q_in = (qq * jnp.exp(bcs)).astype(BF16)
                k_end = (kk * jnp.exp(tot - bcs)).astype(BF16)
            else:
                q_in = (qh * by_sub_chunk(lambda p: jnp.exp(r_in[p]))).astype(BF16)
                k_end = (kt * by_sub_chunk(lambda p: jnp.exp(tot - r_out[p]))).astype(BF16)
            o = lax.dot_general(q_in, ST.astype(BF16), (((1,), (1,)), ((), ())), preferred_element_type=F32)
            upd = jnp.dot(vT_ref[sl, :], k_end, preferred_element_type=F32)
            new_state = ST * jnp.exp(tot) + upd
            lhs = [qh]
            for dd in range(2, nsub):
                rows_x = [None] * nsub
                for p, sc_i in enumerate(order):
                    if p >= dd:
                        rows_x[sc_i] = jnp.broadcast_to(jnp.exp(r_in[p] - r_out[p - dd]), (c, K))
                    else:
                        rows_x[sc_i] = jnp.zeros((c, K), F32)
                lhs.append(qh * jnp.concatenate(rows_x, axis=0))
            lhs = jnp.concatenate(lhs, axis=0).astype(BF16)
            if exact_pairwise:
                rhs = kt.astype(BF16)
            else:
                k_diag = kt * by_sub_chunk(lambda p: jnp.exp(r_in[p] - r_out[p]))
                rhs = jnp.concatenate([kt, k_diag], axis=0).astype(BF16)
            a = lax.dot_general(lhs, rhs, (((1,), (1,)), ((), ())), preferred_element_type=F32)
            return o, a, new_state

        def finish(h, o, a, new_state):
            sl = slice(h * K, (h + 1) * K)
            ST_ref[h] = new_state
            vv = v_ref[:, sl]
            vb = vv.astype(BF16)
            if exact_pairwise:
                bcs = bcs_ref[:, sl]
                kk = kk_ref[:, sl]
                qq = q_ref[:, sl] * (K ** -0.5)
                att = jnp.where(dist == 1, a[0:C], 0.0)
                for dd in range(2, nsub):
                    att = att + jnp.where(dist == dd, a[(dd - 1) * C:dd * C], 0.0)
                o = o + jnp.dot(att.astype(BF16), vb, preferred_element_type=F32)
                for sc_i in range(nsub):
                    r0 = sc_i * c
                    b_i = bcs[r0:r0 + c, :]
                    q_i = qq[r0:r0 + c, :]
                    ws = []
                    for s in range(c):
                        ok = (t16 <= s) if rev else (t16 >= s)
                        ex = jnp.exp(jnp.where(ok, b_i - bcs[r0 + s:r0 + s + 1, :], -jnp.inf))
                        ws.append((q_i * ex * kk[r0 + s:r0 + s + 1, :]).astype(BF16))
                    w_all = jnp.concatenate(ws, axis=0)
                    a_all = jnp.dot(w_all, ones_b, preferred_element_type=F32)
                    o_d = jnp.zeros((c, HGRN_V), F32)
                    for s in range(c):
                        o_d = o_d + a_all[s * c:(s + 1) * c, :] * vv[r0 + s:r0 + s + 1, :]
                    o_ref[r0:r0 + c, sl] = o[r0:r0 + c, :] + o_d
            else:
                att = jnp.zeros((C, 2 * C), F32)
                for dd in range(nsub - 1, 1, -1):
                    att = jnp.where(band[dd], a[(dd - 1) * C:dd * C], att)
                att = jnp.where(band[1] | same_blk, a[0:C], att)
                v2 = jnp.concatenate([vb, vb], axis=0)
                o_ref[:, sl] = o + jnp.dot(att.astype(BF16), v2, preferred_element_type=F32)

        return matmuls, finish

    self.gate, self.cumulate, self.stages = gate, cumulate, stages


def _hgrn_kernel(qf_ref, ff_ref, if_ref, qb_ref, fb_ref, ib_ref, lbf_ref, lbb_ref, s0f_ref, s0b_ref,
                 of_ref, ob_ref, sff_ref, sfb_ref, *scratch):
    n = pl.program_id(0)
    nscr = len(scratch) // 2
    scr_f, scr_b = scratch[:nscr], scratch[nscr:]

    @pl.when(n == 0)
    def _():
        scr_f[0][...] = s0f_ref[...]
        scr_b[0][...] = s0b_ref[...]

    dirs = [_HgrnDir((qf_ref, ff_ref, if_ref, lbf_ref, of_ref) + tuple(scr_f), False).build(),
            _HgrnDir((qb_ref, fb_ref, ib_ref, lbb_ref, ob_ref) + tuple(scr_b), True).build()]
    for h in range(HGRN_HEADS):
        for dr in dirs:
            dr.gate(h)
    worst = jnp.maximum(dirs[0].cumulate(), dirs[1].cumulate())
    safe = jnp.max(worst) < HGRN_SAFE_DECAY

    def heads(exact_pairwise):
        fns = [dr.stages(exact_pairwise) for dr in dirs]
        group = 1 if exact_pairwise else HGRN_HEAD_GROUP
        for h0 in range(0, HGRN_HEADS, group):
            staged = [(fin, h, mm(h)) for h in range(h0, h0 + group) for mm, fin in fns]
            for fin, h, vals in staged:
                fin(h, *vals)

    @pl.when(safe)
    def _():
        heads(False)

    @pl.when(jnp.logical_not(safe))
    def _():
        heads(True)

    @pl.when(n == pl.num_programs(0) - 1)
    def _():
        sff_ref[...] = scr_f[0][...]
        sfb_ref[...] = scr_b[0][...]


def _hgrn_scan(proj, lb, s0):
    L = proj.shape[0]
    d = HGRN_HEADS * HGRN_K
    C = HGRN_CHUNK
    nc = L // C
    sshape = (HGRN_HEADS, HGRN_V, HGRN_K)
    const3 = lambda n: (0, 0, 0)
    fwd = lambda col: pl.BlockSpec((C, d), lambda n: (n, col))
    bwd = lambda col: pl.BlockSpec((C, d), lambda n: (nc - 1 - n, col))
    lb_spec = pl.BlockSpec(lb[0].shape, lambda n: (0, 0))
    st_spec = pl.BlockSpec(sshape, const3)
    scratch = [pltpu.VMEM(sshape, F32), pltpu.VMEM((C, d), F32), pltpu.VMEM((C, d), F32),
               pltpu.VMEM((C, 3 * d), BF16), pltpu.VMEM((C, d), F32), pltpu.VMEM((d, C), BF16)]
    return pl.pallas_call(
        _hgrn_kernel,
        grid=(nc,),
        in_specs=[fwd(COL_Q), fwd(COL_F), fwd(COL_I), bwd(COL_Q), bwd(COL_F + 1), bwd(COL_I),
                  lb_spec, lb_spec, st_spec, st_spec],
        out_specs=[fwd(0), bwd(0), st_spec, st_spec],
        out_shape=[jax.ShapeDtypeStruct((L, d), F32), jax.ShapeDtypeStruct((L, d), F32),
                   jax.ShapeDtypeStruct(sshape, F32), jax.ShapeDtypeStruct(sshape, F32)],
        scratch_shapes=scratch + scratch,
        compiler_params=_cparams(("arbitrary",)),
        name="hgrn",
    )(proj, proj, proj, proj, proj, proj, lb[0], lb[1], s0[0], s0[1])


def _hgrn_post_kernel(of_ref, ob_ref, g_ref, nw_ref, o_ref):
    nw = nw_ref[...]
    for h in range(HGRN_HEADS):
        sl = slice(h * HGRN_V, (h + 1) * HGRN_V)
        o = of_ref[:, sl] + ob_ref[:, sl]
        ms = jnp.mean(o * o, axis=-1, keepdims=True)
        g = g_ref[:, sl]
        o_ref[:, sl] = (o * lax.rsqrt(ms + EPS) * nw * (g * _sigmoid(g))).astype(o_ref.dtype)


def _hgrn_post(of, ob, proj, nw):
    L, d = of.shape
    tm = min(512, L)
    row = lambda i: (i, 0)
    return pl.pallas_call(
        _hgrn_post_kernel,
        grid=(L // tm,),
        in_specs=[pl.BlockSpec((tm, d), row), pl.BlockSpec((tm, d), row),
                  pl.BlockSpec((tm, d), lambda i: (i, COL_G)),
                  pl.BlockSpec((1, HGRN_V), lambda i: (0, 0))],
        out_specs=pl.BlockSpec((tm, d), row),
        out_shape=jax.ShapeDtypeStruct((L, d), BF16),
        compiler_params=_cparams(("parallel",)),
        name="hgrn_post",
    )(of, ob, proj, nw)


def _merge_kernel(ys_ref, yh_ref, gs_ref, gh_ref, wbs_ref, wbh_ref, o_ref):
    a = jnp.dot(ys_ref[...], wbs_ref[...], preferred_element_type=F32)
    b = jnp.dot(yh_ref[...], wbh_ref[...], preferred_element_type=F32)
    o_ref[...] = (_sigmoid(gs_ref[...]) * a + _sigmoid(gh_ref[...]) * b).astype(o_ref.dtype)


def _merge(ys, yh, proj, wbs, wbh):
    L, d = ys.shape
    tm = min(1024, L)
    tn = 512
    nj = d // tn
    return pl.pallas_call(
        _merge_kernel,
        grid=(L // tm, nj),
        in_specs=[pl.BlockSpec((tm, d), lambda i, j: (i, 0)),
                  pl.BlockSpec((tm, d), lambda i, j: (i, 0)),
                  pl.BlockSpec((tm, tn), lambda i, j: (i, COL_GS * nj + j)),
                  pl.BlockSpec((tm, tn), lambda i, j: (i, COL_GH * nj + j)),
                  pl.BlockSpec((d, tn), lambda i, j: (0, j)),
                  pl.BlockSpec((d, tn), lambda i, j: (0, j))],
        out_specs=pl.BlockSpec((tm, tn), lambda i, j: (i, j)),
        out_shape=jax.ShapeDtypeStruct((L, d), BF16),
        compiler_params=_cparams(("parallel", "arbitrary")),
        name="merge",
    )(ys, yh, proj, proj, wbs, wbh)


def _outproj_kernel(y_ref, x_ref, wo_ref, g1_ref, nw_ref, sh_ref, sc_ref, wr_ref, br_ref,
                    h_ref, hn_ref, lg_ref):
    h = x_ref[...] + g1_ref[...] * jnp.dot(y_ref[...], wo_ref[...], preferred_element_type=F32)
    h_ref[...] = h
    ms = jnp.mean(h * h, axis=-1, keepdims=True)
    hn = h * lax.rsqrt(ms + EPS) * nw_ref[...] * (1.0 + sc_ref[...]) + sh_ref[...]
    h_hi = hn.astype(BF16)
    h_lo = (hn - h_hi.astype(F32)).astype(BF16)
    wr = wr_ref[...]
    w_hi = wr.astype(BF16)
    w_lo = (wr - w_hi.astype(F32)).astype(BF16)
    lg = jnp.dot(h_hi, w_hi, preferred_element_type=F32)
    lg = lg + (jnp.dot(h_hi, w_lo, preferred_element_type=F32) + jnp.dot(h_lo, w_hi, preferred_element_type=F32))
    lg_ref[...] = lg + br_ref[...]
    _store_packed_slabs(hn_ref, hn)


def _outproj(y, x2, wo, g1, nw, sh, sc, wr, br):
    L, d = x2.shape
    tm = min(256, L)
    row = lambda i: (i, 0)
    const = lambda i: (0, 0)
    vec = pl.BlockSpec((1, d), const)
    return pl.pallas_call(
        _outproj_kernel,
        grid=(L // tm,),
        in_specs=[pl.BlockSpec((tm, d), row), pl.BlockSpec((tm, d), row),
                  pl.BlockSpec((d, d), const), vec, vec, vec, vec,
                  pl.BlockSpec((d, LANES), const), pl.BlockSpec((1, LANES), const)],
        out_specs=[pl.BlockSpec((tm, d), row), pl.BlockSpec((tm * NSL, LANES), row), pl.BlockSpec((tm, LANES), row)],
        out_shape=[jax.ShapeDtypeStruct((L, d), F32), jax.ShapeDtypeStruct((L * NSL, LANES), U32),
                   jax.ShapeDtypeStruct((L, LANES), F32)],
        compiler_params=_cparams(("parallel",)),
        name="outproj",
    )(y, x2, wo, g1, nw, sh, sc, wr, br)


def _route_kernel(lg_ref, idx_ref, pos_ref, p_ref, cnt_ref, carry_ref):
    n = pl.program_id(0)

    @pl.when(n == 0)
    def _():
        carry_ref[...] = jnp.zeros_like(carry_ref)

    tm = lg_ref.shape[0]
    lane_i = lax.broadcasted_iota(I32, (tm, LANES), 1)
    lane = lane_i.astype(F32)
    neg = jnp.float32(-jnp.inf)
    l = jnp.where(lane_i < N_EXPERTS, lg_ref[...], neg)
    vals, sels, ids = [], [], []
    for _ in range(TOP_K):
        m = jnp.max(l, axis=-1, keepdims=True)
        i_k = jnp.min(jnp.where(l == m, lane, float(LANES)), axis=-1, keepdims=True)
        sel = lane == i_k
        l = jnp.where(sel, neg, l)
        vals.append(m)
        sels.append(sel)
        ids.append(i_k)
    es = [jnp.exp(v - vals[0]) for v in vals]
    den = es[0]
    for e in es[1:]:
        den = den + e
    onehot = jnp.zeros((tm, LANES), F32)
    for sel in sels:
        onehot = onehot + jnp.where(sel, 1.0, 0.0)
    ti = lax.broadcasted_iota(I32, (tm, tm), 0)
    si = lax.broadcasted_iota(I32, (tm, tm), 1)
    strict = jnp.where(si < ti, 1.0, 0.0).astype(BF16)
    before = jnp.dot(strict, onehot.astype(BF16), preferred_element_type=F32) + carry_ref[...]
    idx_out = jnp.zeros((tm, LANES), F32)
    pos_out = jnp.zeros((tm, LANES), F32)
    p_out = jnp.zeros((tm, LANES), F32)
    for k in range(TOP_K):
        pos_k = jnp.sum(jnp.where(sels[k], before, 0.0), axis=-1, keepdims=True)
        idx_out = jnp.where(lane_i == k, ids[k], idx_out)
        pos_out = jnp.where(lane_i == k, pos_k, pos_out)
        p_out = jnp.where(lane_i == k, es[k] / den, p_out)
    idx_ref[...] = idx_out.astype(I32)
    pos_ref[...] = pos_out.astype(I32)
    p_ref[...] = p_out
    carry_ref[...] = carry_ref[...] + jnp.sum(onehot, axis=0, keepdims=True)
    cnt_ref[...] = carry_ref[...]


def _route(logits):
    L = logits.shape[0]
    tm = min(256, L)
    row = lambda i: (i, 0)
    return pl.pallas_call(
        _route_kernel,
        grid=(L // tm,),
        in_specs=[pl.BlockSpec((tm, LANES), row)],
        out_specs=[pl.BlockSpec((tm, LANES), row), pl.BlockSpec((tm, LANES), row),
                   pl.BlockSpec((tm, LANES), row), pl.BlockSpec((1, LANES), lambda i: (0, 0))],
        out_shape=[jax.ShapeDtypeStruct((L, LANES), I32), jax.ShapeDtypeStruct((L, LANES), I32),
                   jax.ShapeDtypeStruct((L, LANES), F32), jax.ShapeDtypeStruct((1, LANES), F32)],
        scratch_shapes=[pltpu.VMEM((1, LANES), F32)],
        compiler_params=_cparams(("arbitrary",)),
        name="route",
    )(logits)


def _slab_rows(r):
    return pl.ds(pl.multiple_of(r * NSL, NSL), NSL)


def _pack_pair(lo, hi):
    def bf16_bits(v):
        b = lax.bitcast_convert_type(v, U32)
        return (b + jnp.uint32(0x7FFF) + ((b >> 16) & jnp.uint32(1))) >> 16
    return (bf16_bits(hi) << 16) | bf16_bits(lo)


def _unpack_pair(w):
    lo = lax.bitcast_convert_type(w << 16, F32)
    hi = lax.bitcast_convert_type(w & jnp.uint32(0xFFFF0000), F32)
    return lo, hi


def _store_packed_slabs(ref, val, row0=0):
    m = val.shape[0]
    for s in range(NSL):
        lo = val[:, s * LANES:(s + 1) * LANES]
        hi = val[:, HALF_D + s * LANES:HALF_D + (s + 1) * LANES]
        ref[pl.ds(row0 + s, m, stride=NSL), :] = _pack_pair(lo, hi)


def _dispatch_kernel(slot_ref, tail_ref, x_ref, o_ref, zero_ref, sem):
    tm = x_ref.shape[0] // NSL
    base = pl.program_id(0) * tm

    @pl.when(pl.program_id(0) == 0)
    def _():
        zero_ref[...] = jnp.zeros_like(zero_ref)
        n_half = o_ref.shape[0] // (MOE_HALF * NSL)

        def fill_copy(hb):
            start = pl.multiple_of(hb * (MOE_HALF * NSL), MOE_HALF * NSL)
            return pltpu.make_async_copy(zero_ref, o_ref.at[pl.ds(start, MOE_HALF * NSL)], sem)

        def start_fill(hb, carry):
            @pl.when(tail_ref[hb] > 0)
            def _():
                fill_copy(hb).start()
            return carry

        def wait_fill(hb, carry):
            @pl.when(tail_ref[hb] > 0)
            def _():
                fill_copy(hb).wait()
            return carry

        lax.fori_loop(0, n_half, start_fill, 0)
        lax.fori_loop(0, n_half, wait_fill, 0)

    def row_copy(t, s):
        return pltpu.make_async_copy(x_ref.at[_slab_rows(t)], o_ref.at[_slab_rows(s)], sem)

    def issue(t, carry):
        for k in range(TOP_K):
            row_copy(t, slot_ref[(base + t) * TOP_K + k]).start(priority=k % 2)
        return carry

    lax.fori_loop(0, tm, issue, 0)

    def drain(t, carry):
        for k in range(TOP_K):
            row_copy(t, slot_ref[(base + t) * TOP_K + k]).wait()
        return carry

    lax.fori_loop(0, tm, drain, 0)


def _dispatch(slots, tails, x_slab, n_slots):
    L = x_slab.shape[0] // NSL
    tm = min(256, L)
    grid_spec = pltpu.PrefetchScalarGridSpec(
        num_scalar_prefetch=2,
        grid=(L // tm,),
        in_specs=[pl.BlockSpec((tm * NSL, LANES), lambda i, sl, tl: (i, 0))],
        out_specs=pl.BlockSpec(memory_space=pl.ANY),
        scratch_shapes=[pltpu.VMEM((MOE_HALF * NSL, LANES), x_slab.dtype), pltpu.SemaphoreType.DMA(())],
    )
    return pl.pallas_call(
        _dispatch_kernel,
        grid_spec=grid_spec,
        out_shape=jax.ShapeDtypeStruct((n_slots * NSL, LANES), x_slab.dtype),
        compiler_params=_cparams(("arbitrary",)),
        name="dispatch",
    )(slots, tails, x_slab)


def _expert_kernel(be_ref, bc_ref, x_ref, wg_ref, wu_ref, bg_ref, bu_ref, wd_ref, bd_ref, o_ref,
                   xb_ref, acc_ref):
    i = pl.program_id(0)
    j = pl.program_id(1)
    nj = pl.num_programs(1)
    hm = MOE_HALF
    count = bc_ref[i]

    for half in range(MOE_BM // hm):
        rows = slice(half * hm, (half + 1) * hm)
        slab_rows = slice(half * hm * NSL, (half + 1) * hm * NSL)
        live = count > half * hm

        @pl.when(live & (j == 0))
        def _():
            for s in range(NSL):
                lo, hi = _unpack_pair(x_ref[pl.ds(half * hm * NSL + s, hm, stride=NSL), :])
                xb_ref[rows, s * LANES:(s + 1) * LANES] = lo.astype(BF16)
                xb_ref[rows, HALF_D + s * LANES:HALF_D + (s + 1) * LANES] = hi.astype(BF16)
            acc_ref[rows, :] = jnp.zeros((hm, acc_ref.shape[1]), F32)

        @pl.when(live)
        def _():
            xb = xb_ref[rows, :]
            gate = jnp.dot(xb, wg_ref[0].astype(BF16), preferred_element_type=F32) + bg_ref[0]
            up = jnp.dot(xb, wu_ref[0].astype(BF16), preferred_element_type=F32) + bu_ref[0]
            gate = jnp.minimum(gate, SWIGLU_LIMIT)
            up = jnp.clip(up, -SWIGLU_LIMIT, SWIGLU_LIMIT)
            glu = gate * _sigmoid(SWIGLU_ALPHA * gate)
            act = ((up + 1.0) * glu).astype(BF16)
            acc_ref[rows, :] += jnp.dot(act, wd_ref[0].astype(BF16), preferred_element_type=F32)

        @pl.when(live & (j == nj - 1))
        def _():
            _store_packed_slabs(o_ref, acc_ref[rows, :] + bd_ref[0], row0=half * hm * NSL)

        @pl.when(jnp.logical_not(live) & (j == nj - 1))
        def _():
            o_ref[slab_rows, :] = jnp.zeros((hm * NSL, LANES), o_ref.dtype)


def _experts(blk_e, blk_count, xs_slab, w_gu, b_gu, w_dn, b_dn):
    n_slots = xs_slab.shape[0] // NSL
    ne, d, two_ff = w_gu.shape
    dff = two_ff // 2
    bm, tf = MOE_BM, MOE_TF
    nj = dff // tf
    nb = n_slots // bm

    def jeff(j, bv, i):
        return jnp.where(bv[i] > 0, j, nj - 1)

    grid_spec = pltpu.PrefetchScalarGridSpec(
        num_scalar_prefetch=2,
        grid=(nb, nj),
        in_specs=[pl.BlockSpec((bm * NSL, LANES), lambda i, j, be, bv: (jnp.where(bv[i] > 0, i, 0), 0)),
                  pl.BlockSpec((1, d, tf), lambda i, j, be, bv: (be[i], 0, jeff(j, bv, i))),
                  pl.BlockSpec((1, d, tf), lambda i, j, be, bv: (be[i], 0, nj + jeff(j, bv, i))),
                  pl.BlockSpec((1, 1, tf), lambda i, j, be, bv: (be[i], 0, jeff(j, bv, i))),
                  pl.BlockSpec((1, 1, tf), lambda i, j, be, bv: (be[i], 0, nj + jeff(j, bv, i))),
                  pl.BlockSpec((1, tf, d), lambda i, j, be, bv: (be[i], jeff(j, bv, i), 0)),
                  pl.BlockSpec((1, 1, d), lambda i, j, be, bv: (be[i], 0, 0))],
        out_specs=pl.BlockSpec((bm * NSL, LANES), lambda i, j, be, bv: (i, 0)),
        scratch_shapes=[pltpu.VMEM((bm, d), BF16), pltpu.VMEM((bm, d), F32)],
    )
    return pl.pallas_call(
        _expert_kernel,
        grid_spec=grid_spec,
        out_shape=jax.ShapeDtypeStruct((n_slots * NSL, LANES), U32),
        compiler_params=_cparams(("arbitrary", "arbitrary")),
        name="experts",
    )(blk_e, blk_count, xs_slab, w_gu, w_gu, b_gu.reshape(ne, 1, two_ff), b_gu.reshape(ne, 1, two_ff),
      w_dn, b_dn.reshape(ne, 1, d))


def _combine_kernel(slot_ref, h_ref, p_ref, g2_ref, fw_ref, yb_ref, o_ref, *scratch):
    buf_sets = (scratch[0:TOP_K], scratch[TOP_K:2 * TOP_K])
    sems = scratch[2 * TOP_K]
    tm = h_ref.shape[0]
    i = pl.program_id(0)

    def gather(tile, which, wait):
        bufs, sem = buf_sets[which], sems.at[which]

        def body(t, carry):
            for k in range(TOP_K):
                s = slot_ref[(tile * tm + t) * TOP_K + k]
                cp = pltpu.make_async_copy(yb_ref.at[_slab_rows(s)], bufs[k].at[_slab_rows(t)], sem)
                if wait:
                    cp.wait()
                else:
                    cp.start(priority=k % 2)
            return carry

        lax.fori_loop(0, tm, body, 0)

    for which in (0, 1):
        @pl.when(lax.rem(i, 2) == which)
        def _():
            if which == 0:
                @pl.when(i == 0)
                def _():
                    gather(i, 0, wait=False)

            @pl.when(i + 1 < pl.num_programs(0))
            def _():
                gather(i + 1, 1 - which, wait=False)

            gather(i, which, wait=True)
            _combine_reduce(buf_sets[which], h_ref, p_ref, g2_ref, fw_ref, o_ref)


def _combine_reduce(gbufs, h_ref, p_ref, g2_ref, fw_ref, o_ref):
    tm = h_ref.shape[0]
    p = p_ref[...]
    ssq = jnp.zeros((tm, 1), F32)
    for s in range(NSL):
        moe_lo = jnp.zeros((tm, LANES), F32)
        moe_hi = jnp.zeros((tm, LANES), F32)
        for k in range(TOP_K):
            lo, hi = _unpack_pair(gbufs[k][pl.ds(s, tm, stride=NSL), :])
            moe_lo = moe_lo + p[:, k:k + 1] * lo
            moe_hi = moe_hi + p[:, k:k + 1] * hi
        for moe, c0 in ((moe_lo, s * LANES), (moe_hi, HALF_D + s * LANES)):
            cs = slice(c0, c0 + LANES)
            hs = h_ref[:, cs] + g2_ref[:, cs] * moe
            o_ref[:, cs] = hs
            ssq = ssq + jnp.sum(hs * hs, axis=-1, keepdims=True)
    scale = lax.rsqrt(ssq / h_ref.shape[1] + EPS)
    o_ref[...] = o_ref[...] * scale * fw_ref[...]


def _combine(slots, h, p, g2, fw, yb_slab):
    L, d = h.shape
    tm = min(256, L)
    grid_spec = pltpu.PrefetchScalarGridSpec(
        num_scalar_prefetch=1,
        grid=(L // tm,),
        in_specs=[pl.BlockSpec((tm, d), lambda i, sl: (i, 0)),
                  pl.BlockSpec((tm, LANES), lambda i, sl: (i, 0)),
                  pl.BlockSpec((1, d), lambda i, sl: (0, 0)),
                  pl.BlockSpec((1, d), lambda i, sl: (0, 0)),
                  pl.BlockSpec(memory_space=pl.ANY)],
        out_specs=pl.BlockSpec((tm, d), lambda i, sl: (i, 0)),
        scratch_shapes=[pltpu.VMEM((tm * NSL, LANES), U32) for _ in range(2 * TOP_K)]
        + [pltpu.SemaphoreType.DMA((2,))],
    )
    return pl.pallas_call(
        _combine_kernel,
        grid_spec=grid_spec,
        out_shape=jax.ShapeDtypeStruct((L, d), F32),
        compiler_params=_cparams(("arbitrary",)),
        name="combine",
    )(slots, h, p, g2, fw, yb_slab)


def _mixer_scans(x2, nw1, sh1, sc1, w_main, w_dt, conv_w, conv_b, ssm_vecs, rep, lb, row_len, ssd_init, hgrn_init,
                 ssd_post=None):
    proj, dtr = _inproj(x2, nw1, sh1, sc1, w_main, w_dt)
    xs, bc = _conv_silu(proj, conv_w, conv_b, row_len)
    dtb, alog, amask = ssm_vecs
    ssd_f = _ssd_scan(xs, bc, dtr, dtb, alog, amask[0], rep[0], ssd_init[0], rev=False)
    post_inputs = None if ssd_post is None else (ssd_f[0], proj) + tuple(ssd_post)
    ssd_b = _ssd_scan(xs, bc, dtr, dtb, alog, amask[1], rep[1], ssd_init[1], rev=True, post_inputs=post_inputs)
    o_f, o_b, st_f, st_b = _hgrn_scan(proj, lb, hgrn_init)
    return proj, [ssd_f, ssd_b], [(o_f, st_f), (o_b, st_b)]


def kernel(x, c, ctx, c_ctx, w_ada, b_ada, norm1_w, norm2_w, w_in, conv_w, conv_b, dt_bias, a_log, d_skip,
           ssm_norm_w, hgrn_lb, hgrn_norm_w, w_branch_ssm, w_branch_hgrn, w_out, w_router, b_router,
           w_gate_up, b_gate_up, w_down, b_down, final_norm_w):
    bsz, seq, d = x.shape
    assert bsz == 1 and w_ada.shape[0] == 1
    ctx_len = ctx.shape[1]
    H = SSM_HEADS

    w = w_in[0]
    o_z = SSM_INNER
    o_xbc = o_z + SSM_INNER + 2 * SSM_GROUPS * SSM_STATE
    o_dt = o_xbc + 2 * H
    w_main = jnp.concatenate([w[:, :o_z].astype(BF16), w[:, o_dt:].astype(BF16), w[:, o_z:o_xbc].astype(BF16)], axis=1)
    w_dt = jnp.pad(w[:, o_xbc:o_dt], ((0, 0), (0, LANES - 2 * H))).astype(BF16)
    dtb = jnp.pad(dt_bias[0].reshape(1, 2 * H), ((0, 0), (0, LANES - 2 * H)))
    alog = jnp.pad(a_log[0].reshape(1, 2 * H), ((0, 0), (0, LANES - 2 * H)))
    lane = jnp.arange(LANES)
    amask = [((lane >= dd * H) & (lane < (dd + 1) * H)).astype(F32).reshape(1, LANES) for dd in (0, 1)]
    chan_head = jnp.arange(SSM_INNER) // SSM_HEAD_DIM
    rep = [(lane[:, None] == (chan_head[None, :] + dd * H)).astype(BF16) for dd in (0, 1)]
    dsk = jnp.repeat(d_skip[0], SSM_HEAD_DIM).reshape(1, SSM_INNER)
    lb = [hgrn_lb[dd] for dd in (0, 1)]
    wbs = w_branch_ssm[0].astype(BF16)
    wbh = w_branch_hgrn[0].astype(BF16)
    wo = w_out[0].astype(BF16)
    wr = jnp.pad(w_router[0], ((0, 0), (0, LANES - N_EXPERTS)))
    br = jnp.pad(b_router[0].reshape(1, N_EXPERTS), ((0, 0), (0, LANES - N_EXPERTS)))
    row = lambda v: v.reshape(1, -1)

    cc = jnp.zeros((8, d), F32).at[0].set(c[0]).at[1].set(c_ctx)
    mod = _modulation(cc, w_ada[0], b_ada[0])
    sh1, sc1, g1, sh2, sc2, g2 = [mod[0:1, k * d:(k + 1) * d] for k in range(6)]
    csh1, csc1 = mod[1:2, 0:d], mod[1:2, d:2 * d]

    ssm_vecs = (dtb, alog, amask)
    zs = jnp.zeros((SSM_GROUPS, SSM_STATE, SSM_GROUP_W), F32)
    zh = jnp.zeros((HGRN_HEADS, HGRN_V, HGRN_K), F32)
    nw1 = row(norm1_w[0])

    _, ssd_c, hgrn_c = _mixer_scans(ctx[0], nw1, csh1, csc1, w_main, w_dt, conv_w[0], conv_b[0], ssm_vecs,
                                    rep, lb, ctx_len, (zs, zs), (zh, zh))
    ssd_init = (ssd_c[0][1], ssd_c[1][1])
    hgrn_init = (hgrn_c[0][1], hgrn_c[1][1])

    x2 = x[0]
    proj, ssd_l, hgrn_l = _mixer_scans(x2, nw1, sh1, sc1, w_main, w_dt, conv_w[0], conv_b[0], ssm_vecs,
                                       rep, lb, GRID_W, ssd_init, hgrn_init, ssd_post=(dsk, row(ssm_norm_w[0])))
    y_ssm = ssd_l[1][0]
    y_hgrn = _hgrn_post(hgrn_l[0][0], hgrn_l[1][0], proj, row(hgrn_norm_w[0]))
    y = _merge(y_ssm, y_hgrn, proj, wbs, wbh)
    h, hn2, logits = _outproj(y, x2, wo, g1, row(norm2_w[0]), sh2, sc2, wr, br)

    idx, pos, p, counts = _route(logits)
    counts = counts[0, :N_EXPERTS].astype(I32)
    bm = MOE_BM
    n_assign = seq * TOP_K
    n_blocks = -(-(n_assign + N_EXPERTS * (bm - 1)) // bm)
    n_slots = n_blocks * bm
    padded = (counts + bm - 1) // bm * bm
    pad_end = jnp.cumsum(padded)
    pad_start = pad_end - padded
    slots = (pad_start[idx[:, :TOP_K]] + pos[:, :TOP_K]).reshape(-1).astype(I32)
    blk_start = jnp.arange(n_blocks, dtype=I32) * bm
    blk_e = jnp.minimum(jnp.sum(blk_start[:, None] >= pad_end[None, :], axis=1), N_EXPERTS - 1).astype(I32)
    blk_count = jnp.clip(pad_start[blk_e] + counts[blk_e] - blk_start, 0, bm).astype(I32)
    half_start = jnp.arange(n_slots // MOE_HALF, dtype=I32) * MOE_HALF
    half_e = jnp.repeat(blk_e, bm // MOE_HALF)
    tails = (pad_start[half_e] + counts[half_e] - half_start < MOE_HALF).astype(I32)

    assert d == 2 * HALF_D
    xs_slab = _dispatch(slots, tails, hn2, n_slots)
    yb_slab = _experts(blk_e, blk_count, xs_slab, w_gate_up[0], b_gate_up[0], w_down[0], b_down[0])
    out = _combine(slots, h, p, g2, row(final_norm_w), yb_slab)
    return out.reshape(bsz, seq, d)
```
